```python
import math
import jax, jax.numpy as jnp
from jax import lax
import numpy as np

D_MODEL = 2048
BATCH = 2
SEQ = 4096
DEPTH = 1
DEC_BATCH = 128
DEC_SEQ = 8
PAST_LEN = 16384
PAGE_SIZE = 128

N_HEADS = 32
N_KV_HEADS = 8
HEAD_DIM = 64
Q_PER_KV = N_HEADS // N_KV_HEADS
ATTN_WIDTH = N_HEADS * HEAD_DIM
KV_WIDTH = N_KV_HEADS * HEAD_DIM
WINDOW = 128
SSM_EXPAND = 2
SSM_WIDTH = SSM_EXPAND * D_MODEL
SSM_HEAD_DIM = 64
SSM_HEADS = SSM_WIDTH // SSM_HEAD_DIM
SSM_GROUPS = 8
HEADS_PER_GROUP = SSM_HEADS // SSM_GROUPS
D_STATE = 128
CONV_W = 4
CONV_DIM = SSM_WIDTH + 2 * SSM_GROUPS * D_STATE
SSD_CHUNK = 128
NORM_EPS = 1e-6
IN_SIZES = (ATTN_WIDTH, KV_WIDTH, KV_WIDTH, ATTN_WIDTH, CONV_DIM, SSM_WIDTH, SSM_HEADS, D_MODEL, D_MODEL)
IN_DIM = sum(IN_SIZES)

kernel_name = "hybrid_swa_sink_alibi_mamba2_gated_merge_step"


def _offsets(sizes):
    out, acc = [], 0
    for s in sizes[:-1]:
        acc += s
        out.append(acc)
    return out


def rmsnorm(x, g):
    xf = x.astype(jnp.float32)
    xf = xf * lax.rsqrt(jnp.mean(xf * xf, axis=-1, keepdims=True) + NORM_EPS)
    return (xf * g.astype(jnp.float32)).astype(x.dtype)


def alibi_slopes():
    h = jnp.arange(1, N_HEADS + 1, dtype=jnp.float32)
    return jnp.exp2(-8.0 * h / N_HEADS)


def window_attention(q, k, v, q_pos, k_pos, sinks):
    scale = HEAD_DIM ** -0.5
    s = jnp.einsum('bnqkgd,bnskd->bnkgqs', q, k).astype(jnp.float32) * scale
    dist = q_pos[:, :, None] - k_pos[:, None, :]
    valid = (dist >= 0) & (dist < WINDOW) & (k_pos[:, None, :] >= 0)
    slopes = alibi_slopes().reshape(N_KV_HEADS, Q_PER_KV)
    s = s - slopes[None, None, :, :, None, None] * dist.astype(jnp.float32)[None, :, None, None]
    s = jnp.where(valid[None, :, None, None], s, -jnp.inf)
    sink = sinks.astype(jnp.float32).reshape(N_KV_HEADS, Q_PER_KV)[None, None, :, :, None, None]
    m = jnp.maximum(jnp.max(s, axis=-1, keepdims=True), sink)
    p = jnp.exp(s - m)
    denom = jnp.sum(p, axis=-1, keepdims=True) + jnp.exp(sink - m)
    o = jnp.einsum('bnkgqs,bnskd->bnqkgd', (p / denom).astype(v.dtype), v)
    b, n, lq = o.shape[:3]
    return o.reshape(b, n * lq, ATTN_WIDTH)


def attn_prompt(q, k, v, sinks):
    b, L = q.shape[:2]
    nb = L // WINDOW
    qb = q.reshape(b, nb, WINDOW, N_KV_HEADS, Q_PER_KV, HEAD_DIM)
    kb = k.reshape(b, nb, WINDOW, N_KV_HEADS, HEAD_DIM)
    vb = v.reshape(b, nb, WINDOW, N_KV_HEADS, HEAD_DIM)
    pad = ((0, 0), (1, 0), (0, 0), (0, 0), (0, 0))
    kwin = jnp.concatenate([jnp.pad(kb, pad)[:, :-1], kb], axis=2)
    vwin = jnp.concatenate([jnp.pad(vb, pad)[:, :-1], vb], axis=2)
    q_pos = jnp.arange(L, dtype=jnp.int32).reshape(nb, WINDOW)
    k_pos = (jnp.arange(nb, dtype=jnp.int32) * WINDOW - WINDOW)[:, None] + jnp.arange(2 * WINDOW, dtype=jnp.int32)[None]
    return window_attention(qb, kwin, vwin, q_pos, k_pos, sinks)


def attn_sample(q, all_k, all_v, sinks):
    L = q.shape[1]
    past = all_k.shape[1] - L
    q_pos = (PAST_LEN + jnp.arange(L, dtype=jnp.int32))[None]
    k_pos = (PAST_LEN - past + jnp.arange(past + L, dtype=jnp.int32))[None]
    return window_attention(q[:, None], all_k[:, None], all_v[:, None], q_pos, k_pos, sinks)


def causal_conv(xbc, buf, w, bias):
    full = jnp.concatenate([buf, xbc], axis=1)
    y = lax.conv_general_dilated(full, w.astype(full.dtype)[:, None, :], window_strides=(1,),
                                 padding='VALID', dimension_numbers=('NWC', 'WIO', 'NWC'),
                                 feature_group_count=CONV_DIM)
    return jax.nn.silu(y + bias.astype(y.dtype)), full[:, -(CONV_W - 1):]


def ssd_scan(x, dt, a, bm, cm, s0, chunk):
    b, L = x.shape[:2]
    nc = L // chunk
    G, R, P, N = SSM_GROUPS, HEADS_PER_GROUP, SSM_HEAD_DIM, D_STATE
    xr = x.astype(jnp.float32).reshape(b, nc, chunk, G, R, P)
    dtr = dt.reshape(b, nc, chunk, G, R)
    br = bm.astype(jnp.float32).reshape(b, nc, chunk, G, N)
    cr = cm.astype(jnp.float32).reshape(b, nc, chunk, G, N)
    acum = jnp.cumsum(dtr * a.reshape(G, R), axis=2)
    at = jnp.moveaxis(acum, 2, -1)
    causal = jnp.tril(jnp.ones((chunk, chunk), dtype=bool))
    decay = jnp.exp(jnp.where(causal, at[..., :, None] - at[..., None, :], -jnp.inf))
    cb = jnp.einsum('bclgn,bcsgn->bcgls', cr, br)
    y_intra = jnp.einsum('bcgls,bcgrls,bcsgr,bcsgrp->bclgrp', cb, decay, dtr, xr)
    decay_end = jnp.exp(at[..., -1:] - at)
    states = jnp.einsum('bcsgn,bcgrs,bcsgr,bcsgrp->bcgrpn', br, decay_end, dtr, xr)
    chunk_decay = jnp.exp(at[..., -1])

    def step(s, inp):
        st, dec = inp
        return dec[..., None, None] * s + st, s

    s_init = s0.astype(jnp.float32).reshape(b, G, R, P, N)
    s_final, s_in = lax.scan(step, s_init, (jnp.moveaxis(states, 1, 0), jnp.moveaxis(chunk_decay, 1, 0)))
    s_in = jnp.moveaxis(s_in, 0, 1)
    y_inter = jnp.einsum('bclgn,bcgrpn,bcgrl->bclgrp', cr, s_in, jnp.exp(at))
    y = (y_intra + y_inter).reshape(b, L, SSM_HEADS, P)
    return y, s_final.reshape(b, SSM_HEADS, P, N)


def gated_group_rmsnorm(y, z, g):
    b, L, _ = y.shape
    u = (y.astype(jnp.float32) * jax.nn.silu(z.astype(jnp.float32))).reshape(b, L, SSM_GROUPS, -1)
    u = u * lax.rsqrt(jnp.mean(u * u, axis=-1, keepdims=True) + NORM_EPS)
    return (u.reshape(b, L, SSM_WIDTH) * g.astype(jnp.float32)).astype(z.dtype)


def hybrid_layer(x, win_k, win_v, ssm_state, conv_buf, norm_pre, w_in, conv_w, conv_b, dt_bias,
                 a_log, d_skip, ssm_norm, attn_sinks, w_attn_br, w_ssm_br, w_out, norm_post):
    b, L, _ = x.shape
    prompt = win_k is None
    h = rmsnorm(x, norm_pre)
    q, k, v, z_a, xbc, z_s, dt_raw, g_a, g_s = jnp.split(h @ w_in, _offsets(IN_SIZES), axis=-1)
    q = q.reshape(b, L, N_KV_HEADS, Q_PER_KV, HEAD_DIM)
    k = k.reshape(b, L, N_KV_HEADS, HEAD_DIM)
    v = v.reshape(b, L, N_KV_HEADS, HEAD_DIM)
    if prompt:
        attn = attn_prompt(q, k, v, attn_sinks)
        new_k, new_v = k[:, L - WINDOW:], v[:, L - WINDOW:]
        conv_buf = jnp.zeros((b, CONV_W - 1, CONV_DIM), x.dtype)
        ssm_state = jnp.zeros((b, SSM_HEADS, SSM_HEAD_DIM, D_STATE), jnp.float32)
        chunk = SSD_CHUNK
    else:
        all_k = jnp.concatenate([win_k, k], axis=1)
        all_v = jnp.concatenate([win_v, v], axis=1)
        attn = attn_sample(q, all_k, all_v, attn_sinks)
        new_k, new_v = all_k[:, L:], all_v[:, L:]
        chunk = L
    attn_out = attn * jax.nn.silu(z_a)

    xbc_c, new_conv = causal_conv(xbc, conv_buf, conv_w, conv_b)
    xs, bm, cm = jnp.split(xbc_c, _offsets((SSM_WIDTH, SSM_GROUPS * D_STATE, SSM_GROUPS * D_STATE)), axis=-1)
    xs = xs.reshape(b, L, SSM_HEADS, SSM_HEAD_DIM)
    dt = jax.nn.softplus(dt_raw.astype(jnp.float32) + dt_bias.astype(jnp.float32))
    a = -jnp.exp(a_log.astype(jnp.float32))
    y, s_final = ssd_scan(xs, dt, a, bm.reshape(b, L, SSM_GROUPS, D_STATE),
                          cm.reshape(b, L, SSM_GROUPS, D_STATE), ssm_state, chunk)
    y = y + d_skip.astype(jnp.float32)[:, None] * xs.astype(jnp.float32)
    ssm_out = gated_group_rmsnorm(y.reshape(b, L, SSM_WIDTH), z_s, ssm_norm)

    merged = jax.nn.sigmoid(g_a) * (attn_out @ w_attn_br) + jax.nn.sigmoid(g_s) * (ssm_out @ w_ssm_br)
    out = x + rmsnorm(merged @ w_out, norm_post)
    return out, new_k, new_v, s_final.astype(x.dtype), new_conv


def setup_inputs(seed: int = 0) -> dict:
    key = jax.random.key(seed)
    ks = jax.random.split(key, 20)
    f32 = jnp.float32
    nrm = lambda k, shape, s=1.0: jax.random.normal(k, shape, f32) * s
    dt0 = jnp.exp(jax.random.uniform(ks[10], (DEPTH, SSM_HEADS), f32, math.log(1e-3), math.log(1e-1)))
    return {
        "x_prompt": nrm(ks[0], (BATCH, SEQ, D_MODEL)),
        "x_sample": nrm(ks[1], (DEC_BATCH, DEC_SEQ, D_MODEL)),
        "cache_k": nrm(ks[2], (DEPTH, DEC_BATCH, WINDOW, N_KV_HEADS, HEAD_DIM)),
        "cache_v": nrm(ks[3], (DEPTH, DEC_BATCH, WINDOW, N_KV_HEADS, HEAD_DIM)),
        "state_ssm": nrm(ks[4], (DEPTH, DEC_BATCH, SSM_HEADS, SSM_HEAD_DIM, D_STATE), 0.5),
        "state_conv": nrm(ks[5], (DEPTH, DEC_BATCH, CONV_W - 1, CONV_DIM)),
        "norm_pre": 1.0 + nrm(ks[6], (DEPTH, D_MODEL), 0.05),
        "w_in": nrm(ks[7], (DEPTH, D_MODEL, IN_DIM), D_MODEL ** -0.5),
        "conv_w": nrm(ks[8], (DEPTH, CONV_W, CONV_DIM), CONV_W ** -0.5),
        "conv_b": nrm(ks[9], (DEPTH, CONV_DIM), 0.05),
        "dt_bias": dt0 + jnp.log(-jnp.expm1(-dt0)),
        "a_log": jnp.log(jax.random.uniform(ks[11], (DEPTH, SSM_HEADS), f32, 1.0, 16.0)),
        "d_skip": 1.0 + nrm(ks[12], (DEPTH, SSM_HEADS), 0.1),
        "ssm_norm": 1.0 + nrm(ks[13], (DEPTH, SSM_WIDTH), 0.05),
        "attn_sinks": nrm(ks[14], (DEPTH, N_HEADS), 0.5),
        "w_attn_br": nrm(ks[15], (DEPTH, ATTN_WIDTH, D_MODEL), ATTN_WIDTH ** -0.5),
        "w_ssm_br": nrm(ks[16], (DEPTH, SSM_WIDTH, D_MODEL), SSM_WIDTH ** -0.5),
        "w_out": nrm(ks[17], (DEPTH, D_MODEL, D_MODEL), D_MODEL ** -0.5),
        "norm_post": 1.0 + nrm(ks[18], (DEPTH, D_MODEL), 0.05),
    }


def reference(x_prompt, x_sample, cache_k, cache_v, state_ssm, state_conv, norm_pre, w_in, conv_w,
              conv_b, dt_bias, a_log, d_skip, ssm_norm, attn_sinks, w_attn_br, w_ssm_br, w_out, norm_post):
    yp, ys = x_prompt, x_sample
    outs = []
    for layer in range(DEPTH):
        w = (norm_pre[layer], w_in[layer], conv_w[layer], conv_b[layer], dt_bias[layer], a_log[layer],
             d_skip[layer], ssm_norm[layer], attn_sinks[layer], w_attn_br[layer], w_ssm_br[layer],
             w_out[layer], norm_post[layer])
        yp, kp, vp, sp, cp = hybrid_layer(yp, None, None, None, None, *w)
        ys, kq, vq, sq, cq = hybrid_layer(ys, cache_k[layer], cache_v[layer], state_ssm[layer],
                                          state_conv[layer], *w)
        outs.append((kp, vp, sp, cp, kq, vq, sq, cq))
    k_p, v_p, s_p, c_p, k_s, v_s, s_s, c_s = [jnp.stack(arrs, axis=0) for arrs in zip(*outs)]
    return (yp, ys, k_p, v_p, s_p, c_p, k_s, v_s, s_s, c_s)
```

```python
import functools

import jax
import jax.numpy as jnp
from jax import lax
from jax.experimental import pallas as pl
from jax.experimental.pallas import tpu as pltpu

f32 = jnp.float32
bf16 = jnp.bfloat16

D_MODEL = 2048
SEQ = 4096
DEC_SEQ = 8
N_HEADS = 32
N_KV_HEADS = 8
HEAD_DIM = 64
Q_PER_KV = N_HEADS // N_KV_HEADS
ATTN_WIDTH = N_HEADS * HEAD_DIM
KV_WIDTH = N_KV_HEADS * HEAD_DIM
WINDOW = 128
SSM_WIDTH = 2 * D_MODEL
SSM_HEAD_DIM = 64
SSM_HEADS = SSM_WIDTH // SSM_HEAD_DIM
SSM_GROUPS = 8
HEADS_PER_GROUP = SSM_HEADS // SSM_GROUPS
GROUP_WIDTH = HEADS_PER_GROUP * SSM_HEAD_DIM
D_STATE = 128
CONV_W = 4
BC_WIDTH = SSM_GROUPS * D_STATE
CONV_DIM = SSM_WIDTH + 2 * BC_WIDTH
CHUNK = 128
NORM_EPS = 1e-6

_SRC = dict(q=0, k=2048, v=2560, z_a=3072, xbc=5120, z_s=11264, dt=15360, g_a=15424, g_s=17472)
_COL = dict(q=0, z_a=2048, g_a=4096, g_s=6144, z_s=8192, x=12288, B=16384, C=17408, k=18432, v=18944)
P_WIDTH = 19456
LANES = 128
SEQS_PER_STEP = 16

_VMEM_LIMIT = 56 * 1024 * 1024


def _params(sem):
    return pltpu.CompilerParams(dimension_semantics=sem, vmem_limit_bytes=_VMEM_LIMIT)


def _silu(v):
    return v * jax.nn.sigmoid(v)


def _div(v, n):
    assert n & (n - 1) == 0
    return v >> (n.bit_length() - 1)


def _mod(v, n):
    assert n & (n - 1) == 0
    return v & (n - 1)


_NORM_ROWS = 128


def _inproj_kernel(x_ref, g_ref, w_ref, wdt_ref, dtb_ref, p_ref, dt_ref, h_ref):
    @pl.when(pl.program_id(1) == 0)
    def _():
        def norm_rows(i, carry):
            rows = pl.ds(pl.multiple_of(i * _NORM_ROWS, _NORM_ROWS), _NORM_ROWS)
            x = x_ref[rows, :]
            ms = jnp.mean(x * x, axis=-1, keepdims=True)
            h = (x * lax.rsqrt(ms + NORM_EPS) * g_ref[...]).astype(bf16)
            h_ref[rows, :] = h
            v = jnp.dot(h, wdt_ref[...], preferred_element_type=f32) + dtb_ref[...]
            dt_ref[rows, :] = jnp.maximum(v, 0.0) + jnp.log1p(jnp.exp(-jnp.abs(v)))
            return carry

        lax.fori_loop(0, x_ref.shape[0] // _NORM_ROWS, norm_rows, 0)

    p_ref[...] = jnp.dot(h_ref[...], w_ref[...], preferred_element_type=f32)


def _inproj(x, g, w, wdt, dtb, tm=1024, tn=1024):
    m = x.shape[0]
    return pl.pallas_call(
        _inproj_kernel,
        grid=(m // tm, P_WIDTH // tn),
        in_specs=[
            pl.BlockSpec((tm, D_MODEL), lambda i, j: (i, 0)),
            pl.BlockSpec((1, D_MODEL), lambda i, j: (0, 0)),
            pl.BlockSpec((D_MODEL, tn), lambda i, j: (0, j)),
            pl.BlockSpec((D_MODEL, LANES), lambda i, j: (0, 0)),
            pl.BlockSpec((1, LANES), lambda i, j: (0, 0)),
        ],
        out_specs=[
            pl.BlockSpec((tm, tn), lambda i, j: (i, j)),
            pl.BlockSpec((tm, LANES), lambda i, j: (i, 0)),
        ],
        out_shape=[jax.ShapeDtypeStruct((m, P_WIDTH), f32), jax.ShapeDtypeStruct((m, LANES), f32)],
        scratch_shapes=[pltpu.VMEM((tm, D_MODEL), bf16)],
        compiler_params=_params(("parallel", "arbitrary")),
        name="inproj",
    )(x, g, w, wdt, dtb)


def _alibi_slope(h):
    return 2.0 ** (-8.0 * (h + 1) / N_HEADS)


def _per_head_column(rows_per_head, values):
    g = _div(lax.broadcasted_iota(jnp.int32, (Q_PER_KV * rows_per_head, 1), 0), rows_per_head)
    col = jnp.full(g.shape, values[Q_PER_KV - 1], f32)
    for i in range(Q_PER_KV - 2, -1, -1):
        col = jnp.where(g == i, values[i], col)
    return col


def _attend(q, k, v, dist, valid, slopes, sinks):
    s = lax.dot_general(q, k, (((1,), (1,)), ((), ())), preferred_element_type=f32) * (HEAD_DIM ** -0.5)
    s = jnp.where(valid, s - slopes * dist, -jnp.inf)
    m = jnp.maximum(jnp.max(s, axis=-1, keepdims=True), sinks)
    p = jnp.exp(s - m)
    denom = jnp.sum(p, axis=-1, keepdims=True) + jnp.exp(sinks - m)
    return jnp.dot(p.astype(bf16), v, preferred_element_type=f32) / denom


def _attn_prompt_kernel(sink_ref, q_ref, kc_ref, kp_ref, vc_ref, vp_ref, za_ref, o_ref):
    i = pl.program_id(1)
    shape = (Q_PER_KV * WINDOW, 2 * WINDOW)
    row = lax.broadcasted_iota(jnp.int32, shape, 0)
    col = lax.broadcasted_iota(jnp.int32, shape, 1)
    dist = WINDOW + _mod(row, WINDOW) - col
    valid = (dist >= 0) & (dist < WINDOW) & ((col >= WINDOW) | (i > 0))
    dist = dist.astype(f32)
    for j in range(N_KV_HEADS):
        heads = [j * Q_PER_KV + g for g in range(Q_PER_KV)]
        q = jnp.concatenate(
            [q_ref[:, h * HEAD_DIM:(h + 1) * HEAD_DIM] for h in heads], axis=0).astype(bf16)
        ks = slice(j * HEAD_DIM, (j + 1) * HEAD_DIM)
        k = jnp.concatenate([kp_ref[:, ks], kc_ref[:, ks]], axis=0).astype(bf16)
        v = jnp.concatenate([vp_ref[:, ks], vc_ref[:, ks]], axis=0).astype(bf16)
        slopes = _per_head_column(WINDOW, [_alibi_slope(h) for h in heads])
        sinks = _per_head_column(WINDOW, [sink_ref[h] for h in heads])
        o = _attend(q, k, v, dist, valid, slopes, sinks)
        o = jnp.concatenate([o[g * WINDOW:(g + 1) * WINDOW] for g in range(Q_PER_KV)], axis=1)
        cs = slice(j * Q_PER_KV * HEAD_DIM, (j + 1) * Q_PER_KV * HEAD_DIM)
        o_ref[:, cs] = (o * _silu(za_ref[:, cs])).astype(o_ref.dtype)


def _attn_prompt(p, sinks, batch):
    nb = SEQ // WINDOW
    kcol, vcol = _COL["k"] // KV_WIDTH, _COL["v"] // KV_WIDTH
    cur = lambda b, i: b * nb + i
    prev = lambda b, i: b * nb + jnp.maximum(i - 1, 0)
    return pl.pallas_call(
        _attn_prompt_kernel,
        grid=(batch, nb),
        in_specs=[
            pl.BlockSpec(memory_space=pltpu.SMEM),
            pl.BlockSpec((WINDOW, ATTN_WIDTH), lambda b, i: (cur(b, i), _COL["q"] // ATTN_WIDTH)),
            pl.BlockSpec((WINDOW, KV_WIDTH), lambda b, i: (cur(b, i), kcol)),
            pl.BlockSpec((WINDOW, KV_WIDTH), lambda b, i: (prev(b, i), kcol)),
            pl.BlockSpec((WINDOW, KV_WIDTH), lambda b, i: (cur(b, i), vcol)),
            pl.BlockSpec((WINDOW, KV_WIDTH), lambda b, i: (prev(b, i), vcol)),
            pl.BlockSpec((WINDOW, ATTN_WIDTH), lambda b, i: (cur(b, i), _COL["z_a"] // ATTN_WIDTH)),
        ],
        out_specs=pl.BlockSpec((WINDOW, ATTN_WIDTH), lambda b, i: (cur(b, i), 0)),
        out_shape=jax.ShapeDtypeStruct((batch * SEQ, ATTN_WIDTH), bf16),
        compiler_params=_params(("parallel", "parallel")),
        name="attn_prompt",
    )(sinks, p, p, p, p, p, p)


_ATTN_SEQS = 8


def _attn_sample_kernel(sink_ref, q_ref, kn_ref, vn_ref, za_ref, ck_ref, cv_ref,
                        o_ref, ko_ref, vo_ref, ak_ref, av_ref):
    rows = Q_PER_KV * DEC_SEQ
    keys = 2 * WINDOW
    row = lax.broadcasted_iota(jnp.int32, (rows, keys), 0)
    col = lax.broadcasted_iota(jnp.int32, (rows, keys), 1)
    dist = WINDOW + _mod(row, DEC_SEQ) - col
    valid = (dist >= 0) & (dist < WINDOW) & (col < WINDOW + DEC_SEQ)
    dist = dist.astype(f32)
    pad = jnp.zeros((keys - WINDOW - DEC_SEQ, KV_WIDTH), f32)
    ak_ref[WINDOW + DEC_SEQ:, :] = pad
    av_ref[WINDOW + DEC_SEQ:, :] = pad

    def one_sequence(n, carry):
        new = pl.ds(pl.multiple_of(n * DEC_SEQ, DEC_SEQ), DEC_SEQ)
        for cache_ref, new_ref, all_ref, out_ref in ((ck_ref, kn_ref, ak_ref, ko_ref),
                                                     (cv_ref, vn_ref, av_ref, vo_ref)):
            all_ref[:WINDOW, :] = cache_ref[n]
            all_ref[WINDOW:WINDOW + DEC_SEQ, :] = new_ref[new, :]
            out_ref[n] = all_ref[DEC_SEQ:WINDOW + DEC_SEQ, :]
        for j in range(N_KV_HEADS):
            heads = [j * Q_PER_KV + g for g in range(Q_PER_KV)]
            q = jnp.concatenate(
                [q_ref[new, h * HEAD_DIM:(h + 1) * HEAD_DIM] for h in heads], axis=0).astype(bf16)
            ks = slice(j * HEAD_DIM, (j + 1) * HEAD_DIM)
            slopes = _per_head_column(DEC_SEQ, [_alibi_slope(h) for h in heads])
            sinks = _per_head_column(DEC_SEQ, [sink_ref[h] for h in heads])
            o = _attend(q, ak_ref[:, ks].astype(bf16), av_ref[:, ks].astype(bf16),
                        dist, valid, slopes, sinks)
            o = jnp.concatenate([o[g * DEC_SEQ:(g + 1) * DEC_SEQ] for g in range(Q_PER_KV)], axis=1)
            cs = slice(j * Q_PER_KV * HEAD_DIM, (j + 1) * Q_PER_KV * HEAD_DIM)
            o_ref[new, cs] = o * _silu(za_ref[new, cs])
        return carry

    lax.fori_loop(0, _ATTN_SEQS, one_sequence, 0)


def _attn_sample(p, sinks, cache_k, cache_v):
    nseq = cache_k.shape[0]
    rows = _ATTN_SEQS * DEC_SEQ
    cache_spec = pl.BlockSpec((_ATTN_SEQS, WINDOW, KV_WIDTH), lambda s: (s, 0, 0))
    return pl.pallas_call(
        _attn_sample_kernel,
        grid=(nseq // _ATTN_SEQS,),
        in_specs=[
            pl.BlockSpec(memory_space=pltpu.SMEM),
            pl.BlockSpec((rows, ATTN_WIDTH), lambda s: (s, _COL["q"] // ATTN_WIDTH)),
            pl.BlockSpec((rows, KV_WIDTH), lambda s: (s, _COL["k"] // KV_WIDTH)),
            pl.BlockSpec((rows, KV_WIDTH), lambda s: (s, _COL["v"] // KV_WIDTH)),
            pl.BlockSpec((rows, ATTN_WIDTH), lambda s: (s, _COL["z_a"] // ATTN_WIDTH)),
            cache_spec, cache_spec,
        ],
        out_specs=[pl.BlockSpec((rows, ATTN_WIDTH), lambda s: (s, 0)), cache_spec, cache_spec],
        out_shape=[
            jax.ShapeDtypeStruct((nseq * DEC_SEQ, ATTN_WIDTH), f32),
            jax.ShapeDtypeStruct(cache_k.shape, f32),
            jax.ShapeDtypeStruct(cache_v.shape, f32),
        ],
        scratch_shapes=[pltpu.VMEM((2 * WINDOW, KV_WIDTH), f32), pltpu.VMEM((2 * WINDOW, KV_WIDTH), f32)],
        compiler_params=_params(("parallel",)),
        name="attn_sample",
    )(sinks, p, p, p, p, cache_k, cache_v)


def _expand_heads(v):
    rows = v.shape[0]
    head = _div(lax.broadcasted_iota(jnp.int32, (rows, GROUP_WIDTH), 1), SSM_HEAD_DIM)
    out = jnp.zeros((rows, GROUP_WIDTH), f32)
    for r in range(HEADS_PER_GROUP):
        out = jnp.where(head == r, v[:, r:r + 1], out)
    return out


def _segmented_cumsum(v, axis, segment):
    pos = _mod(lax.broadcasted_iota(jnp.int32, v.shape, axis), segment)
    step = 1
    while step < segment:
        v = v + jnp.where(pos >= step, pltpu.roll(v, step, axis), 0.0)
        step *= 2
    return v


def _intra_chunk(cb, causal, acum_c, acum_r, xdt):
    lane = lax.broadcasted_iota(jnp.int32, (CHUNK, LANES), 1)
    pieces = []
    for pair in range(HEADS_PER_GROUP // 2):
        x2 = xdt[:, pair * LANES:(pair + 1) * LANES]
        acc = jnp.zeros((CHUNK, LANES), f32)
        for half in range(2):
            r = 2 * pair + half
            keep = (lane < SSM_HEAD_DIM) if half == 0 else (lane >= SSM_HEAD_DIM)
            decay = jnp.exp(jnp.where(causal, acum_c[:, r:r + 1] - acum_r[r:r + 1, :], -jnp.inf))
            acc += jnp.dot((cb * decay).astype(bf16), jnp.where(keep, x2, 0.0).astype(bf16),
                           preferred_element_type=f32)
        pieces.append(acc)
    return jnp.concatenate(pieces, axis=1)


def _gated_norm(y, z, gain):
    u = y * _silu(z)
    ms = jnp.mean(u * u, axis=-1, keepdims=True)
    return u * lax.rsqrt(ms + NORM_EPS) * gain


def _conv_silu(cur, shifted, w_ref, b_ref):
    y = b_ref[...] + w_ref[CONV_W - 1:CONV_W, :] * cur
    for k in range(1, CONV_W):
        y = y + w_ref[CONV_W - 1 - k:CONV_W - k, :] * shifted[k - 1]
    return _silu(y)


def _ssd_prompt_kernel(x_ref, b_ref, c_ref, z_ref, dtc_ref, dtr_ref, alc_ref, alr_ref,
                       wx_ref, wb_ref, wc_ref, bx_ref, bb_ref, bc_ref, d_ref, gain_ref,
                       y_ref, st_ref, xpad, bpad, cpad, state):
    c = pl.program_id(2)
    tail = 8

    @pl.when(c == 0)
    def _():
        state[...] = jnp.zeros_like(state)
        for pad in (xpad, bpad, cpad):
            pad[:tail, :] = jnp.zeros((tail, pad.shape[1]), f32)

    @pl.when(c > 0)
    def _():
        for pad in (xpad, bpad, cpad):
            pad[:tail, :] = pad[CHUNK:CHUNK + tail, :]

    conv = []
    for raw_ref, pad, w_ref, bias_ref in ((x_ref, xpad, wx_ref, bx_ref), (b_ref, bpad, wb_ref, bb_ref),
                                          (c_ref, cpad, wc_ref, bc_ref)):
        pad[tail:, :] = raw_ref[...]
        shifted = [pad[tail - k:tail - k + CHUNK, :] for k in range(1, CONV_W)]
        conv.append(_conv_silu(raw_ref[...], shifted, w_ref, bias_ref))
    xc, bm, cm = conv

    dt_c, dt_r = dtc_ref[...], dtr_ref[...]
    acum_c = _segmented_cumsum(dt_c * -jnp.exp(alc_ref[...]), 0, CHUNK)
    acum_r = _segmented_cumsum(dt_r * -jnp.exp(alr_ref[...]), 1, CHUNK)
    row = lax.broadcasted_iota(jnp.int32, (CHUNK, CHUNK), 0)
    col = lax.broadcasted_iota(jnp.int32, (CHUNK, CHUNK), 1)
    cb = lax.dot_general(cm.astype(bf16), bm.astype(bf16), (((1,), (1,)), ((), ())),
                         preferred_element_type=f32)
    xdt = xc * _expand_heads(dt_c)
    y = _intra_chunk(cb, row >= col, acum_c, acum_r, xdt)

    s_in = state[...]
    decay_in = _expand_heads(jnp.exp(acum_c))
    y += decay_in * jnp.dot(cm.astype(bf16), s_in.astype(bf16), preferred_element_type=f32)
    decay_out = _expand_heads(jnp.exp(acum_c[CHUNK - 1:CHUNK, :] - acum_c))
    s_out = decay_in[CHUNK - 1:CHUNK, :] * s_in + jnp.dot(
        bm.T.astype(bf16), (xdt * decay_out).astype(bf16), preferred_element_type=f32)
    state[...] = s_out

    y += d_ref[...] * xc
    y_ref[...] = _gated_norm(y, z_ref[...], gain_ref[...]).astype(y_ref.dtype)

    @pl.when(c == pl.num_programs(2) - 1)
    def _():
        st_ref[...] = s_out.T


def _ssd_prompt(p, dt_c, dt_r, al_c, al_r, wx, wb, wc, bx, bb, bc, d_exp, gain, batch):
    nc = SEQ // CHUNK
    rb = lambda b, g, c: b * nc + c
    xcol, zcol = _COL["x"] // GROUP_WIDTH, _COL["z_s"] // GROUP_WIDTH
    bcol, ccol = _COL["B"] // D_STATE, _COL["C"] // D_STATE
    per_group = lambda shape: pl.BlockSpec(shape, lambda b, g, c: (0, g))
    return pl.pallas_call(
        _ssd_prompt_kernel,
        grid=(batch, SSM_GROUPS, nc),
        in_specs=[
            pl.BlockSpec((CHUNK, GROUP_WIDTH), lambda b, g, c: (rb(b, g, c), xcol + g)),
            pl.BlockSpec((CHUNK, D_STATE), lambda b, g, c: (rb(b, g, c), bcol + g)),
            pl.BlockSpec((CHUNK, D_STATE), lambda b, g, c: (rb(b, g, c), ccol + g)),
            pl.BlockSpec((CHUNK, GROUP_WIDTH), lambda b, g, c: (rb(b, g, c), zcol + g)),
            pl.BlockSpec((None, CHUNK, HEADS_PER_GROUP), lambda b, g, c: (g, rb(b, g, c), 0)),
            pl.BlockSpec((None, HEADS_PER_GROUP, CHUNK), lambda b, g, c: (g, 0, rb(b, g, c))),
            pl.BlockSpec((None, 1, HEADS_PER_GROUP), lambda b, g, c: (g, 0, 0)),
            pl.BlockSpec((None, HEADS_PER_GROUP, 1), lambda b, g, c: (g, 0, 0)),
            per_group((CONV_W, GROUP_WIDTH)), per_group((CONV_W, D_STATE)), per_group((CONV_W, D_STATE)),
            per_group((1, GROUP_WIDTH)), per_group((1, D_STATE)), per_group((1, D_STATE)),
            per_group((1, GROUP_WIDTH)), per_group((1, GROUP_WIDTH)),
        ],
        out_specs=[
            pl.BlockSpec((CHUNK, GROUP_WIDTH), lambda b, g, c: (rb(b, g, c), g)),
            pl.BlockSpec((None, GROUP_WIDTH, D_STATE), lambda b, g, c: (b, g, 0)),
        ],
        out_shape=[
            jax.ShapeDtypeStruct((batch * SEQ, SSM_WIDTH), bf16),
            jax.ShapeDtypeStruct((batch, SSM_WIDTH, D_STATE), f32),
        ],
        scratch_shapes=[
            pltpu.VMEM((CHUNK + 8, GROUP_WIDTH), f32),
            pltpu.VMEM((CHUNK + 8, D_STATE), f32),
            pltpu.VMEM((CHUNK + 8, D_STATE), f32),
            pltpu.VMEM((D_STATE, GROUP_WIDTH), f32),
        ],
        compiler_params=_params(("parallel", "parallel", "arbitrary")),
        name="ssd_prompt",
    )(p, p, p, p, dt_c, dt_r, al_c, al_r, wx, wb, wc, bx, bb, bc, d_exp, gain)


def _ssd_sample_kernel(x_ref, b_ref, c_ref, z_ref, dtc_ref, dtr_ref, alc_ref, alr_ref,
                       wx_ref, wb_ref, wc_ref, bx_ref, bb_ref, bc_ref, d_ref, gain_ref,
                       cx_ref, cbuf_ref, cc_ref, s0_ref, y_ref, s1_ref, xprev, bprev, cprev):
    row1 = lax.broadcasted_iota(jnp.int32, (CHUNK, 1), 0)
    conv = []
    for raw_ref, buf_ref, prev, w_ref, bias_ref in (
            (x_ref, cx_ref, xprev, wx_ref, bx_ref), (b_ref, cbuf_ref, bprev, wb_ref, bb_ref),
            (c_ref, cc_ref, cprev, wc_ref, bc_ref)):
        prev[...] = jnp.zeros_like(prev)
        prev[:, :CONV_W - 1, :] = buf_ref[...]
        cached = prev[...].reshape(CHUNK, prev.shape[2])
        cur = raw_ref[...]
        shifted = []
        for k in range(1, CONV_W):
            from_cache = pltpu.roll(cached, (k - (CONV_W - 1)) % CHUNK, 0)
            shifted.append(jnp.where(_mod(row1, DEC_SEQ) >= k, pltpu.roll(cur, k, 0), from_cache))
        conv.append(_conv_silu(cur, shifted, w_ref, bias_ref))
    xc, bm, cm = conv

    dt_c, dt_r = dtc_ref[...], dtr_ref[...]
    acum_c = _segmented_cumsum(dt_c * -jnp.exp(alc_ref[...]), 0, DEC_SEQ)
    acum_r = _segmented_cumsum(dt_r * -jnp.exp(alr_ref[...]), 1, DEC_SEQ)
    row = lax.broadcasted_iota(jnp.int32, (CHUNK, CHUNK), 0)
    col = lax.broadcasted_iota(jnp.int32, (CHUNK, CHUNK), 1)
    same_seq = _div(row, DEC_SEQ) == _div(col, DEC_SEQ)
    cb = lax.dot_general(cm.astype(bf16), bm.astype(bf16), (((1,), (1,)), ((), ())),
                         preferred_element_type=f32)
    xdt = xc * _expand_heads(dt_c)
    y = _intra_chunk(cb, same_seq & (row >= col), acum_c, acum_r, xdt)

    last = _mod(row1, DEC_SEQ) == DEC_SEQ - 1
    total = jnp.where(last, acum_c, 0.0)
    step = 1
    while step < DEC_SEQ:
        total = total + pltpu.roll(total, CHUNK - step, 0)
        step *= 2
    decay_in = _expand_heads(jnp.exp(acum_c))
    xw = (xdt * _expand_heads(jnp.exp(total - acum_c))).T.astype(bf16)
    keep = _expand_heads(jnp.exp(total)).T
    seq_of_row = _div(row1, DEC_SEQ)
    lane = lax.broadcasted_iota(jnp.int32, (GROUP_WIDTH, CHUNK), 1)
    inter = jnp.zeros((CHUNK, GROUP_WIDTH), f32)
    for n in range(SEQS_PER_STEP):
        mine = seq_of_row == n
        s0 = s0_ref[n]
        inter += lax.dot_general(jnp.where(mine, cm, 0.0).astype(bf16), s0.astype(bf16),
                                 (((1,), (1,)), ((), ())), preferred_element_type=f32)
        keep_n = jnp.sum(jnp.where(lane == n * DEC_SEQ, keep, 0.0), axis=1, keepdims=True)
        s1_ref[n] = keep_n * s0 + jnp.dot(xw, jnp.where(mine, bm, 0.0).astype(bf16),
                                          preferred_element_type=f32)
    y += decay_in * inter + d_ref[...] * xc
    y_ref[...] = _gated_norm(y, z_ref[...], gain_ref[...]).astype(y_ref.dtype)


def _ssd_sample(p, dt_c, dt_r, al_c, al_r, wx, wb, wc, bx, bb, bc, d_exp, gain, conv_state, s0):
    nseq = s0.shape[0]
    nsb = nseq // SEQS_PER_STEP
    xcol, zcol = _COL["x"] // GROUP_WIDTH, _COL["z_s"] // GROUP_WIDTH
    bcol, ccol = _COL["B"] // D_STATE, _COL["C"] // D_STATE
    per_group = lambda shape: pl.BlockSpec(shape, lambda s, g: (0, g))
    conv_block = lambda width, first: pl.BlockSpec(
        (SEQS_PER_STEP, CONV_W - 1, width), lambda s, g: (s, 0, first + g))
    state_spec = pl.BlockSpec((SEQS_PER_STEP, GROUP_WIDTH, D_STATE), lambda s, g: (s, g, 0))
    return pl.pallas_call(
        _ssd_sample_kernel,
        grid=(nsb, SSM_GROUPS),
        in_specs=[
            pl.BlockSpec((CHUNK, GROUP_WIDTH), lambda s, g: (s, xcol + g)),
            pl.BlockSpec((CHUNK, D_STATE), lambda s, g: (s, bcol + g)),
            pl.BlockSpec((CHUNK, D_STATE), lambda s, g: (s, ccol + g)),
            pl.BlockSpec((CHUNK, GROUP_WIDTH), lambda s, g: (s, zcol + g)),
            pl.BlockSpec((None, CHUNK, HEADS_PER_GROUP), lambda s, g: (g, s, 0)),
            pl.BlockSpec((None, HEADS_PER_GROUP, CHUNK), lambda s, g: (g, 0, s)),
            pl.BlockSpec((None, 1, HEADS_PER_GROUP), lambda s, g: (g, 0, 0)),
            pl.BlockSpec((None, HEADS_PER_GROUP, 1), lambda s, g: (g, 0, 0)),
            per_group((CONV_W, GROUP_WIDTH)), per_group((CONV_W, D_STATE)), per_group((CONV_W, D_STATE)),
            per_group((1, GROUP_WIDTH)), per_group((1, D_STATE)), per_group((1, D_STATE)),
            per_group((1, GROUP_WIDTH)), per_group((1, GROUP_WIDTH)),
            conv_block(GROUP_WIDTH, 0),
            conv_block(D_STATE, SSM_WIDTH // D_STATE),
            conv_block(D_STATE, (SSM_WIDTH + BC_WIDTH) // D_STATE),
            state_spec,
        ],
        out_specs=[pl.BlockSpec((CHUNK, GROUP_WIDTH), lambda s, g: (s, g)), state_spec],
        out_shape=[
            jax.ShapeDtypeStruct((nseq * DEC_SEQ, SSM_WIDTH), bf16),
            jax.ShapeDtypeStruct(s0.shape, f32),
        ],
        scratch_shapes=[
            pltpu.VMEM((SEQS_PER_STEP, 8, GROUP_WIDTH), f32),
            pltpu.VMEM((SEQS_PER_STEP, 8, D_STATE), f32),
            pltpu.VMEM((SEQS_PER_STEP, 8, D_STATE), f32),
        ],
        compiler_params=_params(("parallel", "parallel")),
        name="ssd_sample",
    )(p, p, p, p, dt_c, dt_r, al_c, al_r, wx, wb, wc, bx, bb, bc, d_exp, gain,
      conv_state, conv_state, conv_state, s0)


def _merge_kernel(a_ref, s_ref, ga_ref, gs_ref, wa_ref, ws_ref, o_ref):
    ya = jnp.dot(a_ref[...].astype(bf16), wa_ref[...], preferred_element_type=f32)
    ys = jnp.dot(s_ref[...], ws_ref[...], preferred_element_type=f32)
    o_ref[...] = (jax.nn.sigmoid(ga_ref[...]) * ya + jax.nn.sigmoid(gs_ref[...]) * ys).astype(o_ref.dtype)


def _merge(attn, ssm, p, wa, ws, tm=1024, tn=512):
    m = attn.shape[0]
    return pl.pallas_call(
        _merge_kernel,
        grid=(m // tm, D_MODEL // tn),
        in_specs=[
            pl.BlockSpec((tm, ATTN_WIDTH), lambda i, j: (i, 0)),
            pl.BlockSpec((tm, SSM_WIDTH), lambda i, j: (i, 0)),
            pl.BlockSpec((tm, tn), lambda i, j: (i, _COL["g_a"] // tn + j)),
            pl.BlockSpec((tm, tn), lambda i, j: (i, _COL["g_s"] // tn + j)),
            pl.BlockSpec((ATTN_WIDTH, tn), lambda i, j: (0, j)),
            pl.BlockSpec((SSM_WIDTH, tn), lambda i, j: (0, j)),
        ],
        out_specs=pl.BlockSpec((tm, tn), lambda i, j: (i, j)),
        out_shape=jax.ShapeDtypeStruct((m, D_MODEL), bf16),
        compiler_params=_params(("parallel", "arbitrary")),
        name="merge",
    )(attn, ssm, p, p, wa, ws)


def _out_kernel(m_ref, wo_ref, x_ref, g_ref, o_ref):
    y = jnp.dot(m_ref[...], wo_ref[...], preferred_element_type=f32)
    ms = jnp.mean(y * y, axis=-1, keepdims=True)
    o_ref[...] = x_ref[...] + y * lax.rsqrt(ms + NORM_EPS) * g_ref[...]


def _out(merged, wo, x, g, tm=512):
    m = x.shape[0]
    return pl.pallas_call(
        _out_kernel,
        grid=(m // tm,),
        in_specs=[
            pl.BlockSpec((tm, D_MODEL), lambda i: (i, 0)),
            pl.BlockSpec((D_MODEL, D_MODEL), lambda i: (0, 0)),
            pl.BlockSpec((tm, D_MODEL), lambda i: (i, 0)),
            pl.BlockSpec((1, D_MODEL), lambda i: (0, 0)),
        ],
        out_specs=pl.BlockSpec((tm, D_MODEL), lambda i: (i, 0)),
        out_shape=jax.ShapeDtypeStruct((m, D_MODEL), f32),
        compiler_params=_params(("parallel",)),
        name="outproj",
    )(merged, wo, x, g)


def _group_layouts(dt, rows):
    d = dt[:, :SSM_HEADS].reshape(rows, SSM_GROUPS, HEADS_PER_GROUP)
    return d.transpose(1, 0, 2), d.transpose(1, 2, 0)


def _layer(xp, xs, cache_k, cache_v, state_ssm, state_conv, norm_pre, w_in, conv_w, conv_b, dt_bias,
           a_log, d_skip, ssm_norm, attn_sinks, w_attn_br, w_ssm_br, w_out, norm_post):
    batch, nseq = xp.shape[0], xs.shape[0]
    mp, ms = batch * SEQ, nseq * DEC_SEQ
    src = lambda name, width: w_in[:, _SRC[name]:_SRC[name] + width]
    w_main = jnp.concatenate([
        src("q", ATTN_WIDTH), src("z_a", ATTN_WIDTH), src("g_a", D_MODEL), src("g_s", D_MODEL),
        src("z_s", SSM_WIDTH), src("xbc", CONV_DIM), src("k", KV_WIDTH), src("v", KV_WIDTH)],
        axis=1).astype(bf16)
    w_dt = jnp.pad(src("dt", SSM_HEADS), ((0, 0), (0, LANES - SSM_HEADS))).astype(bf16)
    dtb = jnp.pad(dt_bias, (0, LANES - SSM_HEADS)).reshape(1, LANES)
    g_pre = norm_pre.reshape(1, D_MODEL)

    wx, wb, wc = conv_w[:, :SSM_WIDTH], conv_w[:, SSM_WIDTH:SSM_WIDTH + BC_WIDTH], conv_w[:, SSM_WIDTH + BC_WIDTH:]
    cb2 = conv_b.reshape(1, CONV_DIM)
    bx, bb, bc = cb2[:, :SSM_WIDTH], cb2[:, SSM_WIDTH:SSM_WIDTH + BC_WIDTH], cb2[:, SSM_WIDTH + BC_WIDTH:]
    al_c = a_log.reshape(SSM_GROUPS, 1, HEADS_PER_GROUP)
    al_r = a_log.reshape(SSM_GROUPS, HEADS_PER_GROUP, 1)
    d_exp = jnp.repeat(d_skip, SSM_HEAD_DIM).reshape(1, SSM_WIDTH)
    gain = ssm_norm.reshape(1, SSM_WIDTH)
    ssd_consts = (al_c, al_r, wx, wb, wc, bx, bb, bc, d_exp, gain)
    wa, ws, wo = w_attn_br.astype(bf16), w_ssm_br.astype(bf16), w_out.astype(bf16)
    g_post = norm_post.reshape(1, D_MODEL)

    xp2, xs2 = xp.reshape(mp, D_MODEL), xs.reshape(ms, D_MODEL)
    pp, dtp = _inproj(xp2, g_pre, w_main, w_dt, dtb)
    ps, dts = _inproj(xs2, g_pre, w_main, w_dt, dtb)

    attn_p = _attn_prompt(pp, attn_sinks, batch)
    ssm_p, st_p = _ssd_prompt(pp, *_group_layouts(dtp, mp), *ssd_consts, batch)
    yp = _out(_merge(attn_p, ssm_p, pp, wa, ws), wo, xp2, g_post)
    pp3 = pp.reshape(batch, SEQ, P_WIDTH)
    k_p = pp3[:, SEQ - WINDOW:, _COL["k"]:_COL["k"] + KV_WIDTH]
    v_p = pp3[:, SEQ - WINDOW:, _COL["v"]:_COL["v"] + KV_WIDTH]
    conv_p = pp3[:, SEQ - (CONV_W - 1):, _COL["x"]:_COL["x"] + CONV_DIM]

    attn_s, k_s, v_s = _attn_sample(ps, attn_sinks, cache_k.reshape(nseq, WINDOW, KV_WIDTH),
                                    cache_v.reshape(nseq, WINDOW, KV_WIDTH))
    ssm_s, st_s = _ssd_sample(ps, *_group_layouts(dts, ms), *ssd_consts, state_conv,
                              state_ssm.reshape(nseq, SSM_WIDTH, D_STATE))
    ys = _out(_merge(attn_s, ssm_s, ps, wa, ws), wo, xs2, g_post)
    conv_s = ps.reshape(nseq, DEC_SEQ, P_WIDTH)[:, DEC_SEQ - (CONV_W - 1):, _COL["x"]:_COL["x"] + CONV_DIM]

    kv = lambda t, n: t.reshape(1, n, WINDOW, N_KV_HEADS, HEAD_DIM)
    st = lambda t, n: t.reshape(1, n, SSM_HEADS, SSM_HEAD_DIM, D_STATE)
    return (yp.reshape(xp.shape), ys.reshape(xs.shape), kv(k_p, batch), kv(v_p, batch), st(st_p, batch),
            conv_p[None], kv(k_s, nseq), kv(v_s, nseq), st(st_s, nseq), conv_s[None])


def kernel(x_prompt, x_sample, cache_k, cache_v, state_ssm, state_conv, norm_pre, w_in, conv_w, conv_b,
           dt_bias, a_log, d_skip, ssm_norm, attn_sinks, w_attn_br, w_ssm_br, w_out, norm_post):
    assert w_in.shape[0] == 1, "single-layer trunk"
    return _layer(x_prompt, x_sample, cache_k[0], cache_v[0], state_ssm[0], state_conv[0], norm_pre[0],
                  w_in[0], conv_w[0], conv_b[0], dt_bias[0], a_log[0], d_skip[0], ssm_norm[0],
                  attn_sinks[0], w_attn_br[0], w_ssm_br[0], w_out[0], norm_post[0])
```

```python
import jax
import jax.numpy as jnp
from jax import lax
from jax.experimental import pallas as pl
from jax.experimental.pallas import tpu as pltpu

f32 = jnp.float32
bf16 = jnp.bfloat16

D_MODEL = 2048
SEQ = 4096
DEC_SEQ = 8
N_HEADS = 32
N_KV_HEADS = 8
HEAD_DIM = 64
Q_PER_KV = N_HEADS // N_KV_HEADS
ATTN_WIDTH = N_HEADS * HEAD_DIM
KV_WIDTH = N_KV_HEADS * HEAD_DIM
WINDOW = 128
SSM_WIDTH = 2 * D_MODEL
SSM_HEAD_DIM = 64
SSM_HEADS = SSM_WIDTH // SSM_HEAD_DIM
SSM_GROUPS = 8
HEADS_PER_GROUP = SSM_HEADS // SSM_GROUPS
GROUP_WIDTH = HEADS_PER_GROUP * SSM_HEAD_DIM
D_STATE = 128
CONV_W = 4
BC_WIDTH = SSM_GROUPS * D_STATE
CONV_DIM = SSM_WIDTH + 2 * BC_WIDTH
CHUNK = 128
NORM_EPS = 1e-6

_SRC = dict(q=0, k=2048, v=2560, z_a=3072, xbc=5120, z_s=11264, dt=15360, g_a=15424, g_s=17472)
_COL = dict(q=0, z_a=2048, g_a=4096, g_s=6144, z_s=8192, x=12288, B=16384, C=17408, k=18432, v=18944)
P_WIDTH = 19456
LANES = 128
SEQS_PER_STEP = 16

_VMEM_LIMIT = 56 * 1024 * 1024


def _params(sem):
    return pltpu.CompilerParams(dimension_semantics=sem, vmem_limit_bytes=_VMEM_LIMIT)


def _silu(v):
    return v * jax.nn.sigmoid(v)


def _div(v, n):
    assert n & (n - 1) == 0
    return v >> (n.bit_length() - 1)


def _mod(v, n):
    assert n & (n - 1) == 0
    return v & (n - 1)


_NORM_ROWS = 128


def _inproj_kernel(x_ref, g_ref, w_ref, wdt_ref, dtb_ref, p_ref, dt_ref, h_ref):
    @pl.when(pl.program_id(1) == 0)
    def _():
        def norm_rows(i, carry):
            rows = pl.ds(pl.multiple_of(i * _NORM_ROWS, _NORM_ROWS), _NORM_ROWS)
            x = x_ref[rows, :]
            ms = jnp.mean(x * x, axis=-1, keepdims=True)
            h = (x * lax.rsqrt(ms + NORM_EPS) * g_ref[...]).astype(bf16)
            h_ref[rows, :] = h
            v = jnp.dot(h, wdt_ref[...], preferred_element_type=f32) + dtb_ref[...]
            dt_ref[rows, :] = jnp.maximum(v, 0.0) + jnp.log1p(jnp.exp(-jnp.abs(v)))
            return carry

        lax.fori_loop(0, x_ref.shape[0] // _NORM_ROWS, norm_rows, 0)

    p_ref[...] = jnp.dot(h_ref[...], w_ref[...], preferred_element_type=f32)


def _inproj(x, g, w, wdt, dtb, tm=1024, tn=1024):
    m = x.shape[0]
    return pl.pallas_call(
        _inproj_kernel,
        grid=(m // tm, P_WIDTH // tn),
        in_specs=[
            pl.BlockSpec((tm, D_MODEL), lambda i, j: (i, 0)),
            pl.BlockSpec((1, D_MODEL), lambda i, j: (0, 0)),
            pl.BlockSpec((D_MODEL, tn), lambda i, j: (0, j)),
            pl.BlockSpec((D_MODEL, LANES), lambda i, j: (0, 0)),
            pl.BlockSpec((1, LANES), lambda i, j: (0, 0)),
        ],
        out_specs=[
            pl.BlockSpec((tm, tn), lambda i, j: (i, j)),
            pl.BlockSpec((tm, LANES), lambda i, j: (i, 0)),
        ],
        out_shape=[jax.ShapeDtypeStruct((m, P_WIDTH), f32), jax.ShapeDtypeStruct((m, LANES), f32)],
        scratch_shapes=[pltpu.VMEM((tm, D_MODEL), bf16)],
        compiler_params=_params(("parallel", "arbitrary")),
        name="inproj",
    )(x, g, w, wdt, dtb)


_HALF = LANES // HEAD_DIM
assert _HALF == 2


def _alibi_slopes():
    return jnp.exp2(-8.0 * jnp.arange(1, N_HEADS + 1, dtype=f32) / N_HEADS)


def _prompt_penalty():
    s = jnp.arange(WINDOW)[:, None]
    q = jnp.arange(WINDOW)[None, :]
    dist = jnp.where(s <= q, q - s, WINDOW + q - s).astype(f32)
    pen = (_alibi_slopes()[:, None, None] * dist[None]).reshape(N_KV_HEADS, Q_PER_KV, WINDOW, WINDOW)
    return pen.transpose(0, 2, 1, 3).reshape(N_KV_HEADS, WINDOW, Q_PER_KV * WINDOW)


def _sample_penalty():
    i = jnp.arange(DEC_SEQ)[:, None]
    c = jnp.arange(2 * WINDOW)[None, :]
    dist = WINDOW + i - c
    valid = (dist >= 0) & (dist < WINDOW) & (c < WINDOW + DEC_SEQ)
    pen = _alibi_slopes()[:, None, None] * dist.astype(f32)[None]
    return jnp.where(valid[None], pen, jnp.inf).reshape(N_HEADS * DEC_SEQ, 2 * WINDOW)


def _attn_prompt_kernel(sink_ref, pen_ref, q_ref, kc_ref, kp_ref, vc_ref, vp_ref, za_ref, o_ref):
    cols4 = Q_PER_KV * WINDOW
    key = lax.broadcasted_iota(jnp.int32, (WINDOW, cols4), 0)
    qry = _mod(lax.broadcasted_iota(jnp.int32, (WINDOW, cols4), 1), WINDOW)
    from_cur = key <= qry
    low_half = lax.broadcasted_iota(jnp.int32, (WINDOW, LANES), 1) < HEAD_DIM
    prev_off = jnp.where(pl.program_id(1) > 0, 0.0, -jnp.inf)
    nt = (((1,), (1,)), ((), ()))
    kv_tiles = {}
    for j in range(N_KV_HEADS):
        if j % _HALF == 0:
            tile = slice((j // _HALF) * LANES, (j // _HALF + 1) * LANES)
            kv_tiles = dict(k_cur=kc_ref[:, tile].astype(bf16), k_prev=kp_ref[:, tile].astype(bf16),
                            v_cur=vc_ref[:, tile].T.astype(bf16), v_prev=vp_ref[:, tile].T.astype(bf16))
        mine = low_half == (j % _HALF == 0)
        pieces = []
        for h in range(j * Q_PER_KV, (j + 1) * Q_PER_KV):
            piece = q_ref[:, (h // _HALF) * LANES:(h // _HALF + 1) * LANES] * HEAD_DIM ** -0.5
            if h % _HALF != j % _HALF:
                piece = pltpu.roll(piece, HEAD_DIM, 1)
            pieces.append(jnp.where(mine, piece, 0.0))
        q = jnp.concatenate(pieces, axis=0).astype(bf16)
        s_cur = lax.dot_general(kv_tiles["k_cur"], q, nt, preferred_element_type=f32)
        s_prev = lax.dot_general(kv_tiles["k_prev"], q, nt, preferred_element_type=f32)
        t = jnp.where(from_cur, s_cur, s_prev + prev_off) - pen_ref[j]
        sinks = jnp.concatenate([jnp.full((1, WINDOW), sink_ref[j * Q_PER_KV + g], f32)
                                 for g in range(Q_PER_KV)], axis=1)
        m = jnp.maximum(jnp.max(t, axis=0, keepdims=True), sinks)
        p = jnp.exp(t - m)
        inv = 1.0 / (jnp.sum(p, axis=0, keepdims=True) + jnp.exp(sinks - m))
        o = jnp.dot(kv_tiles["v_cur"], jnp.where(from_cur, p, 0.0).astype(bf16), preferred_element_type=f32)
        o += jnp.dot(kv_tiles["v_prev"], jnp.where(from_cur, 0.0, p).astype(bf16), preferred_element_type=f32)
        o = o * inv
        dims = slice((j % _HALF) * HEAD_DIM, (j % _HALF + 1) * HEAD_DIM)
        for pair in range(Q_PER_KV // _HALF):
            g0 = pair * _HALF
            two_heads = jnp.concatenate([o[dims, g * WINDOW:(g + 1) * WINDOW] for g in (g0, g0 + 1)], axis=0)
            cs = slice((j * Q_PER_KV // _HALF + pair) * LANES, (j * Q_PER_KV // _HALF + pair + 1) * LANES)
            o_ref[:, cs] = (two_heads.T * _silu(za_ref[:, cs])).astype(o_ref.dtype)


def _attn_prompt(p, sinks, batch):
    nb = SEQ // WINDOW
    kcol, vcol = _COL["k"] // KV_WIDTH, _COL["v"] // KV_WIDTH
    cur = lambda b, i: b * nb + i
    prev = lambda b, i: b * nb + jnp.maximum(i - 1, 0)
    return pl.pallas_call(
        _attn_prompt_kernel,
        grid=(batch, nb),
        in_specs=[
            pl.BlockSpec(memory_space=pltpu.SMEM),
            pl.BlockSpec((N_KV_HEADS, WINDOW, Q_PER_KV * WINDOW), lambda b, i: (0, 0, 0)),
            pl.BlockSpec((WINDOW, ATTN_WIDTH), lambda b, i: (cur(b, i), _COL["q"] // ATTN_WIDTH)),
            pl.BlockSpec((WINDOW, KV_WIDTH), lambda b, i: (cur(b, i), kcol)),
            pl.BlockSpec((WINDOW, KV_WIDTH), lambda b, i: (prev(b, i), kcol)),
            pl.BlockSpec((WINDOW, KV_WIDTH), lambda b, i: (cur(b, i), vcol)),
            pl.BlockSpec((WINDOW, KV_WIDTH), lambda b, i: (prev(b, i), vcol)),
            pl.BlockSpec((WINDOW, ATTN_WIDTH), lambda b, i: (cur(b, i), _COL["z_a"] // ATTN_WIDTH)),
        ],
        out_specs=pl.BlockSpec((WINDOW, ATTN_WIDTH), lambda b, i: (cur(b, i), 0)),
        out_shape=jax.ShapeDtypeStruct((batch * SEQ, ATTN_WIDTH), bf16),
        compiler_params=_params(("parallel", "parallel")),
        name="attn_prompt",
    )(sinks, _prompt_penalty(), p, p, p, p, p, p)


_ATTN_SEQS = 8


def _attn_sample_kernel(pen_ref, sink_ref, q_ref, kn_ref, vn_ref, za_ref, ck_ref, cv_ref,
                        o_ref, ko_ref, vo_ref, ak_ref, av_ref):
    keys = 2 * WINDOW
    pad = jnp.zeros((keys - WINDOW - DEC_SEQ, KV_WIDTH), f32)
    ak_ref[WINDOW + DEC_SEQ:, :] = pad
    av_ref[WINDOW + DEC_SEQ:, :] = pad
    low_half = lax.broadcasted_iota(jnp.int32, (DEC_SEQ, LANES), 1) < HEAD_DIM
    nt = (((1,), (1,)), ((), ()))

    def to_half(piece, src, dst):
        return piece if src == dst else pltpu.roll(piece, HEAD_DIM, 1)

    def one_sequence(n, carry):
        new = pl.ds(pl.multiple_of(n * DEC_SEQ, DEC_SEQ), DEC_SEQ)
        for cache_ref, new_ref, all_ref, out_ref in ((ck_ref, kn_ref, ak_ref, ko_ref),
                                                     (cv_ref, vn_ref, av_ref, vo_ref)):
            all_ref[:WINDOW, :] = cache_ref[n]
            all_ref[WINDOW:WINDOW + DEC_SEQ, :] = new_ref[new, :]
            out_ref[n] = all_ref[DEC_SEQ:WINDOW + DEC_SEQ, :]
        k16, v16 = ak_ref[...].astype(bf16), av_ref[...].astype(bf16)
        scores = []
        for j in range(N_KV_HEADS):
            pieces = []
            for h in range(j * Q_PER_KV, (j + 1) * Q_PER_KV):
                piece = q_ref[new, (h // _HALF) * LANES:(h // _HALF + 1) * LANES] * HEAD_DIM ** -0.5
                piece = to_half(piece, h % _HALF, j % _HALF)
                pieces.append(jnp.where(low_half == (j % _HALF == 0), piece, 0.0))
            qj = jnp.concatenate(pieces, axis=0).astype(bf16)
            kj = k16[:, (j // _HALF) * LANES:(j // _HALF + 1) * LANES]
            scores.append(lax.dot_general(qj, kj, nt, preferred_element_type=f32))
        t = jnp.concatenate(scores, axis=0) - pen_ref[...]
        sinks = sink_ref[...]
        m = jnp.maximum(jnp.max(t, axis=-1, keepdims=True), sinks)
        p = jnp.exp(t - m)
        inv = 1.0 / (jnp.sum(p, axis=-1, keepdims=True) + jnp.exp(sinks - m))
        p16 = p.astype(bf16)
        rows_per_kv = Q_PER_KV * DEC_SEQ
        outs = []
        for j in range(N_KV_HEADS):
            rows = slice(j * rows_per_kv, (j + 1) * rows_per_kv)
            vj = v16[:, (j // _HALF) * LANES:(j // _HALF + 1) * LANES]
            oj = jnp.dot(p16[rows], vj, preferred_element_type=f32) * inv[rows]
            for g in range(Q_PER_KV):
                h = j * Q_PER_KV + g
                outs.append(to_half(oj[g * DEC_SEQ:(g + 1) * DEC_SEQ], j % _HALF, h % _HALF))
        o = jnp.concatenate([jnp.where(low_half, outs[h], outs[h + 1]) for h in range(0, N_HEADS, _HALF)],
                            axis=1)
        o_ref[new, :] = o * _silu(za_ref[new, :])
        return carry

    lax.fori_loop(0, _ATTN_SEQS, one_sequence, 0)


def _attn_sample(p, sinks, cache_k, cache_v):
    nseq = cache_k.shape[0]
    rows = _ATTN_SEQS * DEC_SEQ
    cache_spec = pl.BlockSpec((_ATTN_SEQS, WINDOW, KV_WIDTH), lambda s: (s, 0, 0))
    sink_col = jnp.repeat(sinks.astype(f32), DEC_SEQ).reshape(N_HEADS * DEC_SEQ, 1)
    return pl.pallas_call(
        _attn_sample_kernel,
        grid=(nseq // _ATTN_SEQS,),
        in_specs=[
            pl.BlockSpec((N_HEADS * DEC_SEQ, 2 * WINDOW), lambda s: (0, 0)),
            pl.BlockSpec((N_HEADS * DEC_SEQ, 1), lambda s: (0, 0)),
            pl.BlockSpec((rows, ATTN_WIDTH), lambda s: (s, _COL["q"] // ATTN_WIDTH)),
            pl.BlockSpec((rows, KV_WIDTH), lambda s: (s, _COL["k"] // KV_WIDTH)),
            pl.BlockSpec((rows, KV_WIDTH), lambda s: (s, _COL["v"] // KV_WIDTH)),
            pl.BlockSpec((rows, ATTN_WIDTH), lambda s: (s, _COL["z_a"] // ATTN_WIDTH)),
            cache_spec, cache_spec,
        ],
        out_specs=[pl.BlockSpec((rows, ATTN_WIDTH), lambda s: (s, 0)), cache_spec, cache_spec],
        out_shape=[
            jax.ShapeDtypeStruct((nseq * DEC_SEQ, ATTN_WIDTH), f32),
            jax.ShapeDtypeStruct(cache_k.shape, f32),
            jax.ShapeDtypeStruct(cache_v.shape, f32),
        ],
        scratch_shapes=[pltpu.VMEM((2 * WINDOW, KV_WIDTH), f32), pltpu.VMEM((2 * WINDOW, KV_WIDTH), f32)],
        compiler_params=_params(("parallel",)),
        name="attn_sample",
    )(_sample_penalty(), sink_col, p, p, p, p, cache_k, cache_v)


def _expand_heads(v):
    low = lax.broadcasted_iota(jnp.int32, (v.shape[0], LANES), 1) < SSM_HEAD_DIM
    tiles = [jnp.where(low, v[:, r:r + 1], v[:, r + 1:r + 2]) for r in range(0, HEADS_PER_GROUP, 2)]
    return jnp.concatenate(tiles, axis=1)


def _split3(v):
    hi = v.astype(bf16)
    rest = v - hi.astype(f32)
    mid = rest.astype(bf16)
    return hi, mid, (rest - mid.astype(f32)).astype(bf16)


def _masked_sums(mask, v_c, v_r):
    m16 = jnp.where(mask, 1.0, 0.0).astype(bf16)
    out_c = sum(jnp.dot(m16, piece, preferred_element_type=f32) for piece in _split3(v_c))
    out_r = sum(lax.dot_general(piece, m16, (((1,), (1,)), ((), ())), preferred_element_type=f32)
                for piece in _split3(v_r))
    return out_c, out_r


_ROW_BIAS, _ROW_D, _ROW_GAIN, _ROW_ALOG, _ROW_ALOG_COL = CONV_W, CONV_W + 1, CONV_W + 2, CONV_W + 3, CONV_W + 4
_CONST_ROWS = _ROW_ALOG_COL + HEADS_PER_GROUP
_CONST_WIDTH = GROUP_WIDTH + 2 * D_STATE
_X_COLS, _B_COLS, _C_COLS = (slice(0, GROUP_WIDTH), slice(GROUP_WIDTH, GROUP_WIDTH + D_STATE),
                             slice(GROUP_WIDTH + D_STATE, _CONST_WIDTH))


def _ssd_constants(conv_w, conv_b, a_log, d_skip, ssm_norm):
    grouped = lambda t, width: t.reshape(t.shape[0], SSM_GROUPS, width).transpose(1, 0, 2)
    padded = lambda t: jnp.pad(t, ((0, 0), (0, 0), (0, _CONST_WIDTH - t.shape[2])))
    taps = jnp.concatenate([conv_w, conv_b[None]], axis=0)
    top = jnp.concatenate([grouped(taps[:, :SSM_WIDTH], GROUP_WIDTH),
                           grouped(taps[:, SSM_WIDTH:SSM_WIDTH + BC_WIDTH], D_STATE),
                           grouped(taps[:, SSM_WIDTH + BC_WIDTH:], D_STATE)], axis=2)
    al = a_log.reshape(SSM_GROUPS, HEADS_PER_GROUP)
    return jnp.concatenate([top, padded(grouped(jnp.repeat(d_skip, SSM_HEAD_DIM)[None], GROUP_WIDTH)),
                            padded(grouped(ssm_norm[None], GROUP_WIDTH)),
                            padded(al[:, None, :]), padded(al[:, :, None])], axis=1)


def _intra_chunk(cb, causal, acum_c, acum_r, dt_r, xc):
    lane = lax.broadcasted_iota(jnp.int32, (CHUNK, LANES), 1)
    pieces = []
    for pair in range(HEADS_PER_GROUP // 2):
        x2 = xc[:, pair * LANES:(pair + 1) * LANES]
        acc = jnp.zeros((CHUNK, LANES), f32)
        for half in range(2):
            r = 2 * pair + half
            keep = (lane < SSM_HEAD_DIM) if half == 0 else (lane >= SSM_HEAD_DIM)
            decay = jnp.exp(jnp.where(causal, acum_c[:, r:r + 1] - acum_r[r:r + 1, :], -jnp.inf))
            acc += jnp.dot((cb * decay * dt_r[r:r + 1, :]).astype(bf16), jnp.where(keep, x2, 0.0).astype(bf16),
                           preferred_element_type=f32)
        pieces.append(acc)
    return jnp.concatenate(pieces, axis=1)


def _gated_norm(y, z, gain):
    u = y * _silu(z)
    ms = jnp.mean(u * u, axis=-1, keepdims=True)
    return u * lax.rsqrt(ms + NORM_EPS) * gain


def _conv_silu(cur, shifted, cst_ref, cols):
    y = cst_ref[_ROW_BIAS:_ROW_BIAS + 1, cols] + cst_ref[CONV_W - 1:CONV_W, cols] * cur
    for k in range(1, CONV_W):
        y = y + cst_ref[CONV_W - 1 - k:CONV_W - k, cols] * shifted[k - 1]
    return _silu(y)


def _decay_sums(mask, dt_c, dt_r, cst_ref):
    a_c = -jnp.exp(cst_ref[_ROW_ALOG:_ROW_ALOG + 1, 0:HEADS_PER_GROUP])
    a_r = -jnp.exp(cst_ref[_ROW_ALOG_COL:_ROW_ALOG_COL + HEADS_PER_GROUP, 0:1])
    return _masked_sums(mask, dt_c * a_c, dt_r * a_r)


def _ssd_prompt_kernel(x_ref, b_ref, c_ref, z_ref, dtc_ref, dtr_ref, cst_ref,
                       y_ref, st_ref, xpad, bpad, cpad, state):
    c = pl.program_id(2)
    tail = 8

    @pl.when(c == 0)
    def _():
        state[...] = jnp.zeros_like(state)
        for pad in (xpad, bpad, cpad):
            pad[:tail, :] = jnp.zeros((tail, pad.shape[1]), f32)

    @pl.when(c > 0)
    def _():
        for pad in (xpad, bpad, cpad):
            pad[:tail, :] = pad[CHUNK:CHUNK + tail, :]

    conv = []
    for raw_ref, pad, cols in ((x_ref, xpad, _X_COLS), (b_ref, bpad, _B_COLS), (c_ref, cpad, _C_COLS)):
        pad[tail:, :] = raw_ref[...]
        shifted = [pad[tail - k:tail - k + CHUNK, :] for k in range(1, CONV_W)]
        conv.append(_conv_silu(raw_ref[...], shifted, cst_ref, cols))
    xc, bm, cm = conv

    dt_c, dt_r = dtc_ref[...], dtr_ref[...]
    row = lax.broadcasted_iota(jnp.int32, (CHUNK, CHUNK), 0)
    col = lax.broadcasted_iota(jnp.int32, (CHUNK, CHUNK), 1)
    causal = row >= col
    acum_c, acum_r = _decay_sums(causal, dt_c, dt_r, cst_ref)
    cb = lax.dot_general(cm.astype(bf16), bm.astype(bf16), (((1,), (1,)), ((), ())),
                         preferred_element_type=f32)
    y = _intra_chunk(cb, causal, acum_c, acum_r, dt_r, xc)

    s_in = state[...]
    decay_in = _expand_heads(jnp.exp(acum_c))
    y += decay_in * jnp.dot(cm.astype(bf16), s_in.astype(bf16), preferred_element_type=f32)
    weight_out = _expand_heads(jnp.exp(acum_c[CHUNK - 1:CHUNK, :] - acum_c) * dt_c)
    s_out = decay_in[CHUNK - 1:CHUNK, :] * s_in + jnp.dot(
        bm.T.astype(bf16), (xc * weight_out).astype(bf16), preferred_element_type=f32)
    state[...] = s_out

    y += cst_ref[_ROW_D:_ROW_D + 1, _X_COLS] * xc
    y_ref[...] = _gated_norm(y, z_ref[...], cst_ref[_ROW_GAIN:_ROW_GAIN + 1, _X_COLS]).astype(y_ref.dtype)

    @pl.when(c == pl.num_programs(2) - 1)
    def _():
        st_ref[...] = s_out.T


def _ssd_prompt(p, dt_c, dt_r, consts, batch):
    nc = SEQ // CHUNK
    rb = lambda b, g, c: b * nc + c
    xcol, zcol = _COL["x"] // GROUP_WIDTH, _COL["z_s"] // GROUP_WIDTH
    bcol, ccol = _COL["B"] // D_STATE, _COL["C"] // D_STATE
    return pl.pallas_call(
        _ssd_prompt_kernel,
        grid=(batch, SSM_GROUPS, nc),
        in_specs=[
            pl.BlockSpec((CHUNK, GROUP_WIDTH), lambda b, g, c: (rb(b, g, c), xcol + g)),
            pl.BlockSpec((CHUNK, D_STATE), lambda b, g, c: (rb(b, g, c), bcol + g)),
            pl.BlockSpec((CHUNK, D_STATE), lambda b, g, c: (rb(b, g, c), ccol + g)),
            pl.BlockSpec((CHUNK, GROUP_WIDTH), lambda b, g, c: (rb(b, g, c), zcol + g)),
            pl.BlockSpec((None, CHUNK, HEADS_PER_GROUP), lambda b, g, c: (g, rb(b, g, c), 0)),
            pl.BlockSpec((None, HEADS_PER_GROUP, CHUNK), lambda b, g, c: (g, 0, rb(b, g, c))),
            pl.BlockSpec((None, _CONST_ROWS, _CONST_WIDTH), lambda b, g, c: (g, 0, 0)),
        ],
        out_specs=[
            pl.BlockSpec((CHUNK, GROUP_WIDTH), lambda b, g, c: (rb(b, g, c), g)),
            pl.BlockSpec((None, GROUP_WIDTH, D_STATE), lambda b, g, c: (b, g, 0)),
        ],
        out_shape=[
            jax.ShapeDtypeStruct((batch * SEQ, SSM_WIDTH), bf16),
            jax.ShapeDtypeStruct((batch, SSM_WIDTH, D_STATE), f32),
        ],
        scratch_shapes=[
            pltpu.VMEM((CHUNK + 8, GROUP_WIDTH), f32),
            pltpu.VMEM((CHUNK + 8, D_STATE), f32),
            pltpu.VMEM((CHUNK + 8, D_STATE), f32),
            pltpu.VMEM((D_STATE, GROUP_WIDTH), f32),
        ],
        compiler_params=_params(("parallel", "parallel", "arbitrary")),
        name="ssd_prompt",
    )(p, p, p, p, dt_c, dt_r, consts)


def _ssd_sample_kernel(x_ref, b_ref, c_ref, z_ref, dtc_ref, dtr_ref, cst_ref,
                       cx_ref, cbuf_ref, cc_ref, s0_ref, y_ref, s1_ref, xprev, bprev, cprev):
    row1 = lax.broadcasted_iota(jnp.int32, (CHUNK, 1), 0)
    conv = []
    for raw_ref, buf_ref, prev, cols in ((x_ref, cx_ref, xprev, _X_COLS), (b_ref, cbuf_ref, bprev, _B_COLS),
                                         (c_ref, cc_ref, cprev, _C_COLS)):
        prev[...] = jnp.zeros_like(prev)
        prev[:, :CONV_W - 1, :] = buf_ref[...]
        cached = prev[...].reshape(CHUNK, prev.shape[2])
        cur = raw_ref[...]
        shifted = []
        for k in range(1, CONV_W):
            from_cache = pltpu.roll(cached, (k - (CONV_W - 1)) % CHUNK, 0)
            shifted.append(jnp.where(_mod(row1, DEC_SEQ) >= k, pltpu.roll(cur, k, 0), from_cache))
        conv.append(_conv_silu(cur, shifted, cst_ref, cols))
    xc, bm, cm = conv

    dt_c, dt_r = dtc_ref[...], dtr_ref[...]
    row = lax.broadcasted_iota(jnp.int32, (CHUNK, CHUNK), 0)
    col = lax.broadcasted_iota(jnp.int32, (CHUNK, CHUNK), 1)
    same_seq = _div(row, DEC_SEQ) == _div(col, DEC_SEQ)
    causal = same_seq & (row >= col)
    acum_c, acum_r = _decay_sums(causal, dt_c, dt_r, cst_ref)
    cb = lax.dot_general(cm.astype(bf16), bm.astype(bf16), (((1,), (1,)), ((), ())),
                         preferred_element_type=f32)
    y = _intra_chunk(cb, causal, acum_c, acum_r, dt_r, xc)

    to_end, _ = _decay_sums(same_seq & (row < col), dt_c, dt_r, cst_ref)
    decay_in = _expand_heads(jnp.exp(acum_c))
    xw = (xc * _expand_heads(jnp.exp(to_end) * dt_c)).T.astype(bf16)
    keep = jnp.exp(acum_r)
    seq_of_row = _div(row1, DEC_SEQ)
    inter = jnp.zeros((CHUNK, GROUP_WIDTH), f32)
    for n in range(SEQS_PER_STEP):
        mine = seq_of_row == n
        s0 = s0_ref[n]
        inter += lax.dot_general(jnp.where(mine, cm, 0.0).astype(bf16), s0.astype(bf16),
                                 (((1,), (1,)), ((), ())), preferred_element_type=f32)
        update = jnp.dot(xw, jnp.where(mine, bm, 0.0).astype(bf16), preferred_element_type=f32)
        last_lane = (n + 1) * DEC_SEQ - 1
        for r in range(HEADS_PER_GROUP):
            rows = slice(r * SSM_HEAD_DIM, (r + 1) * SSM_HEAD_DIM)
            s1_ref[n, rows, :] = keep[r:r + 1, last_lane:last_lane + 1] * s0[rows] + update[rows]
    y += decay_in * inter + cst_ref[_ROW_D:_ROW_D + 1, _X_COLS] * xc
    y_ref[...] = _gated_norm(y, z_ref[...], cst_ref[_ROW_GAIN:_ROW_GAIN + 1, _X_COLS]).astype(y_ref.dtype)


def _ssd_sample(p, dt_c, dt_r, consts, conv_state, s0):
    nseq = s0.shape[0]
    nsb = nseq // SEQS_PER_STEP
    xcol, zcol = _COL["x"] // GROUP_WIDTH, _COL["z_s"] // GROUP_WIDTH
    bcol, ccol = _COL["B"] // D_STATE, _COL["C"] // D_STATE
    conv_block = lambda width, first: pl.BlockSpec(
        (SEQS_PER_STEP, CONV_W - 1, width), lambda s, g: (s, 0, first + g))
    state_spec = pl.BlockSpec((SEQS_PER_STEP, GROUP_WIDTH, D_STATE), lambda s, g: (s, g, 0))
    return pl.pallas_call(
        _ssd_sample_kernel,
        grid=(nsb, SSM_GROUPS),
        in_specs=[
            pl.BlockSpec((CHUNK, GROUP_WIDTH), lambda s, g: (s, xcol + g)),
            pl.BlockSpec((CHUNK, D_STATE), lambda s, g: (s, bcol + g)),
            pl.BlockSpec((CHUNK, D_STATE), lambda s, g: (s, ccol + g)),
            pl.BlockSpec((CHUNK, GROUP_WIDTH), lambda s, g: (s, zcol + g)),
            pl.BlockSpec((None, CHUNK, HEADS_PER_GROUP), lambda s, g: (g, s, 0)),
            pl.BlockSpec((None, HEADS_PER_GROUP, CHUNK), lambda s, g: (g, 0, s)),
            pl.BlockSpec((None, _CONST_ROWS, _CONST_WIDTH), lambda s, g: (g, 0, 0)),
            conv_block(GROUP_WIDTH, 0),
            conv_block(D_STATE, SSM_WIDTH // D_STATE),
            conv_block(D_STATE, (SSM_WIDTH + BC_WIDTH) // D_STATE),
            state_spec,
        ],
        out_specs=[pl.BlockSpec((CHUNK, GROUP_WIDTH), lambda s, g: (s, g)), state_spec],
        out_shape=[
            jax.ShapeDtypeStruct((nseq * DEC_SEQ, SSM_WIDTH), bf16),
            jax.ShapeDtypeStruct(s0.shape, f32),
        ],
        scratch_shapes=[
            pltpu.VMEM((SEQS_PER_STEP, 8, GROUP_WIDTH), f32),
            pltpu.VMEM((SEQS_PER_STEP, 8, D_STATE), f32),
            pltpu.VMEM((SEQS_PER_STEP, 8, D_STATE), f32),
        ],
        compiler_params=_params(("parallel", "parallel")),
        name="ssd_sample",
    )(p, p, p, p, dt_c, dt_r, consts, conv_state, conv_state, conv_state, s0)


def _merge_kernel(a_ref, s_ref, ga_ref, gs_ref, wa_ref, ws_ref, o_ref):
    ya = jnp.dot(a_ref[...].astype(bf16), wa_ref[...], preferred_element_type=f32)
    ys = jnp.dot(s_ref[...], ws_ref[...], preferred_element_type=f32)
    o_ref[...] = (jax.nn.sigmoid(ga_ref[...]) * ya + jax.nn.sigmoid(gs_ref[...]) * ys).astype(o_ref.dtype)


def _merge(attn, ssm, p, wa, ws, tm=1024, tn=512):
    m = attn.shape[0]
    return pl.pallas_call(
        _merge_kernel,
        grid=(m // tm, D_MODEL // tn),
        in_specs=[
            pl.BlockSpec((tm, ATTN_WIDTH), lambda i, j: (i, 0)),
            pl.BlockSpec((tm, SSM_WIDTH), lambda i, j: (i, 0)),
            pl.BlockSpec((tm, tn), lambda i, j: (i, _COL["g_a"] // tn + j)),
            pl.BlockSpec((tm, tn), lambda i, j: (i, _COL["g_s"] // tn + j)),
            pl.BlockSpec((ATTN_WIDTH, tn), lambda i, j: (0, j)),
            pl.BlockSpec((SSM_WIDTH, tn), lambda i, j: (0, j)),
        ],
        out_specs=pl.BlockSpec((tm, tn), lambda i, j: (i, j)),
        out_shape=jax.ShapeDtypeStruct((m, D_MODEL), bf16),
        compiler_params=_params(("parallel", "arbitrary")),
        name="merge",
    )(attn, ssm, p, p, wa, ws)


def _out_kernel(m_ref, wo_ref, x_ref, g_ref, o_ref):
    y = jnp.dot(m_ref[...], wo_ref[...], preferred_element_type=f32)
    ms = jnp.mean(y * y, axis=-1, keepdims=True)
    o_ref[...] = x_ref[...] + y * lax.rsqrt(ms + NORM_EPS) * g_ref[...]


def _out(merged, wo, x, g, tm=512):
    m = x.shape[0]
    return pl.pallas_call(
        _out_kernel,
        grid=(m // tm,),
        in_specs=[
            pl.BlockSpec((tm, D_MODEL), lambda i: (i, 0)),
            pl.BlockSpec((D_MODEL, D_MODEL), lambda i: (0, 0)),
            pl.BlockSpec((tm, D_MODEL), lambda i: (i, 0)),
            pl.BlockSpec((1, D_MODEL), lambda i: (0, 0)),
        ],
        out_specs=pl.BlockSpec((tm, D_MODEL), lambda i: (i, 0)),
        out_shape=jax.ShapeDtypeStruct((m, D_MODEL), f32),
        compiler_params=_params(("parallel",)),
        name="outproj",
    )(merged, wo, x, g)


def _group_layouts(dt, rows):
    d = dt[:, :SSM_HEADS].reshape(rows, SSM_GROUPS, HEADS_PER_GROUP)
    return d.transpose(1, 0, 2), d.transpose(1, 2, 0)


def _layer(xp, xs, cache_k, cache_v, state_ssm, state_conv, norm_pre, w_in, conv_w, conv_b, dt_bias,
           a_log, d_skip, ssm_norm, attn_sinks, w_attn_br, w_ssm_br, w_out, norm_post):
    batch, nseq = xp.shape[0], xs.shape[0]
    mp, ms = batch * SEQ, nseq * DEC_SEQ
    src = lambda name, width: w_in[:, _SRC[name]:_SRC[name] + width]
    w_main = jnp.concatenate([
        src("q", ATTN_WIDTH), src("z_a", ATTN_WIDTH), src("g_a", D_MODEL), src("g_s", D_MODEL),
        src("z_s", SSM_WIDTH), src("xbc", CONV_DIM), src("k", KV_WIDTH), src("v", KV_WIDTH)],
        axis=1).astype(bf16)
    w_dt = jnp.pad(src("dt", SSM_HEADS), ((0, 0), (0, LANES - SSM_HEADS))).astype(bf16)
    dtb = jnp.pad(dt_bias, (0, LANES - SSM_HEADS)).reshape(1, LANES)
    g_pre = norm_pre.reshape(1, D_MODEL)

    ssd_consts = _ssd_constants(conv_w, conv_b, a_log, d_skip, ssm_norm)
    wa, ws, wo = w_attn_br.astype(bf16), w_ssm_br.astype(bf16), w_out.astype(bf16)
    g_post = norm_post.reshape(1, D_MODEL)

    xp2, xs2 = xp.reshape(mp, D_MODEL), xs.reshape(ms, D_MODEL)
    pp, dtp = _inproj(xp2, g_pre, w_main, w_dt, dtb)
    ps, dts = _inproj(xs2, g_pre, w_main, w_dt, dtb)

    attn_p = _attn_prompt(pp, attn_sinks, batch)
    ssm_p, st_p = _ssd_prompt(pp, *_group_layouts(dtp, mp), ssd_consts, batch)
    yp = _out(_merge(attn_p, ssm_p, pp, wa, ws), wo, xp2, g_post)
    pp3 = pp.reshape(batch, SEQ, P_WIDTH)
    k_p = pp3[:, SEQ - WINDOW:, _COL["k"]:_COL["k"] + KV_WIDTH]
    v_p = pp3[:, SEQ - WINDOW:, _COL["v"]:_COL["v"] + KV_WIDTH]
    conv_p = pp3[:, SEQ - (CONV_W - 1):, _COL["x"]:_COL["x"] + CONV_DIM]

    attn_s, k_s, v_s = _attn_sample(ps, attn_sinks, cache_k.reshape(nseq, WINDOW, KV_WIDTH),
                                    cache_v.reshape(nseq, WINDOW, KV_WIDTH))
    ssm_s, st_s = _ssd_sample(ps, *_group_layouts(dts, ms), ssd_consts, state_conv,
                              state_ssm.reshape(nseq, SSM_WIDTH, D_STATE))
    ys = _out(_merge(attn_s, ssm_s, ps, wa, ws), wo, xs2, g_post)
    conv_s = ps.reshape(nseq, DEC_SEQ, P_WIDTH)[:, DEC_SEQ - (CONV_W - 1):, _COL["x"]:_COL["x"] + CONV_DIM]

    kv = lambda t, n: t.reshape(1, n, WINDOW, N_KV_HEADS, HEAD_DIM)
    st = lambda t, n: t.reshape(1, n, SSM_HEADS, SSM_HEAD_DIM, D_STATE)
    return (yp.reshape(xp.shape), ys.reshape(xs.shape), kv(k_p, batch), kv(v_p, batch), st(st_p, batch),
            conv_p[None], kv(k_s, nseq), kv(v_s, nseq), st(st_s, nseq), conv_s[None])


def kernel(x_prompt, x_sample, cache_k, cache_v, state_ssm, state_conv, norm_pre, w_in, conv_w, conv_b,
           dt_bias, a_log, d_skip, ssm_norm, attn_sinks, w_attn_br, w_ssm_br, w_out, norm_post):
    assert w_in.shape[0] == 1, "single-layer trunk"
    return _layer(x_prompt, x_sample, cache_k[0], cache_v[0], state_ssm[0], state_conv[0], norm_pre[0],
                  w_in[0], conv_w[0], conv_b[0], dt_bias[0], a_log[0], d_skip[0], ssm_norm[0],
                  attn_sinks[0], w_attn_br[0], w_ssm_br[0], w_out[0], norm_post[0])
```

```python
import functools

import jax
import jax.numpy as jnp
from jax import lax
from jax.experimental import pallas as pl
from jax.experimental.pallas import tpu as pltpu

f32 = jnp.float32
bf16 = jnp.bfloat16

D_MODEL = 2048
SEQ = 4096
DEC_SEQ = 8
N_HEADS = 32
N_KV_HEADS = 8
HEAD_DIM = 64
Q_PER_KV = N_HEADS // N_KV_HEADS
ATTN_WIDTH = N_HEADS * HEAD_DIM
KV_WIDTH = N_KV_HEADS * HEAD_DIM
WINDOW = 128
SSM_WIDTH = 2 * D_MODEL
SSM_HEAD_DIM = 64
SSM_HEADS = SSM_WIDTH // SSM_HEAD_DIM
SSM_GROUPS = 8
HEADS_PER_GROUP = SSM_HEADS // SSM_GROUPS
GROUP_WIDTH = HEADS_PER_GROUP * SSM_HEAD_DIM
D_STATE = 128
CONV_W = 4
BC_WIDTH = SSM_GROUPS * D_STATE
CONV_DIM = SSM_WIDTH + 2 * BC_WIDTH
CHUNK = 128
NORM_EPS = 1e-6

_SRC = dict(q=0, k=2048, v=2560, z_a=3072, xbc=5120, z_s=11264, dt=15360, g_a=15424, g_s=17472)
_COL = dict(q=0, k=2048, v=2560, z_a=3072, x=5120, B=9216, C=10240, z_s=11264, g_a=15360, g_s=17408)
P_WIDTH = 19456
_GATE_COL = _COL["g_a"]
HALF_ATTN = ATTN_WIDTH // 2
LANES = 128
SEQS_PER_STEP = 16

_VMEM_LIMIT = 56 * 1024 * 1024


def _params(sem):
    return pltpu.CompilerParams(dimension_semantics=sem, vmem_limit_bytes=_VMEM_LIMIT)


def _silu(v):
    return v * jax.nn.sigmoid(v)


def _div(v, n):
    assert n & (n - 1) == 0
    return v >> (n.bit_length() - 1)


def _mod(v, n):
    assert n & (n - 1) == 0
    return v & (n - 1)


_NORM_ROWS = 128


def _inproj_kernel(x_ref, g_ref, w_ref, wg_ref, wdt_ref, dtb_ref, p_ref, dt_ref, h_ref, *, main_blocks):
    j = pl.program_id(1)

    @pl.when(j == 0)
    def _():
        def norm_rows(i, carry):
            rows = pl.ds(pl.multiple_of(i * _NORM_ROWS, _NORM_ROWS), _NORM_ROWS)
            x = x_ref[rows, :]
            ms = jnp.mean(x * x, axis=-1, keepdims=True)
            h = (x * lax.rsqrt(ms + NORM_EPS) * g_ref[...]).astype(bf16)
            h_ref[rows, :] = h
            v = jnp.dot(h, wdt_ref[...], preferred_element_type=f32) + dtb_ref[...]
            dt_ref[rows, :] = jnp.maximum(v, 0.0) + jnp.log1p(jnp.exp(-jnp.abs(v)))
            return carry

        lax.fori_loop(0, x_ref.shape[0] // _NORM_ROWS, norm_rows, 0)

    @pl.when(j < main_blocks)
    def _():
        p_ref[...] = jnp.dot(h_ref[...], w_ref[...], preferred_element_type=f32)

    @pl.when(j >= main_blocks)
    def _():
        p_ref[...] = jnp.dot(h_ref[...], wg_ref[...], preferred_element_type=f32)


def _inproj(x, g, w, wg, wdt, dtb, tm=1024, tn=1024):
    m = x.shape[0]
    main_blocks = w.shape[1] // tn
    return pl.pallas_call(
        functools.partial(_inproj_kernel, main_blocks=main_blocks),
        grid=(m // tm, P_WIDTH // tn),
        in_specs=[
            pl.BlockSpec((tm, D_MODEL), lambda i, j: (i, 0)),
            pl.BlockSpec((1, D_MODEL), lambda i, j: (0, 0)),
            pl.BlockSpec((D_MODEL, tn), lambda i, j: (0, jnp.minimum(j, main_blocks - 1))),
            pl.BlockSpec((D_MODEL, tn), lambda i, j: (0, jnp.maximum(j - main_blocks, 0))),
            pl.BlockSpec((D_MODEL, LANES), lambda i, j: (0, 0)),
            pl.BlockSpec((1, LANES), lambda i, j: (0, 0)),
        ],
        out_specs=[
            pl.BlockSpec((tm, tn), lambda i, j: (i, j)),
            pl.BlockSpec((tm, LANES), lambda i, j: (i, 0)),
        ],
        out_shape=[jax.ShapeDtypeStruct((m, P_WIDTH), f32), jax.ShapeDtypeStruct((m, LANES), f32)],
        scratch_shapes=[pltpu.VMEM((tm, D_MODEL), bf16)],
        compiler_params=_params(("parallel", "arbitrary")),
        name="inproj",
    )(x, g, w, wg, wdt, dtb)


_HALF = LANES // HEAD_DIM
assert _HALF == 2


def _alibi_slopes():
    return jnp.exp2(-8.0 * jnp.arange(1, N_HEADS + 1, dtype=f32) / N_HEADS)


def _prompt_penalty():
    s = jnp.arange(WINDOW)[:, None]
    q = jnp.arange(WINDOW)[None, :]
    dist = jnp.where(s <= q, q - s, WINDOW + q - s).astype(f32)
    pen = (_alibi_slopes()[:, None, None] * dist[None]).reshape(N_KV_HEADS, Q_PER_KV, WINDOW, WINDOW)
    return pen.transpose(0, 2, 1, 3).reshape(N_KV_HEADS, WINDOW, Q_PER_KV * WINDOW)


def _sample_penalty():
    i = jnp.arange(DEC_SEQ)[:, None]
    c = jnp.arange(2 * WINDOW)[None, :]
    dist = WINDOW + i - c
    valid = (dist >= 0) & (dist < WINDOW) & (c < WINDOW + DEC_SEQ)
    pen = _alibi_slopes()[:, None, None] * dist.astype(f32)[None]
    return jnp.where(valid[None], pen, jnp.inf).reshape(N_HEADS * DEC_SEQ, 2 * WINDOW)


def _attn_prompt_kernel(sink_ref, pen_ref, q_ref, kc_ref, kp_ref, vc_ref, vp_ref, za0_ref, za1_ref, a_ref):
    cols4 = Q_PER_KV * WINDOW
    key = lax.broadcasted_iota(jnp.int32, (WINDOW, cols4), 0)
    qry = _mod(lax.broadcasted_iota(jnp.int32, (WINDOW, cols4), 1), WINDOW)
    from_cur = key <= qry
    low_half = lax.broadcasted_iota(jnp.int32, (WINDOW, LANES), 1) < HEAD_DIM
    prev_off = jnp.where(pl.program_id(1) > 0, 0.0, -jnp.inf)
    nt = (((1,), (1,)), ((), ()))
    kv_tiles = {}
    for j in range(N_KV_HEADS):
        if j % _HALF == 0:
            tile = slice((j // _HALF) * LANES, (j // _HALF + 1) * LANES)
            kv_tiles = dict(k_cur=kc_ref[:, tile].astype(bf16), k_prev=kp_ref[:, tile].astype(bf16),
                            v_cur=vc_ref[:, tile].T.astype(bf16), v_prev=vp_ref[:, tile].T.astype(bf16))
        mine = low_half == (j % _HALF == 0)
        pieces = []
        for h in range(j * Q_PER_KV, (j + 1) * Q_PER_KV):
            piece = q_ref[:, (h // _HALF) * LANES:(h // _HALF + 1) * LANES] * HEAD_DIM ** -0.5
            if h % _HALF != j % _HALF:
                piece = pltpu.roll(piece, HEAD_DIM, 1)
            pieces.append(jnp.where(mine, piece, 0.0))
        q = jnp.concatenate(pieces, axis=0).astype(bf16)
        s_cur = lax.dot_general(kv_tiles["k_cur"], q, nt, preferred_element_type=f32)
        s_prev = lax.dot_general(kv_tiles["k_prev"], q, nt, preferred_element_type=f32)
        t = jnp.where(from_cur, s_cur, s_prev + prev_off) - pen_ref[j]
        sinks = jnp.concatenate([jnp.full((1, WINDOW), sink_ref[j * Q_PER_KV + g], f32)
                                 for g in range(Q_PER_KV)], axis=1)
        m = jnp.maximum(jnp.max(t, axis=0, keepdims=True), sinks)
        p = jnp.exp(t - m)
        inv = 1.0 / (jnp.sum(p, axis=0, keepdims=True) + jnp.exp(sinks - m))
        o = jnp.dot(kv_tiles["v_cur"], jnp.where(from_cur, p, 0.0).astype(bf16), preferred_element_type=f32)
        o += jnp.dot(kv_tiles["v_prev"], jnp.where(from_cur, 0.0, p).astype(bf16), preferred_element_type=f32)
        o = o * inv
        dims = slice((j % _HALF) * HEAD_DIM, (j % _HALF + 1) * HEAD_DIM)
        for pair in range(Q_PER_KV // _HALF):
            g0 = pair * _HALF
            two_heads = jnp.concatenate([o[dims, g * WINDOW:(g + 1) * WINDOW] for g in (g0, g0 + 1)], axis=0)
            first = (j * Q_PER_KV // _HALF + pair) * LANES
            za_ref = (za0_ref, za1_ref)[first // HALF_ATTN]
            za = za_ref[:, first % HALF_ATTN:first % HALF_ATTN + LANES]
            a_ref[:, first:first + LANES] = (two_heads.T * _silu(za)).astype(a_ref.dtype)


def _attn_prompt(p, sinks, batch):
    nb = SEQ // WINDOW
    kcol, vcol = _COL["k"] // KV_WIDTH, _COL["v"] // KV_WIDTH
    zcol = _COL["z_a"] // HALF_ATTN
    cur = lambda b, i: b * nb + i
    prev = lambda b, i: b * nb + jnp.maximum(i - 1, 0)
    half_block = lambda col: pl.BlockSpec((WINDOW, HALF_ATTN), lambda b, i: (cur(b, i), col))
    return pl.pallas_call(
        _attn_prompt_kernel,
        grid=(batch, nb),
        in_specs=[
            pl.BlockSpec(memory_space=pltpu.SMEM),
            pl.BlockSpec((N_KV_HEADS, WINDOW, Q_PER_KV * WINDOW), lambda b, i: (0, 0, 0)),
            pl.BlockSpec((WINDOW, ATTN_WIDTH), lambda b, i: (cur(b, i), _COL["q"] // ATTN_WIDTH)),
            pl.BlockSpec((WINDOW, KV_WIDTH), lambda b, i: (cur(b, i), kcol)),
            pl.BlockSpec((WINDOW, KV_WIDTH), lambda b, i: (prev(b, i), kcol)),
            pl.BlockSpec((WINDOW, KV_WIDTH), lambda b, i: (cur(b, i), vcol)),
            pl.BlockSpec((WINDOW, KV_WIDTH), lambda b, i: (prev(b, i), vcol)),
            half_block(zcol), half_block(zcol + 1),
        ],
        out_specs=pl.BlockSpec((WINDOW, ATTN_WIDTH), lambda b, i: (cur(b, i), 0)),
        out_shape=jax.ShapeDtypeStruct((batch * SEQ, ATTN_WIDTH), bf16),
        compiler_params=_params(("parallel", "parallel")),
        name="attn_prompt",
    )(sinks, _prompt_penalty(), p, p, p, p, p, p, p)


_ATTN_SEQS = 8


def _attn_sample_kernel(pen_ref, sink_ref, q_ref, kn_ref, vn_ref, za0_ref, za1_ref, ck_ref, cv_ref,
                        a_ref, ko_ref, vo_ref, ak_ref, av_ref):
    keys = 2 * WINDOW
    pad = jnp.zeros((keys - WINDOW - DEC_SEQ, KV_WIDTH), f32)
    ak_ref[WINDOW + DEC_SEQ:, :] = pad
    av_ref[WINDOW + DEC_SEQ:, :] = pad
    low_half = lax.broadcasted_iota(jnp.int32, (DEC_SEQ, LANES), 1) < HEAD_DIM
    nt = (((1,), (1,)), ((), ()))

    def to_half(piece, src, dst):
        return piece if src == dst else pltpu.roll(piece, HEAD_DIM, 1)

    def one_sequence(n, carry):
        new = pl.ds(pl.multiple_of(n * DEC_SEQ, DEC_SEQ), DEC_SEQ)
        for cache_ref, new_ref, all_ref, out_ref in ((ck_ref, kn_ref, ak_ref, ko_ref),
                                                     (cv_ref, vn_ref, av_ref, vo_ref)):
            all_ref[:WINDOW, :] = cache_ref[n]
            all_ref[WINDOW:WINDOW + DEC_SEQ, :] = new_ref[new, :]
            out_ref[n] = all_ref[DEC_SEQ:WINDOW + DEC_SEQ, :]
        k16, v16 = ak_ref[...].astype(bf16), av_ref[...].astype(bf16)
        scores = []
        for j in range(N_KV_HEADS):
            pieces = []
            for h in range(j * Q_PER_KV, (j + 1) * Q_PER_KV):
                piece = q_ref[new, (h // _HALF) * LANES:(h // _HALF + 1) * LANES] * HEAD_DIM ** -0.5
                piece = to_half(piece, h % _HALF, j % _HALF)
                pieces.append(jnp.where(low_half == (j % _HALF == 0), piece, 0.0))
            qj = jnp.concatenate(pieces, axis=0).astype(bf16)
            kj = k16[:, (j // _HALF) * LANES:(j // _HALF + 1) * LANES]
            scores.append(lax.dot_general(qj, kj, nt, preferred_element_type=f32))
        t = jnp.concatenate(scores, axis=0) - pen_ref[...]
        sinks = sink_ref[...]
        m = jnp.maximum(jnp.max(t, axis=-1, keepdims=True), sinks)
        p = jnp.exp(t - m)
        inv = 1.0 / (jnp.sum(p, axis=-1, keepdims=True) + jnp.exp(sinks - m))
        p16 = p.astype(bf16)
        rows_per_kv = Q_PER_KV * DEC_SEQ
        outs = []
        for j in range(N_KV_HEADS):
            rows = slice(j * rows_per_kv, (j + 1) * rows_per_kv)
            vj = v16[:, (j // _HALF) * LANES:(j // _HALF + 1) * LANES]
            oj = jnp.dot(p16[rows], vj, preferred_element_type=f32) * inv[rows]
            for g in range(Q_PER_KV):
                h = j * Q_PER_KV + g
                outs.append(to_half(oj[g * DEC_SEQ:(g + 1) * DEC_SEQ], j % _HALF, h % _HALF))
        o = jnp.concatenate([jnp.where(low_half, outs[h], outs[h + 1]) for h in range(0, N_HEADS, _HALF)],
                            axis=1)
        za = jnp.concatenate([za0_ref[new, :], za1_ref[new, :]], axis=1)
        a_ref[new, :] = o * _silu(za)
        return carry

    lax.fori_loop(0, _ATTN_SEQS, one_sequence, 0)


def _attn_sample(p, sinks, cache_k, cache_v):
    nseq = cache_k.shape[0]
    rows = _ATTN_SEQS * DEC_SEQ
    cache_spec = pl.BlockSpec((_ATTN_SEQS, WINDOW, KV_WIDTH), lambda s: (s, 0, 0))
    sink_col = jnp.repeat(sinks.astype(f32), DEC_SEQ).reshape(N_HEADS * DEC_SEQ, 1)
    zcol = _COL["z_a"] // HALF_ATTN
    half_block = lambda col: pl.BlockSpec((rows, HALF_ATTN), lambda s: (s, col))
    return pl.pallas_call(
        _attn_sample_kernel,
        grid=(nseq // _ATTN_SEQS,),
        in_specs=[
            pl.BlockSpec((N_HEADS * DEC_SEQ, 2 * WINDOW), lambda s: (0, 0)),
            pl.BlockSpec((N_HEADS * DEC_SEQ, 1), lambda s: (0, 0)),
            pl.BlockSpec((rows, ATTN_WIDTH), lambda s: (s, _COL["q"] // ATTN_WIDTH)),
            pl.BlockSpec((rows, KV_WIDTH), lambda s: (s, _COL["k"] // KV_WIDTH)),
            pl.BlockSpec((rows, KV_WIDTH), lambda s: (s, _COL["v"] // KV_WIDTH)),
            half_block(zcol), half_block(zcol + 1),
            cache_spec, cache_spec,
        ],
        out_specs=[pl.BlockSpec((rows, ATTN_WIDTH), lambda s: (s, 0)), cache_spec, cache_spec],
        out_shape=[
            jax.ShapeDtypeStruct((nseq * DEC_SEQ, ATTN_WIDTH), f32),
            jax.ShapeDtypeStruct(cache_k.shape, f32),
            jax.ShapeDtypeStruct(cache_v.shape, f32),
        ],
        scratch_shapes=[pltpu.VMEM((2 * WINDOW, KV_WIDTH), f32), pltpu.VMEM((2 * WINDOW, KV_WIDTH), f32)],
        compiler_params=_params(("parallel",)),
        name="attn_sample",
    )(_sample_penalty(), sink_col, p, p, p, p, p, cache_k, cache_v)


def _expand_heads(v):
    low = lax.broadcasted_iota(jnp.int32, (v.shape[0], LANES), 1) < SSM_HEAD_DIM
    tiles = [jnp.where(low, v[:, r:r + 1], v[:, r + 1:r + 2]) for r in range(0, HEADS_PER_GROUP, 2)]
    return jnp.concatenate(tiles, axis=1)


def _split3(v):
    hi = v.astype(bf16)
    rest = v - hi.astype(f32)
    mid = rest.astype(bf16)
    return hi, mid, (rest - mid.astype(f32)).astype(bf16)


def _masked_sums(mask, v_c, v_r):
    m16 = jnp.where(mask, 1.0, 0.0).astype(bf16)
    out_c = sum(jnp.dot(m16, piece, preferred_element_type=f32) for piece in _split3(v_c))
    out_r = sum(lax.dot_general(piece, m16, (((1,), (1,)), ((), ())), preferred_element_type=f32)
                for piece in _split3(v_r))
    return out_c, out_r


_ROW_BIAS, _ROW_D, _ROW_GAIN, _ROW_ALOG, _ROW_ALOG_COL = CONV_W, CONV_W + 1, CONV_W + 2, CONV_W + 3, CONV_W + 4
_CONST_ROWS = _ROW_ALOG_COL + HEADS_PER_GROUP
_CONST_WIDTH = GROUP_WIDTH + 2 * D_STATE
_X_COLS, _B_COLS, _C_COLS = (slice(0, GROUP_WIDTH), slice(GROUP_WIDTH, GROUP_WIDTH + D_STATE),
                             slice(GROUP_WIDTH + D_STATE, _CONST_WIDTH))


def _ssd_constants(conv_w, conv_b, a_log, d_skip, ssm_norm):
    grouped = lambda t, width: t.reshape(t.shape[0], SSM_GROUPS, width).transpose(1, 0, 2)
    padded = lambda t: jnp.pad(t, ((0, 0), (0, 0), (0, _CONST_WIDTH - t.shape[2])))
    taps = jnp.concatenate([conv_w, conv_b[None]], axis=0)
    top = jnp.concatenate([grouped(taps[:, :SSM_WIDTH], GROUP_WIDTH),
                           grouped(taps[:, SSM_WIDTH:SSM_WIDTH + BC_WIDTH], D_STATE),
                           grouped(taps[:, SSM_WIDTH + BC_WIDTH:], D_STATE)], axis=2)
    al = a_log.reshape(SSM_GROUPS, HEADS_PER_GROUP)
    return jnp.concatenate([top, padded(grouped(jnp.repeat(d_skip, SSM_HEAD_DIM)[None], GROUP_WIDTH)),
                            padded(grouped(ssm_norm[None], GROUP_WIDTH)),
                            padded(al[:, None, :]), padded(al[:, :, None])], axis=1)


def _intra_chunk(cb, causal, acum_c, acum_r, dt_r, xc):
    lane = lax.broadcasted_iota(jnp.int32, (CHUNK, LANES), 1)
    pieces = []
    for pair in range(HEADS_PER_GROUP // 2):
        x2 = xc[:, pair * LANES:(pair + 1) * LANES]
        acc = jnp.zeros((CHUNK, LANES), f32)
        for half in range(2):
            r = 2 * pair + half
            keep = (lane < SSM_HEAD_DIM) if half == 0 else (lane >= SSM_HEAD_DIM)
            decay = jnp.exp(jnp.where(causal, acum_c[:, r:r + 1] - acum_r[r:r + 1, :], -jnp.inf))
            acc += jnp.dot((cb * decay * dt_r[r:r + 1, :]).astype(bf16), jnp.where(keep, x2, 0.0).astype(bf16),
                           preferred_element_type=f32)
        pieces.append(acc)
    return jnp.concatenate(pieces, axis=1)


def _gated_norm(y, z, gain):
    u = y * _silu(z)
    ms = jnp.mean(u * u, axis=-1, keepdims=True)
    return u * lax.rsqrt(ms + NORM_EPS) * gain


def _conv_silu(cur, shifted, cst_ref, cols):
    y = cst_ref[_ROW_BIAS:_ROW_BIAS + 1, cols] + cst_ref[CONV_W - 1:CONV_W, cols] * cur
    for k in range(1, CONV_W):
        y = y + cst_ref[CONV_W - 1 - k:CONV_W - k, cols] * shifted[k - 1]
    return _silu(y)


def _decay_sums(mask, dt_c, dt_r, cst_ref):
    a_c = -jnp.exp(cst_ref[_ROW_ALOG:_ROW_ALOG + 1, 0:HEADS_PER_GROUP])
    a_r = -jnp.exp(cst_ref[_ROW_ALOG_COL:_ROW_ALOG_COL + HEADS_PER_GROUP, 0:1])
    return _masked_sums(mask, dt_c * a_c, dt_r * a_r)


def _ssd_prompt_kernel(x_ref, b_ref, c_ref, z_ref, dtc_ref, dtr_ref, cst_ref,
                       y_ref, st_ref, xpad, bpad, cpad, state):
    c = pl.program_id(2)
    tail = 8

    @pl.when(c == 0)
    def _():
        state[...] = jnp.zeros_like(state)
        for pad in (xpad, bpad, cpad):
            pad[:tail, :] = jnp.zeros((tail, pad.shape[1]), f32)

    @pl.when(c > 0)
    def _():
        for pad in (xpad, bpad, cpad):
            pad[:tail, :] = pad[CHUNK:CHUNK + tail, :]

    conv = []
    for raw_ref, pad, cols in ((x_ref, xpad, _X_COLS), (b_ref, bpad, _B_COLS), (c_ref, cpad, _C_COLS)):
        pad[tail:, :] = raw_ref[...]
        shifted = [pad[tail - k:tail - k + CHUNK, :] for k in range(1, CONV_W)]
        conv.append(_conv_silu(raw_ref[...], shifted, cst_ref, cols))
    xc, bm, cm = conv

    dt_c, dt_r = dtc_ref[...], dtr_ref[...]
    row = lax.broadcasted_iota(jnp.int32, (CHUNK, CHUNK), 0)
    col = lax.broadcasted_iota(jnp.int32, (CHUNK, CHUNK), 1)
    causal = row >= col
    acum_c, acum_r = _decay_sums(causal, dt_c, dt_r, cst_ref)
    cb = lax.dot_general(cm.astype(bf16), bm.astype(bf16), (((1,), (1,)), ((), ())),
                         preferred_element_type=f32)
    y = _intra_chunk(cb, causal, acum_c, acum_r, dt_r, xc)

    s_in = state[...]
    decay_in = _expand_heads(jnp.exp(acum_c))
    y += decay_in * jnp.dot(cm.astype(bf16), s_in.astype(bf16), preferred_element_type=f32)
    weight_out = _expand_heads(jnp.exp(acum_c[CHUNK - 1:CHUNK, :] - acum_c) * dt_c)
    s_out = decay_in[CHUNK - 1:CHUNK, :] * s_in + jnp.dot(
        bm.T.astype(bf16), (xc * weight_out).astype(bf16), preferred_element_type=f32)
    state[...] = s_out

    y += cst_ref[_ROW_D:_ROW_D + 1, _X_COLS] * xc
    y_ref[...] = _gated_norm(y, z_ref[...], cst_ref[_ROW_GAIN:_ROW_GAIN + 1, _X_COLS]).astype(y_ref.dtype)

    @pl.when(c == pl.num_programs(2) - 1)
    def _():
        st_ref[...] = s_out.T


def _ssd_prompt(p, dt_c, dt_r, consts, batch):
    nc = SEQ // CHUNK
    rb = lambda b, g, c: b * nc + c
    xcol, zcol = _COL["x"] // GROUP_WIDTH, _COL["z_s"] // GROUP_WIDTH
    bcol, ccol = _COL["B"] // D_STATE, _COL["C"] // D_STATE
    return pl.pallas_call(
        _ssd_prompt_kernel,
        grid=(batch, SSM_GROUPS, nc),
        in_specs=[
            pl.BlockSpec((CHUNK, GROUP_WIDTH), lambda b, g, c: (rb(b, g, c), xcol + g)),
            pl.BlockSpec((CHUNK, D_STATE), lambda b, g, c: (rb(b, g, c), bcol + g)),
            pl.BlockSpec((CHUNK, D_STATE), lambda b, g, c: (rb(b, g, c), ccol + g)),
            pl.BlockSpec((CHUNK, GROUP_WIDTH), lambda b, g, c: (rb(b, g, c), zcol + g)),
            pl.BlockSpec((None, CHUNK, HEADS_PER_GROUP), lambda b, g, c: (g, rb(b, g, c), 0)),
            pl.BlockSpec((None, HEADS_PER_GROUP, CHUNK), lambda b, g, c: (g, 0, rb(b, g, c))),
            pl.BlockSpec((None, _CONST_ROWS, _CONST_WIDTH), lambda b, g, c: (g, 0, 0)),
        ],
        out_specs=[
            pl.BlockSpec((CHUNK, GROUP_WIDTH), lambda b, g, c: (rb(b, g, c), g)),
            pl.BlockSpec((None, GROUP_WIDTH, D_STATE), lambda b, g, c: (b, g, 0)),
        ],
        out_shape=[
            jax.ShapeDtypeStruct((batch * SEQ, SSM_WIDTH), bf16),
            jax.ShapeDtypeStruct((batch, SSM_WIDTH, D_STATE), f32),
        ],
        scratch_shapes=[
            pltpu.VMEM((CHUNK + 8, GROUP_WIDTH), f32),
            pltpu.VMEM((CHUNK + 8, D_STATE), f32),
            pltpu.VMEM((CHUNK + 8, D_STATE), f32),
            pltpu.VMEM((D_STATE, GROUP_WIDTH), f32),
        ],
        compiler_params=_params(("parallel", "parallel", "arbitrary")),
        name="ssd_prompt",
    )(p, p, p, p, dt_c, dt_r, consts)


def _ssd_sample_kernel(x_ref, b_ref, c_ref, z_ref, dtc_ref, dtr_ref, cst_ref,
                       cx_ref, cbuf_ref, cc_ref, s0_ref, y_ref, s1_ref, xprev, bprev, cprev):
    row1 = lax.broadcasted_iota(jnp.int32, (CHUNK, 1), 0)
    conv = []
    for raw_ref, buf_ref, prev, cols in ((x_ref, cx_ref, xprev, _X_COLS), (b_ref, cbuf_ref, bprev, _B_COLS),
                                         (c_ref, cc_ref, cprev, _C_COLS)):
        prev[...] = jnp.zeros_like(prev)
        prev[:, :CONV_W - 1, :] = buf_ref[...]
        cached = prev[...].reshape(CHUNK, prev.shape[2])
        cur = raw_ref[...]
        shifted = []
        for k in range(1, CONV_W):
            from_cache = pltpu.roll(cached, (k - (CONV_W - 1)) % CHUNK, 0)
            shifted.append(jnp.where(_mod(row1, DEC_SEQ) >= k, pltpu.roll(cur, k, 0), from_cache))
        conv.append(_conv_silu(cur, shifted, cst_ref, cols))
    xc, bm, cm = conv

    dt_c, dt_r = dtc_ref[...], dtr_ref[...]
    row = lax.broadcasted_iota(jnp.int32, (CHUNK, CHUNK), 0)
    col = lax.broadcasted_iota(jnp.int32, (CHUNK, CHUNK), 1)
    same_seq = _div(row, DEC_SEQ) == _div(col, DEC_SEQ)
    causal = same_seq & (row >= col)
    acum_c, acum_r = _decay_sums(causal, dt_c, dt_r, cst_ref)
    cb = lax.dot_general(cm.astype(bf16), bm.astype(bf16), (((1,), (1,)), ((), ())),
                         preferred_element_type=f32)
    y = _intra_chunk(cb, causal, acum_c, acum_r, dt_r, xc)

    to_end, _ = _decay_sums(same_seq & (row < col), dt_c, dt_r, cst_ref)
    decay_in = _expand_heads(jnp.exp(acum_c))
    xw = (xc * _expand_heads(jnp.exp(to_end) * dt_c)).T.astype(bf16)
    keep = jnp.exp(acum_r)
    seq_of_row = _div(row1, DEC_SEQ)
    inter = jnp.zeros((CHUNK, GROUP_WIDTH), f32)
    for n in range(SEQS_PER_STEP):
        mine = seq_of_row == n
        s0 = s0_ref[n]
        inter += lax.dot_general(jnp.where(mine, cm, 0.0).astype(bf16), s0.astype(bf16),
                                 (((1,), (1,)), ((), ())), preferred_element_type=f32)
        update = jnp.dot(xw, jnp.where(mine, bm, 0.0).astype(bf16), preferred_element_type=f32)
        last_lane = (n + 1) * DEC_SEQ - 1
        for r in range(HEADS_PER_GROUP):
            rows = slice(r * SSM_HEAD_DIM, (r + 1) * SSM_HEAD_DIM)
            s1_ref[n, rows, :] = keep[r:r + 1, last_lane:last_lane + 1] * s0[rows] + update[rows]
    y += decay_in * inter + cst_ref[_ROW_D:_ROW_D + 1, _X_COLS] * xc
    y_ref[...] = _gated_norm(y, z_ref[...], cst_ref[_ROW_GAIN:_ROW_GAIN + 1, _X_COLS]).astype(y_ref.dtype)


def _ssd_sample(p, dt_c, dt_r, consts, conv_state, s0):
    nseq = s0.shape[0]
    nsb = nseq // SEQS_PER_STEP
    xcol, zcol = _COL["x"] // GROUP_WIDTH, _COL["z_s"] // GROUP_WIDTH
    bcol, ccol = _COL["B"] // D_STATE, _COL["C"] // D_STATE
    conv_block = lambda width, first: pl.BlockSpec(
        (SEQS_PER_STEP, CONV_W - 1, width), lambda s, g: (s, 0, first + g))
    state_spec = pl.BlockSpec((SEQS_PER_STEP, GROUP_WIDTH, D_STATE), lambda s, g: (s, g, 0))
    return pl.pallas_call(
        _ssd_sample_kernel,
        grid=(nsb, SSM_GROUPS),
        in_specs=[
            pl.BlockSpec((CHUNK, GROUP_WIDTH), lambda s, g: (s, xcol + g)),
            pl.BlockSpec((CHUNK, D_STATE), lambda s, g: (s, bcol + g)),
            pl.BlockSpec((CHUNK, D_STATE), lambda s, g: (s, ccol + g)),
            pl.BlockSpec((CHUNK, GROUP_WIDTH), lambda s, g: (s, zcol + g)),
            pl.BlockSpec((None, CHUNK, HEADS_PER_GROUP), lambda s, g: (g, s, 0)),
            pl.BlockSpec((None, HEADS_PER_GROUP, CHUNK), lambda s, g: (g, 0, s)),
            pl.BlockSpec((None, _CONST_ROWS, _CONST_WIDTH), lambda s, g: (g, 0, 0)),
            conv_block(GROUP_WIDTH, 0),
            conv_block(D_STATE, SSM_WIDTH // D_STATE),
            conv_block(D_STATE, (SSM_WIDTH + BC_WIDTH) // D_STATE),
            state_spec,
        ],
        out_specs=[pl.BlockSpec((CHUNK, GROUP_WIDTH), lambda s, g: (s, g)), state_spec],
        out_shape=[
            jax.ShapeDtypeStruct((nseq * DEC_SEQ, SSM_WIDTH), bf16),
            jax.ShapeDtypeStruct(s0.shape, f32),
        ],
        scratch_shapes=[
            pltpu.VMEM((SEQS_PER_STEP, 8, GROUP_WIDTH), f32),
            pltpu.VMEM((SEQS_PER_STEP, 8, D_STATE), f32),
            pltpu.VMEM((SEQS_PER_STEP, 8, D_STATE), f32),
        ],
        compiler_params=_params(("parallel", "parallel")),
        name="ssd_sample",
    )(p, p, p, p, dt_c, dt_r, consts, conv_state, conv_state, conv_state, s0)


def _merge_kernel(a_ref, s_ref, ga_ref, gs_ref, wa_ref, ws_ref, o_ref):
    ya = jnp.dot(a_ref[...].astype(bf16), wa_ref[...], preferred_element_type=f32)
    ys = jnp.dot(s_ref[...], ws_ref[...], preferred_element_type=f32)
    o_ref[...] = (jax.nn.sigmoid(ga_ref[...]) * ya + jax.nn.sigmoid(gs_ref[...]) * ys).astype(o_ref.dtype)


def _merge(attn, ssm, p, wa, ws, tm=1024, tn=512):
    m = attn.shape[0]
    return pl.pallas_call(
        _merge_kernel,
        grid=(m // tm, D_MODEL // tn),
        in_specs=[
            pl.BlockSpec((tm, ATTN_WIDTH), lambda i, j: (i, 0)),
            pl.BlockSpec((tm, SSM_WIDTH), lambda i, j: (i, 0)),
            pl.BlockSpec((tm, tn), lambda i, j: (i, _COL["g_a"] // tn + j)),
            pl.BlockSpec((tm, tn), lambda i, j: (i, _COL["g_s"] // tn + j)),
            pl.BlockSpec((ATTN_WIDTH, tn), lambda i, j: (0, j)),
            pl.BlockSpec((SSM_WIDTH, tn), lambda i, j: (0, j)),
        ],
        out_specs=pl.BlockSpec((tm, tn), lambda i, j: (i, j)),
        out_shape=jax.ShapeDtypeStruct((m, D_MODEL), bf16),
        compiler_params=_params(("parallel", "arbitrary")),
        name="merge",
    )(attn, ssm, p, p, wa, ws)


def _out_kernel(m_ref, wo_ref, x_ref, g_ref, o_ref):
    y = jnp.dot(m_ref[...], wo_ref[...], preferred_element_type=f32)
    ms = jnp.mean(y * y, axis=-1, keepdims=True)
    o_ref[...] = x_ref[...] + y * lax.rsqrt(ms + NORM_EPS) * g_ref[...]


def _out(merged, wo, x, g, tm=512):
    m = x.shape[0]
    return pl.pallas_call(
        _out_kernel,
        grid=(m // tm,),
        in_specs=[
            pl.BlockSpec((tm, D_MODEL), lambda i: (i, 0)),
            pl.BlockSpec((D_MODEL, D_MODEL), lambda i: (0, 0)),
            pl.BlockSpec((tm, D_MODEL), lambda i: (i, 0)),
            pl.BlockSpec((1, D_MODEL), lambda i: (0, 0)),
        ],
        out_specs=pl.BlockSpec((tm, D_MODEL), lambda i: (i, 0)),
        out_shape=jax.ShapeDtypeStruct((m, D_MODEL), f32),
        compiler_params=_params(("parallel",)),
        name="outproj",
    )(merged, wo, x, g)


def _group_layouts(dt, rows):
    d = dt[:, :SSM_HEADS].reshape(rows, SSM_GROUPS, HEADS_PER_GROUP)
    return d.transpose(1, 0, 2), d.transpose(1, 2, 0)


def _layer(xp, xs, cache_k, cache_v, state_ssm, state_conv, norm_pre, w_in, conv_w, conv_b, dt_bias,
           a_log, d_skip, ssm_norm, attn_sinks, w_attn_br, w_ssm_br, w_out, norm_post):
    batch, nseq = xp.shape[0], xs.shape[0]
    mp, ms = batch * SEQ, nseq * DEC_SEQ
    src = lambda name, width: w_in[:, _SRC[name]:_SRC[name] + width]
    w_main = w_in[:, :_SRC["dt"]].astype(bf16)
    w_gate = w_in[:, _SRC["g_a"]:].astype(bf16)
    w_dt = jnp.pad(src("dt", SSM_HEADS), ((0, 0), (0, LANES - SSM_HEADS))).astype(bf16)
    dtb = jnp.pad(dt_bias, (0, LANES - SSM_HEADS)).reshape(1, LANES)
    g_pre = norm_pre.reshape(1, D_MODEL)

    ssd_consts = _ssd_constants(conv_w, conv_b, a_log, d_skip, ssm_norm)
    wa, ws, wo = w_attn_br.astype(bf16), w_ssm_br.astype(bf16), w_out.astype(bf16)
    g_post = norm_post.reshape(1, D_MODEL)

    xp2, xs2 = xp.reshape(mp, D_MODEL), xs.reshape(ms, D_MODEL)
    pp, dtp = _inproj(xp2, g_pre, w_main, w_gate, w_dt, dtb)
    ps, dts = _inproj(xs2, g_pre, w_main, w_gate, w_dt, dtb)

    attn_p = _attn_prompt(pp, attn_sinks, batch)
    ssm_p, st_p = _ssd_prompt(pp, *_group_layouts(dtp, mp), ssd_consts, batch)
    yp = _out(_merge(attn_p, ssm_p, pp, wa, ws), wo, xp2, g_post)
    pp3 = pp.reshape(batch, SEQ, P_WIDTH)
    k_p = pp3[:, SEQ - WINDOW:, _COL["k"]:_COL["k"] + KV_WIDTH]
    v_p = pp3[:, SEQ - WINDOW:, _COL["v"]:_COL["v"] + KV_WIDTH]
    conv_p = pp3[:, SEQ - (CONV_W - 1):, _COL["x"]:_COL["x"] + CONV_DIM]

    attn_s, k_s, v_s = _attn_sample(ps, attn_sinks, cache_k.reshape(nseq, WINDOW, KV_WIDTH),
                                    cache_v.reshape(nseq, WINDOW, KV_WIDTH))
    ssm_s, st_s = _ssd_sample(ps, *_group_layouts(dts, ms), ssd_consts, state_conv,
                              state_ssm.reshape(nseq, SSM_WIDTH, D_STATE))
    ys = _out(_merge(attn_s, ssm_s, ps, wa, ws), wo, xs2, g_post)
    conv_s = ps.reshape(nseq, DEC_SEQ, P_WIDTH)[:, DEC_SEQ - (CONV_W - 1):, _COL["x"]:_COL["x"] + CONV_DIM]

    kv = lambda t, n: t.reshape(1, n, WINDOW, N_KV_HEADS, HEAD_DIM)
    st = lambda t, n: t.reshape(1, n, SSM_HEADS, SSM_HEAD_DIM, D_STATE)
    return (yp.reshape(xp.shape), ys.reshape(xs.shape), kv(k_p, batch), kv(v_p, batch), st(st_p, batch),
            conv_p[None], kv(k_s, nseq), kv(v_s, nseq), st(st_s, nseq), conv_s[None])


def kernel(x_prompt, x_sample, cache_k, cache_v, state_ssm, state_conv, norm_pre, w_in, conv_w, conv_b,
           dt_bias, a_log, d_skip, ssm_norm, attn_sinks, w_attn_br, w_ssm_br, w_out, norm_post):
    assert w_in.shape[0] == 1, "single-layer trunk"
    return _layer(x_prompt, x_sample, cache_k[0], cache_v[0], state_ssm[0], state_conv[0], norm_pre[0],
                  w_in[0], conv_w[0], conv_b[0], dt_bias[0], a_log[0], d_skip[0], ssm_norm[0],
                  attn_sinks[0], w_attn_br[0], w_ssm_br[0], w_out[0], norm_post[0])
```

```python
import functools

import jax
import jax.numpy as jnp
from jax import lax
from jax.experimental import pallas as pl
from jax.experimental.pallas import tpu as pltpu

f32 = jnp.float32
bf16 = jnp.bfloat16

D_MODEL = 2048
SEQ = 4096
DEC_SEQ = 8
N_HEADS = 32
N_KV_HEADS = 8
HEAD_DIM = 64
Q_PER_KV = N_HEADS // N_KV_HEADS
ATTN_WIDTH = N_HEADS * HEAD_DIM
KV_WIDTH = N_KV_HEADS * HEAD_DIM
WINDOW = 128
SSM_WIDTH = 2 * D_MODEL
SSM_HEAD_DIM = 64
SSM_HEADS = SSM_WIDTH // SSM_HEAD_DIM
SSM_GROUPS = 8
HEADS_PER_GROUP = SSM_HEADS // SSM_GROUPS
GROUP_WIDTH = HEADS_PER_GROUP * SSM_HEAD_DIM
D_STATE = 128
CONV_W = 4
BC_WIDTH = SSM_GROUPS * D_STATE
CONV_DIM = SSM_WIDTH + 2 * BC_WIDTH
CHUNK = 128
NORM_EPS = 1e-6

_SRC = dict(q=0, k=2048, v=2560, z_a=3072, xbc=5120, z_s=11264, dt=15360, g_a=15424, g_s=17472)
_COL = dict(q=0, k=2048, v=2560, z_a=3072, x=5120, B=9216, C=10240, z_s=11264, g_a=15360, g_s=17408)
P_WIDTH = 19456
_GATE_COL = _COL["g_a"]
HALF_ATTN = ATTN_WIDTH // 2
LANES = 128
SEQS_PER_STEP = 16

_VMEM_LIMIT = 56 * 1024 * 1024


def _params(sem):
    return pltpu.CompilerParams(dimension_semantics=sem, vmem_limit_bytes=_VMEM_LIMIT)


def _silu(v):
    half = 0.5 * v
    return half + half * jnp.tanh(half)


def _div(v, n):
    assert n & (n - 1) == 0
    return v >> (n.bit_length() - 1)


def _mod(v, n):
    assert n & (n - 1) == 0
    return v & (n - 1)


_NORM_ROWS = 128


def _cast_weights_kernel(main_ref, gate_ref, o_ref, *, main_blocks):
    j = pl.program_id(0)

    @pl.when(j < main_blocks)
    def _():
        o_ref[...] = main_ref[...].astype(o_ref.dtype)

    @pl.when(j >= main_blocks)
    def _():
        o_ref[...] = gate_ref[...].astype(o_ref.dtype)


def _cast_weights(w_in, w_gate, tn=1024):
    main_blocks = _SRC["dt"] // tn
    return pl.pallas_call(
        functools.partial(_cast_weights_kernel, main_blocks=main_blocks),
        grid=(P_WIDTH // tn,),
        in_specs=[
            pl.BlockSpec((D_MODEL, tn), lambda j: (0, jnp.minimum(j, main_blocks - 1))),
            pl.BlockSpec((D_MODEL, tn), lambda j: (0, jnp.maximum(j - main_blocks, 0))),
        ],
        out_specs=pl.BlockSpec((D_MODEL, tn), lambda j: (0, j)),
        out_shape=jax.ShapeDtypeStruct((D_MODEL, P_WIDTH), bf16),
        compiler_params=_params(("parallel",)),
        name="cast_weights",
    )(w_in, w_gate)


def _inproj_kernel(x_ref, g_ref, w_ref, wdt_ref, dtb_ref, p_ref, dt_ref, h_ref):
    @pl.when(pl.program_id(1) == 0)
    def _():
        def norm_rows(i, carry):
            rows = pl.ds(pl.multiple_of(i * _NORM_ROWS, _NORM_ROWS), _NORM_ROWS)
            x = x_ref[rows, :]
            ms = jnp.mean(x * x, axis=-1, keepdims=True)
            h = (x * lax.rsqrt(ms + NORM_EPS) * g_ref[...]).astype(bf16)
            h_ref[rows, :] = h
            v = jnp.dot(h, wdt_ref[...], preferred_element_type=f32) + dtb_ref[...]
            dt_ref[rows, :] = jnp.maximum(v, 0.0) + jnp.log1p(jnp.exp(-jnp.abs(v)))
            return carry

        lax.fori_loop(0, x_ref.shape[0] // _NORM_ROWS, norm_rows, 0)

    p_ref[...] = jnp.dot(h_ref[...], w_ref[...], preferred_element_type=f32)


def _inproj(x, g, w, wdt, dtb, tm=1024, tn=1024):
    m = x.shape[0]
    return pl.pallas_call(
        _inproj_kernel,
        grid=(m // tm, P_WIDTH // tn),
        in_specs=[
            pl.BlockSpec((tm, D_MODEL), lambda i, j: (i, 0)),
            pl.BlockSpec((1, D_MODEL), lambda i, j: (0, 0)),
            pl.BlockSpec((D_MODEL, tn), lambda i, j: (0, j)),
            pl.BlockSpec((D_MODEL, LANES), lambda i, j: (0, 0)),
            pl.BlockSpec((1, LANES), lambda i, j: (0, 0)),
        ],
        out_specs=[
            pl.BlockSpec((tm, tn), lambda i, j: (i, j)),
            pl.BlockSpec((tm, LANES), lambda i, j: (i, 0)),
        ],
        out_shape=[jax.ShapeDtypeStruct((m, P_WIDTH), f32), jax.ShapeDtypeStruct((m, LANES), f32)],
        scratch_shapes=[pltpu.VMEM((tm, D_MODEL), bf16)],
        compiler_params=_params(("parallel", "arbitrary")),
        name="inproj",
    )(x, g, w, wdt, dtb)


_HALF = LANES // HEAD_DIM
assert _HALF == 2


def _alibi_slopes():
    return jnp.exp2(-8.0 * jnp.arange(1, N_HEADS + 1, dtype=f32) / N_HEADS)


def _prompt_penalty():
    s = jnp.arange(WINDOW)[:, None]
    q = jnp.arange(WINDOW)[None, :]
    dist = jnp.where(s <= q, q - s, WINDOW + q - s).astype(f32)
    pen = (_alibi_slopes()[:, None, None] * dist[None]).reshape(N_KV_HEADS, Q_PER_KV, WINDOW, WINDOW)
    return pen.transpose(0, 2, 1, 3).reshape(N_KV_HEADS, WINDOW, Q_PER_KV * WINDOW)


def _sample_penalty():
    i = jnp.arange(DEC_SEQ)[:, None]
    c = jnp.arange(2 * WINDOW)[None, :]
    dist = WINDOW + i - c
    valid = (dist >= 0) & (dist < WINDOW) & (c < WINDOW + DEC_SEQ)
    pen = _alibi_slopes()[:, None, None] * dist.astype(f32)[None]
    return jnp.where(valid[None], pen, jnp.inf).reshape(N_HEADS * DEC_SEQ, 2 * WINDOW)


def _attn_prompt_kernel(sink_ref, pen_ref, q_ref, kc_ref, kp_ref, vc_ref, vp_ref, za0_ref, za1_ref, a_ref):
    cols4 = Q_PER_KV * WINDOW
    key = lax.broadcasted_iota(jnp.int32, (WINDOW, cols4), 0)
    qry = _mod(lax.broadcasted_iota(jnp.int32, (WINDOW, cols4), 1), WINDOW)
    from_cur = key <= qry
    low_half = lax.broadcasted_iota(jnp.int32, (WINDOW, LANES), 1) < HEAD_DIM
    prev_off = jnp.where(pl.program_id(1) > 0, 0.0, -jnp.inf)
    nt = (((1,), (1,)), ((), ()))
    kv_tiles = {}
    for j in range(N_KV_HEADS):
        if j % _HALF == 0:
            tile = slice((j // _HALF) * LANES, (j // _HALF + 1) * LANES)
            kv_tiles = dict(k_cur=kc_ref[:, tile].astype(bf16), k_prev=kp_ref[:, tile].astype(bf16),
                            v_cur=vc_ref[:, tile].T.astype(bf16), v_prev=vp_ref[:, tile].T.astype(bf16))
        mine = low_half == (j % _HALF == 0)
        pieces = []
        for h in range(j * Q_PER_KV, (j + 1) * Q_PER_KV):
            piece = q_ref[:, (h // _HALF) * LANES:(h // _HALF + 1) * LANES] * HEAD_DIM ** -0.5
            if h % _HALF != j % _HALF:
                piece = pltpu.roll(piece, HEAD_DIM, 1)
            pieces.append(jnp.where(mine, piece, 0.0))
        q = jnp.concatenate(pieces, axis=0).astype(bf16)
        s_cur = lax.dot_general(kv_tiles["k_cur"], q, nt, preferred_element_type=f32)
        s_prev = lax.dot_general(kv_tiles["k_prev"], q, nt, preferred_element_type=f32)
        t = jnp.where(from_cur, s_cur, s_prev + prev_off) - pen_ref[j]
        sinks = jnp.concatenate([jnp.full((1, WINDOW), sink_ref[j * Q_PER_KV + g], f32)
                                 for g in range(Q_PER_KV)], axis=1)
        m = jnp.maximum(jnp.max(t, axis=0, keepdims=True), sinks)
        p = jnp.exp(t - m)
        inv = 1.0 / (jnp.sum(p, axis=0, keepdims=True) + jnp.exp(sinks - m))
        o = jnp.dot(kv_tiles["v_cur"], jnp.where(from_cur, p, 0.0).astype(bf16), preferred_element_type=f32)
        o += jnp.dot(kv_tiles["v_prev"], jnp.where(from_cur, 0.0, p).astype(bf16), preferred_element_type=f32)
        o = o * inv
        dims = slice((j % _HALF) * HEAD_DIM, (j % _HALF + 1) * HEAD_DIM)
        for pair in range(Q_PER_KV // _HALF):
            g0 = pair * _HALF
            two_heads = jnp.concatenate([o[dims, g * WINDOW:(g + 1) * WINDOW] for g in (g0, g0 + 1)], axis=0)
            first = (j * Q_PER_KV // _HALF + pair) * LANES
            za_ref = (za0_ref, za1_ref)[first // HALF_ATTN]
            za = za_ref[:, first % HALF_ATTN:first % HALF_ATTN + LANES]
            a_ref[:, first:first + LANES] = (two_heads.T * _silu(za)).astype(a_ref.dtype)


def _attn_prompt(p, sinks, batch):
    nb = SEQ // WINDOW
    kcol, vcol = _COL["k"] // KV_WIDTH, _COL["v"] // KV_WIDTH
    zcol = _COL["z_a"] // HALF_ATTN
    cur = lambda b, i: b * nb + i
    prev = lambda b, i: b * nb + jnp.maximum(i - 1, 0)
    half_block = lambda col: pl.BlockSpec((WINDOW, HALF_ATTN), lambda b, i: (cur(b, i), col))
    return pl.pallas_call(
        _attn_prompt_kernel,
        grid=(batch, nb),
        in_specs=[
            pl.BlockSpec(memory_space=pltpu.SMEM),
            pl.BlockSpec((N_KV_HEADS, WINDOW, Q_PER_KV * WINDOW), lambda b, i: (0, 0, 0)),
            pl.BlockSpec((WINDOW, ATTN_WIDTH), lambda b, i: (cur(b, i), _COL["q"] // ATTN_WIDTH)),
            pl.BlockSpec((WINDOW, KV_WIDTH), lambda b, i: (cur(b, i), kcol)),
            pl.BlockSpec((WINDOW, KV_WIDTH), lambda b, i: (prev(b, i), kcol)),
            pl.BlockSpec((WINDOW, KV_WIDTH), lambda b, i: (cur(b, i), vcol)),
            pl.BlockSpec((WINDOW, KV_WIDTH), lambda b, i: (prev(b, i), vcol)),
            half_block(zcol), half_block(zcol + 1),
        ],
        out_specs=pl.BlockSpec((WINDOW, ATTN_WIDTH), lambda b, i: (cur(b, i), 0)),
        out_shape=jax.ShapeDtypeStruct((batch * SEQ, ATTN_WIDTH), bf16),
        compiler_params=_params(("parallel", "parallel")),
        name="attn_prompt",
    )(sinks, _prompt_penalty(), p, p, p, p, p, p, p)


_ATTN_SEQS = 8


def _attn_sample_kernel(pen_ref, sink_ref, q_ref, kn_ref, vn_ref, za0_ref, za1_ref, ck_ref, cv_ref,
                        a_ref, ko_ref, vo_ref, ak_ref, av_ref):
    keys = 2 * WINDOW
    pad = jnp.zeros((keys - WINDOW - DEC_SEQ, KV_WIDTH), f32)
    ak_ref[WINDOW + DEC_SEQ:, :] = pad
    av_ref[WINDOW + DEC_SEQ:, :] = pad
    low_half = lax.broadcasted_iota(jnp.int32, (DEC_SEQ, LANES), 1) < HEAD_DIM
    nt = (((1,), (1,)), ((), ()))

    def to_half(piece, src, dst):
        return piece if src == dst else pltpu.roll(piece, HEAD_DIM, 1)

    def one_sequence(n, carry):
        new = pl.ds(pl.multiple_of(n * DEC_SEQ, DEC_SEQ), DEC_SEQ)
        for cache_ref, new_ref, all_ref, out_ref in ((ck_ref, kn_ref, ak_ref, ko_ref),
                                                     (cv_ref, vn_ref, av_ref, vo_ref)):
            all_ref[:WINDOW, :] = cache_ref[n]
            all_ref[WINDOW:WINDOW + DEC_SEQ, :] = new_ref[new, :]
            out_ref[n] = all_ref[DEC_SEQ:WINDOW + DEC_SEQ, :]
        k16, v16 = ak_ref[...].astype(bf16), av_ref[...].astype(bf16)
        scores = []
        for j in range(N_KV_HEADS):
            pieces = []
            for h in range(j * Q_PER_KV, (j + 1) * Q_PER_KV):
                piece = q_ref[new, (h // _HALF) * LANES:(h // _HALF + 1) * LANES] * HEAD_DIM ** -0.5
                piece = to_half(piece, h % _HALF, j % _HALF)
                pieces.append(jnp.where(low_half == (j % _HALF == 0), piece, 0.0))
            qj = jnp.concatenate(pieces, axis=0).astype(bf16)
            kj = k16[:, (j // _HALF) * LANES:(j // _HALF + 1) * LANES]
            scores.append(lax.dot_general(qj, kj, nt, preferred_element_type=f32))
        t = jnp.concatenate(scores, axis=0) - pen_ref[...]
        sinks = sink_ref[...]
        m = jnp.maximum(jnp.max(t, axis=-1, keepdims=True), sinks)
        p = jnp.exp(t - m)
        inv = 1.0 / (jnp.sum(p, axis=-1, keepdims=True) + jnp.exp(sinks - m))
        p16 = p.astype(bf16)
        rows_per_kv = Q_PER_KV * DEC_SEQ
        outs = []
        for j in range(N_KV_HEADS):
            rows = slice(j * rows_per_kv, (j + 1) * rows_per_kv)
            vj = v16[:, (j // _HALF) * LANES:(j // _HALF + 1) * LANES]
            oj = jnp.dot(p16[rows], vj, preferred_element_type=f32) * inv[rows]
            for g in range(Q_PER_KV):
                h = j * Q_PER_KV + g
                outs.append(to_half(oj[g * DEC_SEQ:(g + 1) * DEC_SEQ], j % _HALF, h % _HALF))
        o = jnp.concatenate([jnp.where(low_half, outs[h], outs[h + 1]) for h in range(0, N_HEADS, _HALF)],
                            axis=1)
        za = jnp.concatenate([za0_ref[new, :], za1_ref[new, :]], axis=1)
        a_ref[new, :] = o * _silu(za)
        return carry

    lax.fori_loop(0, _ATTN_SEQS, one_sequence, 0)


def _attn_sample(p, sinks, cache_k, cache_v):
    nseq = cache_k.shape[0]
    rows = _ATTN_SEQS * DEC_SEQ
    cache_spec = pl.BlockSpec((_ATTN_SEQS, WINDOW, KV_WIDTH), lambda s: (s, 0, 0))
    sink_col = jnp.repeat(sinks.astype(f32), DEC_SEQ).reshape(N_HEADS * DEC_SEQ, 1)
    zcol = _COL["z_a"] // HALF_ATTN
    half_block = lambda col: pl.BlockSpec((rows, HALF_ATTN), lambda s: (s, col))
    return pl.pallas_call(
        _attn_sample_kernel,
        grid=(nseq // _ATTN_SEQS,),
        in_specs=[
            pl.BlockSpec((N_HEADS * DEC_SEQ, 2 * WINDOW), lambda s: (0, 0)),
            pl.BlockSpec((N_HEADS * DEC_SEQ, 1), lambda s: (0, 0)),
            pl.BlockSpec((rows, ATTN_WIDTH), lambda s: (s, _COL["q"] // ATTN_WIDTH)),
            pl.BlockSpec((rows, KV_WIDTH), lambda s: (s, _COL["k"] // KV_WIDTH)),
            pl.BlockSpec((rows, KV_WIDTH), lambda s: (s, _COL["v"] // KV_WIDTH)),
            half_block(zcol), half_block(zcol + 1),
            cache_spec, cache_spec,
        ],
        out_specs=[pl.BlockSpec((rows, ATTN_WIDTH), lambda s: (s, 0)), cache_spec, cache_spec],
        out_shape=[
            jax.ShapeDtypeStruct((nseq * DEC_SEQ, ATTN_WIDTH), f32),
            jax.ShapeDtypeStruct(cache_k.shape, f32),
            jax.ShapeDtypeStruct(cache_v.shape, f32),
        ],
        scratch_shapes=[pltpu.VMEM((2 * WINDOW, KV_WIDTH), f32), pltpu.VMEM((2 * WINDOW, KV_WIDTH), f32)],
        compiler_params=_params(("parallel",)),
        name="attn_sample",
    )(_sample_penalty(), sink_col, p, p, p, p, p, cache_k, cache_v)


def _expand_heads(v):
    low = lax.broadcasted_iota(jnp.int32, (v.shape[0], LANES), 1) < SSM_HEAD_DIM
    tiles = [jnp.where(low, v[:, r:r + 1], v[:, r + 1:r + 2]) for r in range(0, HEADS_PER_GROUP, 2)]
    return jnp.concatenate(tiles, axis=1)


def _split3(v):
    hi = v.astype(bf16)
    rest = v - hi.astype(f32)
    mid = rest.astype(bf16)
    return hi, mid, (rest - mid.astype(f32)).astype(bf16)


def _masked_sums(mask, v_c, v_r):
    m16 = jnp.where(mask, 1.0, 0.0).astype(bf16)
    out_c = sum(jnp.dot(m16, piece, preferred_element_type=f32) for piece in _split3(v_c))
    out_r = sum(lax.dot_general(piece, m16, (((1,), (1,)), ((), ())), preferred_element_type=f32)
                for piece in _split3(v_r))
    return out_c, out_r


_ROW_BIAS, _ROW_D, _ROW_GAIN, _ROW_ALOG, _ROW_ALOG_COL = CONV_W, CONV_W + 1, CONV_W + 2, CONV_W + 3, CONV_W + 4
_CONST_ROWS = _ROW_ALOG_COL + HEADS_PER_GROUP
_CONST_WIDTH = GROUP_WIDTH + 2 * D_STATE
_X_COLS, _B_COLS, _C_COLS = (slice(0, GROUP_WIDTH), slice(GROUP_WIDTH, GROUP_WIDTH + D_STATE),
                             slice(GROUP_WIDTH + D_STATE, _CONST_WIDTH))


def _ssd_constants(conv_w, conv_b, a_log, d_skip, ssm_norm):
    grouped = lambda t, width: t.reshape(t.shape[0], SSM_GROUPS, width).transpose(1, 0, 2)
    padded = lambda t: jnp.pad(t, ((0, 0), (0, 0), (0, _CONST_WIDTH - t.shape[2])))
    taps = jnp.concatenate([conv_w, conv_b[None]], axis=0)
    top = jnp.concatenate([grouped(taps[:, :SSM_WIDTH], GROUP_WIDTH),
                           grouped(taps[:, SSM_WIDTH:SSM_WIDTH + BC_WIDTH], D_STATE),
                           grouped(taps[:, SSM_WIDTH + BC_WIDTH:], D_STATE)], axis=2)
    al = a_log.reshape(SSM_GROUPS, HEADS_PER_GROUP)
    return jnp.concatenate([top, padded(grouped(jnp.repeat(d_skip, SSM_HEAD_DIM)[None], GROUP_WIDTH)),
                            padded(grouped(ssm_norm[None], GROUP_WIDTH)),
                            padded(al[:, None, :]), padded(al[:, :, None])], axis=1)


def _intra_chunk(cb, causal, acum_c, acum_r, dt_r, xc):
    lane = lax.broadcasted_iota(jnp.int32, (CHUNK, LANES), 1)
    pieces = []
    for pair in range(HEADS_PER_GROUP // 2):
        x2 = xc[:, pair * LANES:(pair + 1) * LANES]
        acc = jnp.zeros((CHUNK, LANES), f32)
        for half in range(2):
            r = 2 * pair + half
            keep = (lane < SSM_HEAD_DIM) if half == 0 else (lane >= SSM_HEAD_DIM)
            decay = jnp.exp(jnp.where(causal, acum_c[:, r:r + 1] - acum_r[r:r + 1, :], -jnp.inf))
            acc += jnp.dot((cb * decay * dt_r[r:r + 1, :]).astype(bf16), jnp.where(keep, x2, 0.0).astype(bf16),
                           preferred_element_type=f32)
        pieces.append(acc)
    return jnp.concatenate(pieces, axis=1)


def _gated_norm(y, z, gain):
    u = y * _silu(z)
    ms = jnp.mean(u * u, axis=-1, keepdims=True)
    return u * lax.rsqrt(ms + NORM_EPS) * gain


def _conv_silu(cur, shifted, cst_ref, cols):
    y = cst_ref[_ROW_BIAS:_ROW_BIAS + 1, cols] + cst_ref[CONV_W - 1:CONV_W, cols] * cur
    for k in range(1, CONV_W):
        y = y + cst_ref[CONV_W - 1 - k:CONV_W - k, cols] * shifted[k - 1]
    return _silu(y)


def _decay_sums(mask, dt_c, dt_r, cst_ref):
    a_c = -jnp.exp(cst_ref[_ROW_ALOG:_ROW_ALOG + 1, 0:HEADS_PER_GROUP])
    a_r = -jnp.exp(cst_ref[_ROW_ALOG_COL:_ROW_ALOG_COL + HEADS_PER_GROUP, 0:1])
    return _masked_sums(mask, dt_c * a_c, dt_r * a_r)


_SSM_PARTS = SSM_WIDTH // BC_WIDTH


def _ssd_prompt_kernel(*refs):
    x_refs, refs = refs[:_SSM_PARTS], refs[_SSM_PARTS:]
    z_refs, refs = refs[:_SSM_PARTS], refs[_SSM_PARTS:]
    b_ref, c_ref, dtc_ref, dtr_ref, cst_ref, y_ref, st_ref, xpad, bpad, cpad, state = refs
    groups_per_part = SSM_GROUPS // _SSM_PARTS
    c = pl.program_id(1)
    tail = 8

    @pl.when(c == 0)
    def _():
        state[...] = jnp.zeros_like(state)
        for pad in (xpad, bpad, cpad):
            pad[:tail, :] = jnp.zeros((tail, pad.shape[1]), f32)

    @pl.when(c > 0)
    def _():
        for pad in (xpad, bpad, cpad):
            pad[:tail, :] = pad[CHUNK:CHUNK + tail, :]

    for i, part_ref in enumerate(x_refs):
        xpad[tail:, i * BC_WIDTH:(i + 1) * BC_WIDTH] = part_ref[...]
    bpad[tail:, :] = b_ref[...]
    cpad[tail:, :] = c_ref[...]
    row = lax.broadcasted_iota(jnp.int32, (CHUNK, CHUNK), 0)
    col = lax.broadcasted_iota(jnp.int32, (CHUNK, CHUNK), 1)
    causal = row >= col

    for g in range(SSM_GROUPS):
        cst = cst_ref.at[g]
        xs = slice(g * GROUP_WIDTH, (g + 1) * GROUP_WIDTH)
        ns = slice(g * D_STATE, (g + 1) * D_STATE)
        conv = []
        for pad, lanes, cols in ((xpad, xs, _X_COLS), (bpad, ns, _B_COLS), (cpad, ns, _C_COLS)):
            shifted = [pad[tail - k:tail - k + CHUNK, lanes] for k in range(1, CONV_W)]
            conv.append(_conv_silu(pad[tail:, lanes], shifted, cst, cols))
        xc, bm, cm = conv
        z = z_refs[g // groups_per_part][:, (g % groups_per_part) * GROUP_WIDTH:
                                         (g % groups_per_part + 1) * GROUP_WIDTH]

        dt_c, dt_r = dtc_ref[g], dtr_ref[g]
        acum_c, acum_r = _decay_sums(causal, dt_c, dt_r, cst)
        cb = lax.dot_general(cm.astype(bf16), bm.astype(bf16), (((1,), (1,)), ((), ())),
                             preferred_element_type=f32)
        y = _intra_chunk(cb, causal, acum_c, acum_r, dt_r, xc)

        s_in = state[g]
        decay_in = _expand_heads(jnp.exp(acum_c))
        y += decay_in * jnp.dot(cm.astype(bf16), s_in.astype(bf16), preferred_element_type=f32)
        weight_out = _expand_heads(jnp.exp(acum_c[CHUNK - 1:CHUNK, :] - acum_c) * dt_c)
        state[g] = decay_in[CHUNK - 1:CHUNK, :] * s_in + jnp.dot(
            bm.T.astype(bf16), (xc * weight_out).astype(bf16), preferred_element_type=f32)

        y += cst[_ROW_D:_ROW_D + 1, _X_COLS] * xc
        y_ref[:, xs] = _gated_norm(y, z, cst[_ROW_GAIN:_ROW_GAIN + 1, _X_COLS]).astype(y_ref.dtype)

    @pl.when(c == pl.num_programs(1) - 1)
    def _():
        for g in range(SSM_GROUPS):
            st_ref[g * GROUP_WIDTH:(g + 1) * GROUP_WIDTH, :] = state[g].T


def _ssd_prompt(p, dt_c, dt_r, consts, batch):
    nc = SEQ // CHUNK
    rb = lambda b, c: b * nc + c
    part = lambda name, i=0: pl.BlockSpec((CHUNK, BC_WIDTH), lambda b, c: (rb(b, c), _COL[name] // BC_WIDTH + i))
    return pl.pallas_call(
        _ssd_prompt_kernel,
        grid=(batch, nc),
        in_specs=[
            *[part("x", i) for i in range(_SSM_PARTS)],
            *[part("z_s", i) for i in range(_SSM_PARTS)],
            part("B"), part("C"),
            pl.BlockSpec((SSM_GROUPS, CHUNK, HEADS_PER_GROUP), lambda b, c: (0, rb(b, c), 0)),
            pl.BlockSpec((SSM_GROUPS, HEADS_PER_GROUP, CHUNK), lambda b, c: (0, 0, rb(b, c))),
            pl.BlockSpec((SSM_GROUPS, _CONST_ROWS, _CONST_WIDTH), lambda b, c: (0, 0, 0)),
        ],
        out_specs=[
            pl.BlockSpec((CHUNK, SSM_WIDTH), lambda b, c: (rb(b, c), 0)),
            pl.BlockSpec((None, SSM_WIDTH, D_STATE), lambda b, c: (b, 0, 0)),
        ],
        out_shape=[
            jax.ShapeDtypeStruct((batch * SEQ, SSM_WIDTH), bf16),
            jax.ShapeDtypeStruct((batch, SSM_WIDTH, D_STATE), f32),
        ],
        scratch_shapes=[
            pltpu.VMEM((CHUNK + 8, SSM_WIDTH), f32),
            pltpu.VMEM((CHUNK + 8, BC_WIDTH), f32),
            pltpu.VMEM((CHUNK + 8, BC_WIDTH), f32),
            pltpu.VMEM((SSM_GROUPS, D_STATE, GROUP_WIDTH), f32),
        ],
        compiler_params=_params(("parallel", "arbitrary")),
        name="ssd_prompt",
    )(*[p] * (2 * _SSM_PARTS + 2), dt_c, dt_r, consts)


def _ssd_sample_kernel(x_ref, b_ref, c_ref, z_ref, dtc_ref, dtr_ref, cst_ref,
                       cx_ref, cbuf_ref, cc_ref, s0_ref, y_ref, s1_ref, xprev, bprev, cprev):
    row1 = lax.broadcasted_iota(jnp.int32, (CHUNK, 1), 0)
    conv = []
    for raw_ref, buf_ref, prev, cols in ((x_ref, cx_ref, xprev, _X_COLS), (b_ref, cbuf_ref, bprev, _B_COLS),
                                         (c_ref, cc_ref, cprev, _C_COLS)):
        prev[...] = jnp.zeros_like(prev)
        prev[:, :CONV_W - 1, :] = buf_ref[...]
        cached = prev[...].reshape(CHUNK, prev.shape[2])
        cur = raw_ref[...]
        shifted = []
        for k in range(1, CONV_W):
            from_cache = pltpu.roll(cached, (k - (CONV_W - 1)) % CHUNK, 0)
            shifted.append(jnp.where(_mod(row1, DEC_SEQ) >= k, pltpu.roll(cur, k, 0), from_cache))
        conv.append(_conv_silu(cur, shifted, cst_ref, cols))
    xc, bm, cm = conv

    dt_c, dt_r = dtc_ref[...], dtr_ref[...]
    row = lax.broadcasted_iota(jnp.int32, (CHUNK, CHUNK), 0)
    col = lax.broadcasted_iota(jnp.int32, (CHUNK, CHUNK), 1)
    same_seq = _div(row, DEC_SEQ) == _div(col, DEC_SEQ)
    causal = same_seq & (row >= col)
    acum_c, acum_r = _decay_sums(causal, dt_c, dt_r, cst_ref)
    cb = lax.dot_general(cm.astype(bf16), bm.astype(bf16), (((1,), (1,)), ((), ())),
                         preferred_element_type=f32)
    y = _intra_chunk(cb, causal, acum_c, acum_r, dt_r, xc)

    to_end, _ = _decay_sums(same_seq & (row < col), dt_c, dt_r, cst_ref)
    decay_in = _expand_heads(jnp.exp(acum_c))
    xw = (xc * _expand_heads(jnp.exp(to_end) * dt_c)).T.astype(bf16)
    keep = jnp.exp(acum_r)
    seq_of_row = _div(row1, DEC_SEQ)
    inter = jnp.zeros((CHUNK, GROUP_WIDTH), f32)
    for n in range(SEQS_PER_STEP):
        mine = seq_of_row == n
        s0 = s0_ref[n]
        inter += lax.dot_general(jnp.where(mine, cm, 0.0).astype(bf16), s0.astype(bf16),
                                 (((1,), (1,)), ((), ())), preferred_element_type=f32)
        update = jnp.dot(xw, jnp.where(mine, bm, 0.0).astype(bf16), preferred_element_type=f32)
        last_lane = (n + 1) * DEC_SEQ - 1
        for r in range(HEADS_PER_GROUP):
            rows = slice(r * SSM_HEAD_DIM, (r + 1) * SSM_HEAD_DIM)
            s1_ref[n, rows, :] = keep[r:r + 1, last_lane:last_lane + 1] * s0[rows] + update[rows]
    y += decay_in * inter + cst_ref[_ROW_D:_ROW_D + 1, _X_COLS] * xc
    y_ref[...] = _gated_norm(y, z_ref[...], cst_ref[_ROW_GAIN:_ROW_GAIN + 1, _X_COLS]).astype(y_ref.dtype)


def _ssd_sample(p, dt_c, dt_r, consts, conv_state, s0):
    nseq = s0.shape[0]
    nsb = nseq // SEQS_PER_STEP
    xcol, zcol = _COL["x"] // GROUP_WIDTH, _COL["z_s"] // GROUP_WIDTH
    bcol, ccol = _COL["B"] // D_STATE, _COL["C"] // D_STATE
    conv_block = lambda width, first: pl.BlockSpec(
        (SEQS_PER_STEP, CONV_W - 1, width), lambda s, g: (s, 0, first + g))
    state_spec = pl.BlockSpec((SEQS_PER_STEP, GROUP_WIDTH, D_STATE), lambda s, g: (s, g, 0))
    return pl.pallas_call(
        _ssd_sample_kernel,
        grid=(nsb, SSM_GROUPS),
        in_specs=[
            pl.BlockSpec((CHUNK, GROUP_WIDTH), lambda s, g: (s, xcol + g)),
            pl.BlockSpec((CHUNK, D_STATE), lambda s, g: (s, bcol + g)),
            pl.BlockSpec((CHUNK, D_STATE), lambda s, g: (s, ccol + g)),
            pl.BlockSpec((CHUNK, GROUP_WIDTH), lambda s, g: (s, zcol + g)),
            pl.BlockSpec((None, CHUNK, HEADS_PER_GROUP), lambda s, g: (g, s, 0)),
            pl.BlockSpec((None, HEADS_PER_GROUP, CHUNK), lambda s, g: (g, 0, s)),
            pl.BlockSpec((None, _CONST_ROWS, _CONST_WIDTH), lambda s, g: (g, 0, 0)),
            conv_block(GROUP_WIDTH, 0),
            conv_block(D_STATE, SSM_WIDTH // D_STATE),
            conv_block(D_STATE, (SSM_WIDTH + BC_WIDTH) // D_STATE),
            state_spec,
        ],
        out_specs=[pl.BlockSpec((CHUNK, GROUP_WIDTH), lambda s, g: (s, g)), state_spec],
        out_shape=[
            jax.ShapeDtypeStruct((nseq * DEC_SEQ, SSM_WIDTH), bf16),
            jax.ShapeDtypeStruct(s0.shape, f32),
        ],
        scratch_shapes=[
            pltpu.VMEM((SEQS_PER_STEP, 8, GROUP_WIDTH), f32),
            pltpu.VMEM((SEQS_PER_STEP, 8, D_STATE), f32),
            pltpu.VMEM((SEQS_PER_STEP, 8, D_STATE), f32),
        ],
        compiler_params=_params(("parallel", "parallel")),
        name="ssd_sample",
    )(p, p, p, p, dt_c, dt_r, consts, conv_state, conv_state, conv_state, s0)


def _merge_kernel(a_ref, s_ref, ga_ref, gs_ref, wa_ref, ws_ref, o_ref):
    ya = jnp.dot(a_ref[...].astype(bf16), wa_ref[...], preferred_element_type=f32)
    ys = jnp.dot(s_ref[...], ws_ref[...], preferred_element_type=f32)
    o_ref[...] = (jax.nn.sigmoid(ga_ref[...]) * ya + jax.nn.sigmoid(gs_ref[...]) * ys).astype(o_ref.dtype)


def _merge(attn, ssm, p, wa, ws, tm=1024, tn=512):
    m = attn.shape[0]
    return pl.pallas_call(
        _merge_kernel,
        grid=(m // tm, D_MODEL // tn),
        in_specs=[
            pl.BlockSpec((tm, ATTN_WIDTH), lambda i, j: (i, 0)),
            pl.BlockSpec((tm, SSM_WIDTH), lambda i, j: (i, 0)),
            pl.BlockSpec((tm, tn), lambda i, j: (i, _COL["g_a"] // tn + j)),
            pl.BlockSpec((tm, tn), lambda i, j: (i, _COL["g_s"] // tn + j)),
            pl.BlockSpec((ATTN_WIDTH, tn), lambda i, j: (0, j)),
            pl.BlockSpec((SSM_WIDTH, tn), lambda i, j: (0, j)),
        ],
        out_specs=pl.BlockSpec((tm, tn), lambda i, j: (i, j)),
        out_shape=jax.ShapeDtypeStruct((m, D_MODEL), bf16),
        compiler_params=_params(("parallel", "arbitrary")),
        name="merge",
    )(attn, ssm, p, p, wa, ws)


def _out_kernel(m_ref, wo_ref, x_ref, g_ref, o_ref):
    y = jnp.dot(m_ref[...], wo_ref[...], preferred_element_type=f32)
    ms = jnp.mean(y * y, axis=-1, keepdims=True)
    o_ref[...] = x_ref[...] + y * lax.rsqrt(ms + NORM_EPS) * g_ref[...]


def _out(merged, wo, x, g, tm=512):
    m = x.shape[0]
    return pl.pallas_call(
        _out_kernel,
        grid=(m // tm,),
        in_specs=[
            pl.BlockSpec((tm, D_MODEL), lambda i: (i, 0)),
            pl.BlockSpec((D_MODEL, D_MODEL), lambda i: (0, 0)),
            pl.BlockSpec((tm, D_MODEL), lambda i: (i, 0)),
            pl.BlockSpec((1, D_MODEL), lambda i: (0, 0)),
        ],
        out_specs=pl.BlockSpec((tm, D_MODEL), lambda i: (i, 0)),
        out_shape=jax.ShapeDtypeStruct((m, D_MODEL), f32),
        compiler_params=_params(("parallel",)),
        name="outproj",
    )(merged, wo, x, g)


def _group_layouts(dt, rows):
    d = dt[:, :SSM_HEADS].reshape(rows, SSM_GROUPS, HEADS_PER_GROUP)
    return d.transpose(1, 0, 2), d.transpose(1, 2, 0)


def _layer(xp, xs, cache_k, cache_v, state_ssm, state_conv, norm_pre, w_in, conv_w, conv_b, dt_bias,
           a_log, d_skip, ssm_norm, attn_sinks, w_attn_br, w_ssm_br, w_out, norm_post):
    batch, nseq = xp.shape[0], xs.shape[0]
    mp, ms = batch * SEQ, nseq * DEC_SEQ
    src = lambda name, width: w_in[:, _SRC[name]:_SRC[name] + width]
    w_main = _cast_weights(w_in, w_in[:, _SRC["g_a"]:])
    w_dt = jnp.pad(src("dt", SSM_HEADS), ((0, 0), (0, LANES - SSM_HEADS))).astype(bf16)
    dtb = jnp.pad(dt_bias, (0, LANES - SSM_HEADS)).reshape(1, LANES)
    g_pre = norm_pre.reshape(1, D_MODEL)

    ssd_consts = _ssd_constants(conv_w, conv_b, a_log, d_skip, ssm_norm)
    wa, ws, wo = w_attn_br.astype(bf16), w_ssm_br.astype(bf16), w_out.astype(bf16)
    g_post = norm_post.reshape(1, D_MODEL)

    xp2, xs2 = xp.reshape(mp, D_MODEL), xs.reshape(ms, D_MODEL)
    pp, dtp = _inproj(xp2, g_pre, w_main, w_dt, dtb)
    ps, dts = _inproj(xs2, g_pre, w_main, w_dt, dtb)

    attn_p = _attn_prompt(pp, attn_sinks, batch)
    ssm_p, st_p = _ssd_prompt(pp, *_group_layouts(dtp, mp), ssd_consts, batch)
    yp = _out(_merge(attn_p, ssm_p, pp, wa, ws), wo, xp2, g_post)
    pp3 = pp.reshape(batch, SEQ, P_WIDTH)
    k_p = pp3[:, SEQ - WINDOW:, _COL["k"]:_COL["k"] + KV_WIDTH]
    v_p = pp3[:, SEQ - WINDOW:, _COL["v"]:_COL["v"] + KV_WIDTH]
    conv_p = pp3[:, SEQ - (CONV_W - 1):, _COL["x"]:_COL["x"] + CONV_DIM]

    attn_s, k_s, v_s = _attn_sample(ps, attn_sinks, cache_k.reshape(nseq, WINDOW, KV_WIDTH),
                                    cache_v.reshape(nseq, WINDOW, KV_WIDTH))
    ssm_s, st_s = _ssd_sample(ps, *_group_layouts(dts, ms), ssd_consts, state_conv,
                              state_ssm.reshape(nseq, SSM_WIDTH, D_STATE))
    ys = _out(_merge(attn_s, ssm_s, ps, wa, ws), wo, xs2, g_post)
    conv_s = ps.reshape(nseq, DEC_SEQ, P_WIDTH)[:, DEC_SEQ - (CONV_W - 1):, _COL["x"]:_COL["x"] + CONV_DIM]

    kv = lambda t, n: t.reshape(1, n, WINDOW, N_KV_HEADS, HEAD_DIM)
    st = lambda t, n: t.reshape(1, n, SSM_HEADS, SSM_HEAD_DIM, D_STATE)
    return (yp.reshape(xp.shape), ys.reshape(xs.shape), kv(k_p, batch), kv(v_p, batch), st(st_p, batch),
            conv_p[None], kv(k_s, nseq), kv(v_s, nseq), st(st_s, nseq), conv_s[None])


def kernel(x_prompt, x_sample, cache_k, cache_v, state_ssm, state_conv, norm_pre, w_in, conv_w, conv_b,
           dt_bias, a_log, d_skip, ssm_norm, attn_sinks, w_attn_br, w_ssm_br, w_out, norm_post):
    assert w_in.shape[0] == 1, "single-layer trunk"
    return _layer(x_prompt, x_sample, cache_k[0], cache_v[0], state_ssm[0], state_conv[0], norm_pre[0],
                  w_in[0], conv_w[0], conv_b[0], dt_bias[0], a_log[0], d_skip[0], ssm_norm[0],
                  attn_sinks[0], w_attn_br[0], w_ssm_br[0], w_out[0], norm_post[0])
```

```python
import functools

import jax
import jax.numpy as jnp
from jax import lax
from jax.experimental import pallas as pl
from jax.experimental.pallas import tpu as pltpu

f32 = jnp.float32
bf16 = jnp.bfloat16

D_MODEL = 2048
SEQ = 4096
DEC_SEQ = 8
N_HEADS = 32
N_KV_HEADS = 8
HEAD_DIM = 64
Q_PER_KV = N_HEADS // N_KV_HEADS
ATTN_WIDTH = N_HEADS * HEAD_DIM
KV_WIDTH = N_KV_HEADS * HEAD_DIM
WINDOW = 128
SSM_WIDTH = 2 * D_MODEL
SSM_HEAD_DIM = 64
SSM_HEADS = SSM_WIDTH // SSM_HEAD_DIM
SSM_GROUPS = 8
HEADS_PER_GROUP = SSM_HEADS // SSM_GROUPS
GROUP_WIDTH = HEADS_PER_GROUP * SSM_HEAD_DIM
D_STATE = 128
CONV_W = 4
BC_WIDTH = SSM_GROUPS * D_STATE
CONV_DIM = SSM_WIDTH + 2 * BC_WIDTH
CHUNK = 128
NORM_EPS = 1e-6

_SRC = dict(q=0, k=2048, v=2560, z_a=3072, xbc=5120, z_s=11264, dt=15360, g_a=15424, g_s=17472)
_COL = dict(q=0, k=2048, v=2560, z_a=3072, x=5120, B=9216, C=10240, z_s=11264, g_a=15360, g_s=17408)
P_WIDTH = 19456
_GATE_COL = _COL["g_a"]
HALF_ATTN = ATTN_WIDTH // 2
LANES = 128
SEQS_PER_STEP = 16

_VMEM_LIMIT = 56 * 1024 * 1024


def _params(sem):
    return pltpu.CompilerParams(dimension_semantics=sem, vmem_limit_bytes=_VMEM_LIMIT)


def _silu(v):
    half = 0.5 * v
    return half + half * jnp.tanh(half)


def _div(v, n):
    assert n & (n - 1) == 0
    return v >> (n.bit_length() - 1)


def _mod(v, n):
    assert n & (n - 1) == 0
    return v & (n - 1)


_NORM_ROWS = 128


def _cast_weights_kernel(main_ref, gate_ref, o_ref, *, main_blocks):
    j = pl.program_id(0)

    @pl.when(j < main_blocks)
    def _():
        o_ref[...] = main_ref[...].astype(o_ref.dtype)

    @pl.when(j >= main_blocks)
    def _():
        o_ref[...] = gate_ref[...].astype(o_ref.dtype)


def _cast_weights(w_t, w_gate_t, tn=1024):
    main_blocks = _SRC["dt"] // tn
    return pl.pallas_call(
        functools.partial(_cast_weights_kernel, main_blocks=main_blocks),
        grid=(P_WIDTH // tn,),
        in_specs=[
            pl.BlockSpec((tn, D_MODEL), lambda j: (jnp.minimum(j, main_blocks - 1), 0)),
            pl.BlockSpec((tn, D_MODEL), lambda j: (jnp.maximum(j - main_blocks, 0), 0)),
        ],
        out_specs=pl.BlockSpec((tn, D_MODEL), lambda j: (j, 0)),
        out_shape=jax.ShapeDtypeStruct((P_WIDTH, D_MODEL), bf16),
        compiler_params=_params(("parallel",)),
        name="cast_weights",
    )(w_t, w_gate_t)


_NT = (((1,), (1,)), ((), ()))


def _inproj_kernel(x_ref, g_ref, w_ref, wdt_ref, dtb_ref, p_ref, dt_ref, h_ref):
    @pl.when(pl.program_id(1) == 0)
    def _():
        def norm_rows(i, carry):
            rows = pl.ds(pl.multiple_of(i * _NORM_ROWS, _NORM_ROWS), _NORM_ROWS)
            x = x_ref[rows, :]
            ms = jnp.mean(x * x, axis=-1, keepdims=True)
            h = (x * lax.rsqrt(ms + NORM_EPS) * g_ref[...]).astype(bf16)
            h_ref[rows, :] = h
            v = lax.dot_general(h, wdt_ref[...], _NT, preferred_element_type=f32) + dtb_ref[...]
            dt_ref[rows, :] = jnp.maximum(v, 0.0) + jnp.log1p(jnp.exp(-jnp.abs(v)))
            return carry

        lax.fori_loop(0, x_ref.shape[0] // _NORM_ROWS, norm_rows, 0)

    p_ref[...] = lax.dot_general(h_ref[...], w_ref[...], _NT, preferred_element_type=f32)


def _inproj(x, g, w, wdt, dtb, tm=1024, tn=1024):
    m = x.shape[0]
    return pl.pallas_call(
        _inproj_kernel,
        grid=(m // tm, P_WIDTH // tn),
        in_specs=[
            pl.BlockSpec((tm, D_MODEL), lambda i, j: (i, 0)),
            pl.BlockSpec((1, D_MODEL), lambda i, j: (0, 0)),
            pl.BlockSpec((tn, D_MODEL), lambda i, j: (j, 0)),
            pl.BlockSpec((LANES, D_MODEL), lambda i, j: (0, 0)),
            pl.BlockSpec((1, LANES), lambda i, j: (0, 0)),
        ],
        out_specs=[
            pl.BlockSpec((tm, tn), lambda i, j: (i, j)),
            pl.BlockSpec((tm, LANES), lambda i, j: (i, 0)),
        ],
        out_shape=[jax.ShapeDtypeStruct((m, P_WIDTH), f32), jax.ShapeDtypeStruct((m, LANES), f32)],
        scratch_shapes=[pltpu.VMEM((tm, D_MODEL), bf16)],
        compiler_params=_params(("parallel", "arbitrary")),
        name="inproj",
    )(x, g, w, wdt, dtb)


_HALF = LANES // HEAD_DIM
assert _HALF == 2


def _alibi_slopes():
    return jnp.exp2(-8.0 * jnp.arange(1, N_HEADS + 1, dtype=f32) / N_HEADS)


def _prompt_penalty():
    s = jnp.arange(WINDOW)[:, None]
    q = jnp.arange(WINDOW)[None, :]
    dist = jnp.where(s <= q, q - s, WINDOW + q - s).astype(f32)
    pen = (_alibi_slopes()[:, None, None] * dist[None]).reshape(N_KV_HEADS, Q_PER_KV, WINDOW, WINDOW)
    return pen.transpose(0, 2, 1, 3).reshape(N_KV_HEADS, WINDOW, Q_PER_KV * WINDOW)


def _sample_penalty():
    i = jnp.arange(DEC_SEQ)[:, None]
    c = jnp.arange(2 * WINDOW)[None, :]
    dist = WINDOW + i - c
    valid = (dist >= 0) & (dist < WINDOW) & (c < WINDOW + DEC_SEQ)
    pen = _alibi_slopes()[:, None, None] * dist.astype(f32)[None]
    return jnp.where(valid[None], pen, jnp.inf).reshape(N_HEADS * DEC_SEQ, 2 * WINDOW)


def _attn_prompt_kernel(sink_ref, pen_ref, q_ref, kc_ref, kp_ref, vc_ref, vp_ref, za0_ref, za1_ref, a_ref):
    cols4 = Q_PER_KV * WINDOW
    key = lax.broadcasted_iota(jnp.int32, (WINDOW, cols4), 0)
    qry = _mod(lax.broadcasted_iota(jnp.int32, (WINDOW, cols4), 1), WINDOW)
    from_cur = key <= qry
    low_half = lax.broadcasted_iota(jnp.int32, (WINDOW, LANES), 1) < HEAD_DIM
    prev_off = jnp.where(pl.program_id(1) > 0, 0.0, -jnp.inf)
    nt = (((1,), (1,)), ((), ()))
    kv_tiles = {}
    for j in range(N_KV_HEADS):
        if j % _HALF == 0:
            tile = slice((j // _HALF) * LANES, (j // _HALF + 1) * LANES)
            kv_tiles = dict(k_cur=kc_ref[:, tile].astype(bf16), k_prev=kp_ref[:, tile].astype(bf16),
                            v_cur=vc_ref[:, tile].T.astype(bf16), v_prev=vp_ref[:, tile].T.astype(bf16))
        mine = low_half == (j % _HALF == 0)
        pieces = []
        for h in range(j * Q_PER_KV, (j + 1) * Q_PER_KV):
            piece = q_ref[:, (h // _HALF) * LANES:(h // _HALF + 1) * LANES] * HEAD_DIM ** -0.5
            if h % _HALF != j % _HALF:
                piece = pltpu.roll(piece, HEAD_DIM, 1)
            pieces.append(jnp.where(mine, piece, 0.0))
        q = jnp.concatenate(pieces, axis=0).astype(bf16)
        s_cur = lax.dot_general(kv_tiles["k_cur"], q, nt, preferred_element_type=f32)
        s_prev = lax.dot_general(kv_tiles["k_prev"], q, nt, preferred_element_type=f32)
        t = jnp.where(from_cur, s_cur, s_prev + prev_off) - pen_ref[j]
        sinks = jnp.concatenate([jnp.full((1, WINDOW), sink_ref[j * Q_PER_KV + g], f32)
                                 for g in range(Q_PER_KV)], axis=1)
        m = jnp.maximum(jnp.max(t, axis=0, keepdims=True), sinks)
        p = jnp.exp(t - m)
        inv = 1.0 / (jnp.sum(p, axis=0, keepdims=True) + jnp.exp(sinks - m))
        o = jnp.dot(kv_tiles["v_cur"], jnp.where(from_cur, p, 0.0).astype(bf16), preferred_element_type=f32)
        o += jnp.dot(kv_tiles["v_prev"], jnp.where(from_cur, 0.0, p).astype(bf16), preferred_element_type=f32)
        o = o * inv
        dims = slice((j % _HALF) * HEAD_DIM, (j % _HALF + 1) * HEAD_DIM)
        for pair in range(Q_PER_KV // _HALF):
            g0 = pair * _HALF
            two_heads = jnp.concatenate([o[dims, g * WINDOW:(g + 1) * WINDOW] for g in (g0, g0 + 1)], axis=0)
            first = (j * Q_PER_KV // _HALF + pair) * LANES
            za_ref = (za0_ref, za1_ref)[first // HALF_ATTN]
            za = za_ref[:, first % HALF_ATTN:first % HALF_ATTN + LANES]
            a_ref[:, first:first + LANES] = (two_heads.T * _silu(za)).astype(a_ref.dtype)


def _attn_prompt(p, sinks, batch):
    nb = SEQ // WINDOW
    kcol, vcol = _COL["k"] // KV_WIDTH, _COL["v"] // KV_WIDTH
    zcol = _COL["z_a"] // HALF_ATTN
    cur = lambda b, i: b * nb + i
    prev = lambda b, i: b * nb + jnp.maximum(i - 1, 0)
    half_block = lambda col: pl.BlockSpec((WINDOW, HALF_ATTN), lambda b, i: (cur(b, i), col))
    return pl.pallas_call(
        _attn_prompt_kernel,
        grid=(batch, nb),
        in_specs=[
            pl.BlockSpec(memory_space=pltpu.SMEM),
            pl.BlockSpec((N_KV_HEADS, WINDOW, Q_PER_KV * WINDOW), lambda b, i: (0, 0, 0)),
            pl.BlockSpec((WINDOW, ATTN_WIDTH), lambda b, i: (cur(b, i), _COL["q"] // ATTN_WIDTH)),
            pl.BlockSpec((WINDOW, KV_WIDTH), lambda b, i: (cur(b, i), kcol)),
            pl.BlockSpec((WINDOW, KV_WIDTH), lambda b, i: (prev(b, i), kcol)),
            pl.BlockSpec((WINDOW, KV_WIDTH), lambda b, i: (cur(b, i), vcol)),
            pl.BlockSpec((WINDOW, KV_WIDTH), lambda b, i: (prev(b, i), vcol)),
            half_block(zcol), half_block(zcol + 1),
        ],
        out_specs=pl.BlockSpec((WINDOW, ATTN_WIDTH), lambda b, i: (cur(b, i), 0)),
        out_shape=jax.ShapeDtypeStruct((batch * SEQ, ATTN_WIDTH), bf16),
        compiler_params=_params(("parallel", "parallel")),
        name="attn_prompt",
    )(sinks, _prompt_penalty(), p, p, p, p, p, p, p)


_ATTN_SEQS = 8


def _attn_sample_kernel(pen_ref, sink_ref, q_ref, kn_ref, vn_ref, za0_ref, za1_ref, ck_ref, cv_ref,
                        a_ref, ko_ref, vo_ref, ak_ref, av_ref):
    keys = 2 * WINDOW
    pad = jnp.zeros((keys - WINDOW - DEC_SEQ, KV_WIDTH), f32)
    ak_ref[WINDOW + DEC_SEQ:, :] = pad
    av_ref[WINDOW + DEC_SEQ:, :] = pad
    low_half = lax.broadcasted_iota(jnp.int32, (DEC_SEQ, LANES), 1) < HEAD_DIM
    nt = (((1,), (1,)), ((), ()))

    def to_half(piece, src, dst):
        return piece if src == dst else pltpu.roll(piece, HEAD_DIM, 1)

    def one_sequence(n, carry):
        new = pl.ds(pl.multiple_of(n * DEC_SEQ, DEC_SEQ), DEC_SEQ)
        for cache_ref, new_ref, all_ref, out_ref in ((ck_ref, kn_ref, ak_ref, ko_ref),
                                                     (cv_ref, vn_ref, av_ref, vo_ref)):
            all_ref[:WINDOW, :] = cache_ref[n]
            all_ref[WINDOW:WINDOW + DEC_SEQ, :] = new_ref[new, :]
            out_ref[n] = all_ref[DEC_SEQ:WINDOW + DEC_SEQ, :]
        k16, v16 = ak_ref[...].astype(bf16), av_ref[...].astype(bf16)
        scores = []
        for j in range(N_KV_HEADS):
            pieces = []
            for h in range(j * Q_PER_KV, (j + 1) * Q_PER_KV):
                piece = q_ref[new, (h // _HALF) * LANES:(h // _HALF + 1) * LANES] * HEAD_DIM ** -0.5
                piece = to_half(piece, h % _HALF, j % _HALF)
                pieces.append(jnp.where(low_half == (j % _HALF == 0), piece, 0.0))
            qj = jnp.concatenate(pieces, axis=0).astype(bf16)
            kj = k16[:, (j // _HALF) * LANES:(j // _HALF + 1) * LANES]
            scores.append(lax.dot_general(qj, kj, nt, preferred_element_type=f32))
        t = jnp.concatenate(scores, axis=0) - pen_ref[...]
        sinks = sink_ref[...]
        m = jnp.maximum(jnp.max(t, axis=-1, keepdims=True), sinks)
        p = jnp.exp(t - m)
        inv = 1.0 / (jnp.sum(p, axis=-1, keepdims=True) + jnp.exp(sinks - m))
        p16 = p.astype(bf16)
        rows_per_kv = Q_PER_KV * DEC_SEQ
        outs = []
        for j in range(N_KV_HEADS):
            rows = slice(j * rows_per_kv, (j + 1) * rows_per_kv)
            vj = v16[:, (j // _HALF) * LANES:(j // _HALF + 1) * LANES]
            oj = jnp.dot(p16[rows], vj, preferred_element_type=f32) * inv[rows]
            for g in range(Q_PER_KV):
                h = j * Q_PER_KV + g
                outs.append(to_half(oj[g * DEC_SEQ:(g + 1) * DEC_SEQ], j % _HALF, h % _HALF))
        o = jnp.concatenate([jnp.where(low_half, outs[h], outs[h + 1]) for h in range(0, N_HEADS, _HALF)],
                            axis=1)
        za = jnp.concatenate([za0_ref[new, :], za1_ref[new, :]], axis=1)
        a_ref[new, :] = o * _silu(za)
        return carry

    lax.fori_loop(0, _ATTN_SEQS, one_sequence, 0)


def _attn_sample(p, sinks, cache_k, cache_v):
    nseq = cache_k.shape[0]
    rows = _ATTN_SEQS * DEC_SEQ
    cache_spec = pl.BlockSpec((_ATTN_SEQS, WINDOW, KV_WIDTH), lambda s: (s, 0, 0))
    sink_col = jnp.repeat(sinks.astype(f32), DEC_SEQ).reshape(N_HEADS * DEC_SEQ, 1)
    zcol = _COL["z_a"] // HALF_ATTN
    half_block = lambda col: pl.BlockSpec((rows, HALF_ATTN), lambda s: (s, col))
    return pl.pallas_call(
        _attn_sample_kernel,
        grid=(nseq // _ATTN_SEQS,),
        in_specs=[
            pl.BlockSpec((N_HEADS * DEC_SEQ, 2 * WINDOW), lambda s: (0, 0)),
            pl.BlockSpec((N_HEADS * DEC_SEQ, 1), lambda s: (0, 0)),
            pl.BlockSpec((rows, ATTN_WIDTH), lambda s: (s, _COL["q"] // ATTN_WIDTH)),
            pl.BlockSpec((rows, KV_WIDTH), lambda s: (s, _COL["k"] // KV_WIDTH)),
            pl.BlockSpec((rows, KV_WIDTH), lambda s: (s, _COL["v"] // KV_WIDTH)),
            half_block(zcol), half_block(zcol + 1),
            cache_spec, cache_spec,
        ],
        out_specs=[pl.BlockSpec((rows, ATTN_WIDTH), lambda s: (s, 0)), cache_spec, cache_spec],
        out_shape=[
            jax.ShapeDtypeStruct((nseq * DEC_SEQ, ATTN_WIDTH), f32),
            jax.ShapeDtypeStruct(cache_k.shape, f32),
            jax.ShapeDtypeStruct(cache_v.shape, f32),
        ],
        scratch_shapes=[pltpu.VMEM((2 * WINDOW, KV_WIDTH), f32), pltpu.VMEM((2 * WINDOW, KV_WIDTH), f32)],
        compiler_params=_params(("parallel",)),
        name="attn_sample",
    )(_sample_penalty(), sink_col, p, p, p, p, p, cache_k, cache_v)


def _expand_heads(v):
    low = lax.broadcasted_iota(jnp.int32, (v.shape[0], LANES), 1) < SSM_HEAD_DIM
    tiles = [jnp.where(low, v[:, r:r + 1], v[:, r + 1:r + 2]) for r in range(0, HEADS_PER_GROUP, 2)]
    return jnp.concatenate(tiles, axis=1)


def _split3(v):
    hi = v.astype(bf16)
    rest = v - hi.astype(f32)
    mid = rest.astype(bf16)
    return hi, mid, (rest - mid.astype(f32)).astype(bf16)


def _masked_sums(mask, v_c, v_r):
    m16 = jnp.where(mask, 1.0, 0.0).astype(bf16)
    out_c = sum(jnp.dot(m16, piece, preferred_element_type=f32) for piece in _split3(v_c))
    out_r = sum(lax.dot_general(piece, m16, (((1,), (1,)), ((), ())), preferred_element_type=f32)
                for piece in _split3(v_r))
    return out_c, out_r


_ROW_BIAS, _ROW_D, _ROW_GAIN, _ROW_ALOG, _ROW_ALOG_COL = CONV_W, CONV_W + 1, CONV_W + 2, CONV_W + 3, CONV_W + 4
_CONST_ROWS = _ROW_ALOG_COL + HEADS_PER_GROUP
_CONST_WIDTH = GROUP_WIDTH + 2 * D_STATE
_X_COLS, _B_COLS, _C_COLS = (slice(0, GROUP_WIDTH), slice(GROUP_WIDTH, GROUP_WIDTH + D_STATE),
                             slice(GROUP_WIDTH + D_STATE, _CONST_WIDTH))


def _ssd_constants(conv_w, conv_b, a_log, d_skip, ssm_norm):
    grouped = lambda t, width: t.reshape(t.shape[0], SSM_GROUPS, width).transpose(1, 0, 2)
    padded = lambda t: jnp.pad(t, ((0, 0), (0, 0), (0, _CONST_WIDTH - t.shape[2])))
    taps = jnp.concatenate([conv_w, conv_b[None]], axis=0)
    top = jnp.concatenate([grouped(taps[:, :SSM_WIDTH], GROUP_WIDTH),
                           grouped(taps[:, SSM_WIDTH:SSM_WIDTH + BC_WIDTH], D_STATE),
                           grouped(taps[:, SSM_WIDTH + BC_WIDTH:], D_STATE)], axis=2)
    al = a_log.reshape(SSM_GROUPS, HEADS_PER_GROUP)
    return jnp.concatenate([top, padded(grouped(jnp.repeat(d_skip, SSM_HEAD_DIM)[None], GROUP_WIDTH)),
                            padded(grouped(ssm_norm[None], GROUP_WIDTH)),
                            padded(al[:, None, :]), padded(al[:, :, None])], axis=1)


def _intra_chunk(cb, causal, acum_c, acum_r, dt_r, xc):
    lane = lax.broadcasted_iota(jnp.int32, (CHUNK, LANES), 1)
    pieces = []
    for pair in range(HEADS_PER_GROUP // 2):
        x2 = xc[:, pair * LANES:(pair + 1) * LANES]
        acc = jnp.zeros((CHUNK, LANES), f32)
        for half in range(2):
            r = 2 * pair + half
            keep = (lane < SSM_HEAD_DIM) if half == 0 else (lane >= SSM_HEAD_DIM)
            decay = jnp.exp(jnp.where(causal, acum_c[:, r:r + 1] - acum_r[r:r + 1, :], -jnp.inf))
            acc += jnp.dot((cb * decay * dt_r[r:r + 1, :]).astype(bf16), jnp.where(keep, x2, 0.0).astype(bf16),
                           preferred_element_type=f32)
        pieces.append(acc)
    return jnp.concatenate(pieces, axis=1)


def _gated_norm(y, z, gain):
    u = y * _silu(z)
    ms = jnp.mean(u * u, axis=-1, keepdims=True)
    return u * lax.rsqrt(ms + NORM_EPS) * gain


def _conv_silu(cur, shifted, cst_ref, cols):
    y = cst_ref[_ROW_BIAS:_ROW_BIAS + 1, cols] + cst_ref[CONV_W - 1:CONV_W, cols] * cur
    for k in range(1, CONV_W):
        y = y + cst_ref[CONV_W - 1 - k:CONV_W - k, cols] * shifted[k - 1]
    return _silu(y)


def _decay_sums(mask, dt_c, dt_r, cst_ref):
    a_c = -jnp.exp(cst_ref[_ROW_ALOG:_ROW_ALOG + 1, 0:HEADS_PER_GROUP])
    a_r = -jnp.exp(cst_ref[_ROW_ALOG_COL:_ROW_ALOG_COL + HEADS_PER_GROUP, 0:1])
    return _masked_sums(mask, dt_c * a_c, dt_r * a_r)


_SSM_PARTS = SSM_WIDTH // BC_WIDTH


def _ssd_prompt_kernel(*refs):
    x_refs, refs = refs[:_SSM_PARTS], refs[_SSM_PARTS:]
    z_refs, refs = refs[:_SSM_PARTS], refs[_SSM_PARTS:]
    b_ref, c_ref, dtc_ref, dtr_ref, cst_ref, y_ref, st_ref, xpad, bpad, cpad, state = refs
    groups_per_part = SSM_GROUPS // _SSM_PARTS
    c = pl.program_id(1)
    tail = 8

    @pl.when(c == 0)
    def _():
        state[...] = jnp.zeros_like(state)
        for pad in (xpad, bpad, cpad):
            pad[:tail, :] = jnp.zeros((tail, pad.shape[1]), f32)

    @pl.when(c > 0)
    def _():
        for pad in (xpad, bpad, cpad):
            pad[:tail, :] = pad[CHUNK:CHUNK + tail, :]

    for i, part_ref in enumerate(x_refs):
        xpad[tail:, i * BC_WIDTH:(i + 1) * BC_WIDTH] = part_ref[...]
    bpad[tail:, :] = b_ref[...]
    cpad[tail:, :] = c_ref[...]
    row = lax.broadcasted_iota(jnp.int32, (CHUNK, CHUNK), 0)
    col = lax.broadcasted_iota(jnp.int32, (CHUNK, CHUNK), 1)
    causal = row >= col

    for g in range(SSM_GROUPS):
        cst = cst_ref.at[g]
        xs = slice(g * GROUP_WIDTH, (g + 1) * GROUP_WIDTH)
        ns = slice(g * D_STATE, (g + 1) * D_STATE)
        conv = []
        for pad, lanes, cols in ((xpad, xs, _X_COLS), (bpad, ns, _B_COLS), (cpad, ns, _C_COLS)):
            shifted = [pad[tail - k:tail - k + CHUNK, lanes] for k in range(1, CONV_W)]
            conv.append(_conv_silu(pad[tail:, lanes], shifted, cst, cols))
        xc, bm, cm = conv
        z = z_refs[g // groups_per_part][:, (g % groups_per_part) * GROUP_WIDTH:
                                         (g % groups_per_part + 1) * GROUP_WIDTH]

        dt_c, dt_r = dtc_ref[g], dtr_ref[g]
        acum_c, acum_r = _decay_sums(causal, dt_c, dt_r, cst)
        cb = lax.dot_general(cm.astype(bf16), bm.astype(bf16), (((1,), (1,)), ((), ())),
                             preferred_element_type=f32)
        y = _intra_chunk(cb, causal, acum_c, acum_r, dt_r, xc)

        s_in = state[g]
        decay_in = _expand_heads(jnp.exp(acum_c))
        y += decay_in * jnp.dot(cm.astype(bf16), s_in.astype(bf16), preferred_element_type=f32)
        weight_out = _expand_heads(jnp.exp(acum_c[CHUNK - 1:CHUNK, :] - acum_c) * dt_c)
        state[g] = decay_in[CHUNK - 1:CHUNK, :] * s_in + jnp.dot(
            bm.T.astype(bf16), (xc * weight_out).astype(bf16), preferred_element_type=f32)

        y += cst[_ROW_D:_ROW_D + 1, _X_COLS] * xc
        y_ref[:, xs] = _gated_norm(y, z, cst[_ROW_GAIN:_ROW_GAIN + 1, _X_COLS]).astype(y_ref.dtype)

    @pl.when(c == pl.num_programs(1) - 1)
    def _():
        for g in range(SSM_GROUPS):
            st_ref[g * GROUP_WIDTH:(g + 1) * GROUP_WIDTH, :] = state[g].T


def _ssd_prompt(p, dt_c, dt_r, consts, batch):
    nc = SEQ // CHUNK
    rb = lambda b, c: b * nc + c
    part = lambda name, i=0: pl.BlockSpec((CHUNK, BC_WIDTH), lambda b, c: (rb(b, c), _COL[name] // BC_WIDTH + i))
    return pl.pallas_call(
        _ssd_prompt_kernel,
        grid=(batch, nc),
        in_specs=[
            *[part("x", i) for i in range(_SSM_PARTS)],
            *[part("z_s", i) for i in range(_SSM_PARTS)],
            part("B"), part("C"),
            pl.BlockSpec((SSM_GROUPS, CHUNK, HEADS_PER_GROUP), lambda b, c: (0, rb(b, c), 0)),
            pl.BlockSpec((SSM_GROUPS, HEADS_PER_GROUP, CHUNK), lambda b, c: (0, 0, rb(b, c))),
            pl.BlockSpec((SSM_GROUPS, _CONST_ROWS, _CONST_WIDTH), lambda b, c: (0, 0, 0)),
        ],
        out_specs=[
            pl.BlockSpec((CHUNK, SSM_WIDTH), lambda b, c: (rb(b, c), 0)),
            pl.BlockSpec((None, SSM_WIDTH, D_STATE), lambda b, c: (b, 0, 0)),
        ],
        out_shape=[
            jax.ShapeDtypeStruct((batch * SEQ, SSM_WIDTH), bf16),
            jax.ShapeDtypeStruct((batch, SSM_WIDTH, D_STATE), f32),
        ],
        scratch_shapes=[
            pltpu.VMEM((CHUNK + 8, SSM_WIDTH), f32),
            pltpu.VMEM((CHUNK + 8, BC_WIDTH), f32),
            pltpu.VMEM((CHUNK + 8, BC_WIDTH), f32),
            pltpu.VMEM((SSM_GROUPS, D_STATE, GROUP_WIDTH), f32),
        ],
        compiler_params=_params(("parallel", "arbitrary")),
        name="ssd_prompt",
    )(*[p] * (2 * _SSM_PARTS + 2), dt_c, dt_r, consts)


def _ssd_sample_kernel(x_ref, b_ref, c_ref, z_ref, dtc_ref, dtr_ref, cst_ref,
                       cx_ref, cbuf_ref, cc_ref, s0_ref, y_ref, s1_ref, xprev, bprev, cprev):
    row1 = lax.broadcasted_iota(jnp.int32, (CHUNK, 1), 0)
    conv = []
    for raw_ref, buf_ref, prev, cols in ((x_ref, cx_ref, xprev, _X_COLS), (b_ref, cbuf_ref, bprev, _B_COLS),
                                         (c_ref, cc_ref, cprev, _C_COLS)):
        prev[...] = jnp.zeros_like(prev)
        prev[:, :CONV_W - 1, :] = buf_ref[...]
        cached = prev[...].reshape(CHUNK, prev.shape[2])
        cur = raw_ref[...]
        shifted = []
        for k in range(1, CONV_W):
            from_cache = pltpu.roll(cached, (k - (CONV_W - 1)) % CHUNK, 0)
            shifted.append(jnp.where(_mod(row1, DEC_SEQ) >= k, pltpu.roll(cur, k, 0), from_cache))
        conv.append(_conv_silu(cur, shifted, cst_ref, cols))
    xc, bm, cm = conv

    dt_c, dt_r = dtc_ref[...], dtr_ref[...]
    row = lax.broadcasted_iota(jnp.int32, (CHUNK, CHUNK), 0)
    col = lax.broadcasted_iota(jnp.int32, (CHUNK, CHUNK), 1)
    same_seq = _div(row, DEC_SEQ) == _div(col, DEC_SEQ)
    causal = same_seq & (row >= col)
    acum_c, acum_r = _decay_sums(causal, dt_c, dt_r, cst_ref)
    cb = lax.dot_general(cm.astype(bf16), bm.astype(bf16), (((1,), (1,)), ((), ())),
                         preferred_element_type=f32)
    y = _intra_chunk(cb, causal, acum_c, acum_r, dt_r, xc)

    to_end, _ = _decay_sums(same_seq & (row < col), dt_c, dt_r, cst_ref)
    decay_in = _expand_heads(jnp.exp(acum_c))
    xw = (xc * _expand_heads(jnp.exp(to_end) * dt_c)).T.astype(bf16)
    keep = jnp.exp(acum_r)
    seq_of_row = _div(row1, DEC_SEQ)
    inter = jnp.zeros((CHUNK, GROUP_WIDTH), f32)
    for n in range(SEQS_PER_STEP):
        mine = seq_of_row == n
        s0 = s0_ref[n]
        inter += lax.dot_general(jnp.where(mine, cm, 0.0).astype(bf16), s0.astype(bf16),
                                 (((1,), (1,)), ((), ())), preferred_element_type=f32)
        update = jnp.dot(xw, jnp.where(mine, bm, 0.0).astype(bf16), preferred_element_type=f32)
        last_lane = (n + 1) * DEC_SEQ - 1
        for r in range(HEADS_PER_GROUP):
            rows = slice(r * SSM_HEAD_DIM, (r + 1) * SSM_HEAD_DIM)
            s1_ref[n, rows, :] = keep[r:r + 1, last_lane:last_lane + 1] * s0[rows] + update[rows]
    y += decay_in * inter + cst_ref[_ROW_D:_ROW_D + 1, _X_COLS] * xc
    y_ref[...] = _gated_norm(y, z_ref[...], cst_ref[_ROW_GAIN:_ROW_GAIN + 1, _X_COLS]).astype(y_ref.dtype)


def _ssd_sample(p, dt_c, dt_r, consts, conv_state, s0):
    nseq = s0.shape[0]
    nsb = nseq // SEQS_PER_STEP
    xcol, zcol = _COL["x"] // GROUP_WIDTH, _COL["z_s"] // GROUP_WIDTH
    bcol, ccol = _COL["B"] // D_STATE, _COL["C"] // D_STATE
    conv_block = lambda width, first: pl.BlockSpec(
        (SEQS_PER_STEP, CONV_W - 1, width), lambda s, g: (s, 0, first + g))
    state_spec = pl.BlockSpec((SEQS_PER_STEP, GROUP_WIDTH, D_STATE), lambda s, g: (s, g, 0))
    return pl.pallas_call(
        _ssd_sample_kernel,
        grid=(nsb, SSM_GROUPS),
        in_specs=[
            pl.BlockSpec((CHUNK, GROUP_WIDTH), lambda s, g: (s, xcol + g)),
            pl.BlockSpec((CHUNK, D_STATE), lambda s, g: (s, bcol + g)),
            pl.BlockSpec((CHUNK, D_STATE), lambda s, g: (s, ccol + g)),
            pl.BlockSpec((CHUNK, GROUP_WIDTH), lambda s, g: (s, zcol + g)),
            pl.BlockSpec((None, CHUNK, HEADS_PER_GROUP), lambda s, g: (g, s, 0)),
            pl.BlockSpec((None, HEADS_PER_GROUP, CHUNK), lambda s, g: (g, 0, s)),
            pl.BlockSpec((None, _CONST_ROWS, _CONST_WIDTH), lambda s, g: (g, 0, 0)),
            conv_block(GROUP_WIDTH, 0),
            conv_block(D_STATE, SSM_WIDTH // D_STATE),
            conv_block(D_STATE, (SSM_WIDTH + BC_WIDTH) // D_STATE),
            state_spec,
        ],
        out_specs=[pl.BlockSpec((CHUNK, GROUP_WIDTH), lambda s, g: (s, g)), state_spec],
        out_shape=[
            jax.ShapeDtypeStruct((nseq * DEC_SEQ, SSM_WIDTH), bf16),
            jax.ShapeDtypeStruct(s0.shape, f32),
        ],
        scratch_shapes=[
            pltpu.VMEM((SEQS_PER_STEP, 8, GROUP_WIDTH), f32),
            pltpu.VMEM((SEQS_PER_STEP, 8, D_STATE), f32),
            pltpu.VMEM((SEQS_PER_STEP, 8, D_STATE), f32),
        ],
        compiler_params=_params(("parallel", "parallel")),
        name="ssd_sample",
    )(p, p, p, p, dt_c, dt_r, consts, conv_state, conv_state, conv_state, s0)


def _merge_kernel(a_ref, s_ref, ga_ref, gs_ref, wa_ref, ws_ref, o_ref):
    ya = jnp.dot(a_ref[...].astype(bf16), wa_ref[...], preferred_element_type=f32)
    ys = jnp.dot(s_ref[...], ws_ref[...], preferred_element_type=f32)
    o_ref[...] = (jax.nn.sigmoid(ga_ref[...]) * ya + jax.nn.sigmoid(gs_ref[...]) * ys).astype(o_ref.dtype)


def _merge(attn, ssm, p, wa, ws, tm=1024, tn=512):
    m = attn.shape[0]
    return pl.pallas_call(
        _merge_kernel,
        grid=(m // tm, D_MODEL // tn),
        in_specs=[
            pl.BlockSpec((tm, ATTN_WIDTH), lambda i, j: (i, 0)),
            pl.BlockSpec((tm, SSM_WIDTH), lambda i, j: (i, 0)),
            pl.BlockSpec((tm, tn), lambda i, j: (i, _COL["g_a"] // tn + j)),
            pl.BlockSpec((tm, tn), lambda i, j: (i, _COL["g_s"] // tn + j)),
            pl.BlockSpec((ATTN_WIDTH, tn), lambda i, j: (0, j)),
            pl.BlockSpec((SSM_WIDTH, tn), lambda i, j: (0, j)),
        ],
        out_specs=pl.BlockSpec((tm, tn), lambda i, j: (i, j)),
        out_shape=jax.ShapeDtypeStruct((m, D_MODEL), bf16),
        compiler_params=_params(("parallel", "arbitrary")),
        name="merge",
    )(attn, ssm, p, p, wa, ws)


def _out_kernel(m_ref, wo_ref, x_ref, g_ref, o_ref):
    y = jnp.dot(m_ref[...], wo_ref[...], preferred_element_type=f32)
    ms = jnp.mean(y * y, axis=-1, keepdims=True)
    o_ref[...] = x_ref[...] + y * lax.rsqrt(ms + NORM_EPS) * g_ref[...]


def _out(merged, wo, x, g, tm=512):
    m = x.shape[0]
    return pl.pallas_call(
        _out_kernel,
        grid=(m // tm,),
        in_specs=[
            pl.BlockSpec((tm, D_MODEL), lambda i: (i, 0)),
            pl.BlockSpec((D_MODEL, D_MODEL), lambda i: (0, 0)),
            pl.BlockSpec((tm, D_MODEL), lambda i: (i, 0)),
            pl.BlockSpec((1, D_MODEL), lambda i: (0, 0)),
        ],
        out_specs=pl.BlockSpec((tm, D_MODEL), lambda i: (i, 0)),
        out_shape=jax.ShapeDtypeStruct((m, D_MODEL), f32),
        compiler_params=_params(("parallel",)),
        name="outproj",
    )(merged, wo, x, g)


def _group_layouts(dt, rows):
    d = dt[:, :SSM_HEADS].reshape(rows, SSM_GROUPS, HEADS_PER_GROUP)
    return d.transpose(1, 0, 2), d.transpose(1, 2, 0)


def _layer(xp, xs, cache_k, cache_v, state_ssm, state_conv, norm_pre, w_in, conv_w, conv_b, dt_bias,
           a_log, d_skip, ssm_norm, attn_sinks, w_attn_br, w_ssm_br, w_out, norm_post):
    batch, nseq = xp.shape[0], xs.shape[0]
    mp, ms = batch * SEQ, nseq * DEC_SEQ
    w_t = w_in.T
    w_main = _cast_weights(w_t, w_t[_SRC["g_a"]:])
    w_dt = jnp.pad(w_t[_SRC["dt"]:_SRC["dt"] + SSM_HEADS], ((0, LANES - SSM_HEADS), (0, 0))).astype(bf16)
    dtb = jnp.pad(dt_bias, (0, LANES - SSM_HEADS)).reshape(1, LANES)
    g_pre = norm_pre.reshape(1, D_MODEL)

    ssd_consts = _ssd_constants(conv_w, conv_b, a_log, d_skip, ssm_norm)
    wa, ws, wo = w_attn_br.astype(bf16), w_ssm_br.astype(bf16), w_out.astype(bf16)
    g_post = norm_post.reshape(1, D_MODEL)

    xp2, xs2 = xp.reshape(mp, D_MODEL), xs.reshape(ms, D_MODEL)
    pp, dtp = _inproj(xp2, g_pre, w_main, w_dt, dtb)
    ps, dts = _inproj(xs2, g_pre, w_main, w_dt, dtb)

    attn_p = _attn_prompt(pp, attn_sinks, batch)
    ssm_p, st_p = _ssd_prompt(pp, *_group_layouts(dtp, mp), ssd_consts, batch)
    yp = _out(_merge(attn_p, ssm_p, pp, wa, ws), wo, xp2, g_post)
    pp3 = pp.reshape(batch, SEQ, P_WIDTH)
    k_p = pp3[:, SEQ - WINDOW:, _COL["k"]:_COL["k"] + KV_WIDTH]
    v_p = pp3[:, SEQ - WINDOW:, _COL["v"]:_COL["v"] + KV_WIDTH]
    conv_p = pp3[:, SEQ - (CONV_W - 1):, _COL["x"]:_COL["x"] + CONV_DIM]

    attn_s, k_s, v_s = _attn_sample(ps, attn_sinks, cache_k.reshape(nseq, WINDOW, KV_WIDTH),
                                    cache_v.reshape(nseq, WINDOW, KV_WIDTH))
    ssm_s, st_s = _ssd_sample(ps, *_group_layouts(dts, ms), ssd_consts, state_conv,
                              state_ssm.reshape(nseq, SSM_WIDTH, D_STATE))
    ys = _out(_merge(attn_s, ssm_s, ps, wa, ws), wo, xs2, g_post)
    conv_s = ps.reshape(nseq, DEC_SEQ, P_WIDTH)[:, DEC_SEQ - (CONV_W - 1):, _COL["x"]:_COL["x"] + CONV_DIM]

    kv = lambda t, n: t.reshape(1, n, WINDOW, N_KV_HEADS, HEAD_DIM)
    st = lambda t, n: t.reshape(1, n, SSM_HEADS, SSM_HEAD_DIM, D_STATE)
    return (yp.reshape(xp.shape), ys.reshape(xs.shape), kv(k_p, batch), kv(v_p, batch), st(st_p, batch),
            conv_p[None], kv(k_s, nseq), kv(v_s, nseq), st(st_s, nseq), conv_s[None])


def kernel(x_prompt, x_sample, cache_k, cache_v, state_ssm, state_conv, norm_pre, w_in, conv_w, conv_b,
           dt_bias, a_log, d_skip, ssm_norm, attn_sinks, w_attn_br, w_ssm_br, w_out, norm_post):
    assert w_in.shape[0] == 1, "single-layer trunk"
    return _layer(x_prompt, x_sample, cache_k[0], cache_v[0], state_ssm[0], state_conv[0], norm_pre[0],
                  w_in[0], conv_w[0], conv_b[0], dt_bias[0], a_log[0], d_skip[0], ssm_norm[0],
                  attn_sinks[0], w_attn_br[0], w_ssm_br[0], w_out[0], norm_post[0])
```

```python
import functools

import jax
import jax.numpy as jnp
from jax import lax
from jax.experimental import pallas as pl
from jax.experimental.pallas import tpu as pltpu

f32 = jnp.float32
bf16 = jnp.bfloat16

D_MODEL = 2048
SEQ = 4096
DEC_SEQ = 8
N_HEADS = 32
N_KV_HEADS = 8
HEAD_DIM = 64
Q_PER_KV = N_HEADS // N_KV_HEADS
ATTN_WIDTH = N_HEADS * HEAD_DIM
KV_WIDTH = N_KV_HEADS * HEAD_DIM
WINDOW = 128
SSM_WIDTH = 2 * D_MODEL
SSM_HEAD_DIM = 64
SSM_HEADS = SSM_WIDTH // SSM_HEAD_DIM
SSM_GROUPS = 8
HEADS_PER_GROUP = SSM_HEADS // SSM_GROUPS
GROUP_WIDTH = HEADS_PER_GROUP * SSM_HEAD_DIM
D_STATE = 128
CONV_W = 4
BC_WIDTH = SSM_GROUPS * D_STATE
CONV_DIM = SSM_WIDTH + 2 * BC_WIDTH
CHUNK = 128
NORM_EPS = 1e-6

_SRC = dict(q=0, k=2048, v=2560, z_a=3072, xbc=5120, z_s=11264, dt=15360, g_a=15424, g_s=17472)
_COL = dict(q=0, k=2048, v=2560, z_a=3072, x=5120, B=9216, C=10240, z_s=11264, g_a=15360, g_s=17408)
P_WIDTH = 19456
_GATE_COL = _COL["g_a"]
HALF_ATTN = ATTN_WIDTH // 2
LANES = 128
SEQS_PER_STEP = 16

_VMEM_LIMIT = 56 * 1024 * 1024


def _params(sem):
    return pltpu.CompilerParams(dimension_semantics=sem, vmem_limit_bytes=_VMEM_LIMIT)


def _silu(v):
    half = 0.5 * v
    return half + half * jnp.tanh(half)


def _div(v, n):
    assert n & (n - 1) == 0
    return v >> (n.bit_length() - 1)


def _mod(v, n):
    assert n & (n - 1) == 0
    return v & (n - 1)


_NORM_ROWS = 128


def _cast_weights_kernel(w_ref, o_ref):
    o_ref[...] = w_ref[...].astype(o_ref.dtype)


def _cast_weights(w_t, tn=1024):
    main_blocks = _SRC["dt"] // tn
    skip = _SRC["g_a"] - _SRC["dt"]
    return pl.pallas_call(
        _cast_weights_kernel,
        grid=(P_WIDTH // tn,),
        in_specs=[pl.BlockSpec((pl.Element(tn), pl.Element(D_MODEL)),
                               lambda j: (pl.multiple_of(j * tn + jnp.where(j < main_blocks, 0, skip), 8), 0))],
        out_specs=pl.BlockSpec((tn, D_MODEL), lambda j: (j, 0)),
        out_shape=jax.ShapeDtypeStruct((P_WIDTH, D_MODEL), bf16),
        compiler_params=_params(("parallel",)),
        name="cast_weights",
    )(w_t)


_NT = (((1,), (1,)), ((), ()))


def _inproj_kernel(x_ref, g_ref, w_ref, wdt_ref, dtb_ref, p_ref, dt_ref, h_ref):
    @pl.when(pl.program_id(1) == 0)
    def _():
        def norm_rows(i, carry):
            rows = pl.ds(pl.multiple_of(i * _NORM_ROWS, _NORM_ROWS), _NORM_ROWS)
            x = x_ref[rows, :]
            ms = jnp.mean(x * x, axis=-1, keepdims=True)
            h = (x * lax.rsqrt(ms + NORM_EPS) * g_ref[...]).astype(bf16)
            h_ref[rows, :] = h
            v = lax.dot_general(h, wdt_ref[...], _NT, preferred_element_type=f32) + dtb_ref[...]
            dt_ref[rows, :] = jnp.maximum(v, 0.0) + jnp.log1p(jnp.exp(-jnp.abs(v)))
            return carry

        lax.fori_loop(0, x_ref.shape[0] // _NORM_ROWS, norm_rows, 0)

    p_ref[...] = lax.dot_general(h_ref[...], w_ref[...], _NT, preferred_element_type=f32)


def _inproj(x, g, w, wdt, dtb, tm=1024, tn=1024):
    m = x.shape[0]
    return pl.pallas_call(
        _inproj_kernel,
        grid=(m // tm, P_WIDTH // tn),
        in_specs=[
            pl.BlockSpec((tm, D_MODEL), lambda i, j: (i, 0)),
            pl.BlockSpec((1, D_MODEL), lambda i, j: (0, 0)),
            pl.BlockSpec((tn, D_MODEL), lambda i, j: (j, 0)),
            pl.BlockSpec((LANES, D_MODEL), lambda i, j: (0, 0)),
            pl.BlockSpec((1, LANES), lambda i, j: (0, 0)),
        ],
        out_specs=[
            pl.BlockSpec((tm, tn), lambda i, j: (i, j)),
            pl.BlockSpec((tm, LANES), lambda i, j: (i, 0)),
        ],
        out_shape=[jax.ShapeDtypeStruct((m, P_WIDTH), f32), jax.ShapeDtypeStruct((m, LANES), f32)],
        scratch_shapes=[pltpu.VMEM((tm, D_MODEL), bf16)],
        compiler_params=_params(("parallel", "arbitrary")),
        name="inproj",
    )(x, g, w, wdt, dtb)


_HALF = LANES // HEAD_DIM
assert _HALF == 2


def _alibi_slopes():
    return jnp.exp2(-8.0 * jnp.arange(1, N_HEADS + 1, dtype=f32) / N_HEADS)


def _prompt_penalty():
    s = jnp.arange(WINDOW)[:, None]
    q = jnp.arange(WINDOW)[None, :]
    dist = jnp.where(s <= q, q - s, WINDOW + q - s).astype(f32)
    pen = (_alibi_slopes()[:, None, None] * dist[None]).reshape(N_KV_HEADS, Q_PER_KV, WINDOW, WINDOW)
    return pen.transpose(0, 2, 1, 3).reshape(N_KV_HEADS, WINDOW, Q_PER_KV * WINDOW)


def _sample_penalty():
    i = jnp.arange(DEC_SEQ)[:, None]
    c = jnp.arange(2 * WINDOW)[None, :]
    dist = WINDOW + i - c
    valid = (dist >= 0) & (dist < WINDOW) & (c < WINDOW + DEC_SEQ)
    pen = _alibi_slopes()[:, None, None] * dist.astype(f32)[None]
    return jnp.where(valid[None], pen, jnp.inf).reshape(N_HEADS * DEC_SEQ, 2 * WINDOW)


def _attn_prompt_kernel(sink_ref, pen_ref, q_ref, kc_ref, kp_ref, vc_ref, vp_ref, za0_ref, za1_ref, a_ref):
    cols4 = Q_PER_KV * WINDOW
    key = lax.broadcasted_iota(jnp.int32, (WINDOW, cols4), 0)
    qry = _mod(lax.broadcasted_iota(jnp.int32, (WINDOW, cols4), 1), WINDOW)
    from_cur = key <= qry
    low_half = lax.broadcasted_iota(jnp.int32, (WINDOW, LANES), 1) < HEAD_DIM
    prev_off = jnp.where(pl.program_id(1) > 0, 0.0, -jnp.inf)
    nt = (((1,), (1,)), ((), ()))
    kv_tiles = {}
    for j in range(N_KV_HEADS):
        if j % _HALF == 0:
            tile = slice((j // _HALF) * LANES, (j // _HALF + 1) * LANES)
            kv_tiles = dict(k_cur=kc_ref[:, tile].astype(bf16), k_prev=kp_ref[:, tile].astype(bf16),
                            v_cur=vc_ref[:, tile].T.astype(bf16), v_prev=vp_ref[:, tile].T.astype(bf16))
        mine = low_half == (j % _HALF == 0)
        pieces = []
        for h in range(j * Q_PER_KV, (j + 1) * Q_PER_KV):
            piece = q_ref[:, (h // _HALF) * LANES:(h // _HALF + 1) * LANES] * HEAD_DIM ** -0.5
            if h % _HALF != j % _HALF:
                piece = pltpu.roll(piece, HEAD_DIM, 1)
            pieces.append(jnp.where(mine, piece, 0.0))
        q = jnp.concatenate(pieces, axis=0).astype(bf16)
        s_cur = lax.dot_general(kv_tiles["k_cur"], q, nt, preferred_element_type=f32)
        s_prev = lax.dot_general(kv_tiles["k_prev"], q, nt, preferred_element_type=f32)
        t = jnp.where(from_cur, s_cur, s_prev + prev_off) - pen_ref[j]
        sinks = jnp.concatenate([jnp.full((1, WINDOW), sink_ref[j * Q_PER_KV + g], f32)
                                 for g in range(Q_PER_KV)], axis=1)
        m = jnp.maximum(jnp.max(t, axis=0, keepdims=True), sinks)
        p = jnp.exp(t - m)
        inv = 1.0 / (jnp.sum(p, axis=0, keepdims=True) + jnp.exp(sinks - m))
        o = jnp.dot(kv_tiles["v_cur"], jnp.where(from_cur, p, 0.0).astype(bf16), preferred_element_type=f32)
        o += jnp.dot(kv_tiles["v_prev"], jnp.where(from_cur, 0.0, p).astype(bf16), preferred_element_type=f32)
        o = o * inv
        dims = slice((j % _HALF) * HEAD_DIM, (j % _HALF + 1) * HEAD_DIM)
        for pair in range(Q_PER_KV // _HALF):
            g0 = pair * _HALF
            two_heads = jnp.concatenate([o[dims, g * WINDOW:(g + 1) * WINDOW] for g in (g0, g0 + 1)], axis=0)
            first = (j * Q_PER_KV // _HALF + pair) * LANES
            za_ref = (za0_ref, za1_ref)[first // HALF_ATTN]
            za = za_ref[:, first % HALF_ATTN:first % HALF_ATTN + LANES]
            a_ref[:, first:first + LANES] = (two_heads.T * _silu(za)).astype(a_ref.dtype)


def _attn_prompt(p, sinks, batch):
    nb = SEQ // WINDOW
    kcol, vcol = _COL["k"] // KV_WIDTH, _COL["v"] // KV_WIDTH
    zcol = _COL["z_a"] // HALF_ATTN
    cur = lambda b, i: b * nb + i
    prev = lambda b, i: b * nb + jnp.maximum(i - 1, 0)
    half_block = lambda col: pl.BlockSpec((WINDOW, HALF_ATTN), lambda b, i: (cur(b, i), col))
    return pl.pallas_call(
        _attn_prompt_kernel,
        grid=(batch, nb),
        in_specs=[
            pl.BlockSpec(memory_space=pltpu.SMEM),
            pl.BlockSpec((N_KV_HEADS, WINDOW, Q_PER_KV * WINDOW), lambda b, i: (0, 0, 0)),
            pl.BlockSpec((WINDOW, ATTN_WIDTH), lambda b, i: (cur(b, i), _COL["q"] // ATTN_WIDTH)),
            pl.BlockSpec((WINDOW, KV_WIDTH), lambda b, i: (cur(b, i), kcol)),
            pl.BlockSpec((WINDOW, KV_WIDTH), lambda b, i: (prev(b, i), kcol)),
            pl.BlockSpec((WINDOW, KV_WIDTH), lambda b, i: (cur(b, i), vcol)),
            pl.BlockSpec((WINDOW, KV_WIDTH), lambda b, i: (prev(b, i), vcol)),
            half_block(zcol), half_block(zcol + 1),
        ],
        out_specs=pl.BlockSpec((WINDOW, ATTN_WIDTH), lambda b, i: (cur(b, i), 0)),
        out_shape=jax.ShapeDtypeStruct((batch * SEQ, ATTN_WIDTH), bf16),
        compiler_params=_params(("parallel", "parallel")),
        name="attn_prompt",
    )(sinks, _prompt_penalty(), p, p, p, p, p, p, p)


_ATTN_SEQS = 8


def _attn_sample_kernel(pen_ref, sink_ref, q_ref, kn_ref, vn_ref, za0_ref, za1_ref, ck_ref, cv_ref,
                        a_ref, ko_ref, vo_ref, ak_ref, av_ref):
    keys = 2 * WINDOW
    pad = jnp.zeros((keys - WINDOW - DEC_SEQ, KV_WIDTH), f32)
    ak_ref[WINDOW + DEC_SEQ:, :] = pad
    av_ref[WINDOW + DEC_SEQ:, :] = pad
    low_half = lax.broadcasted_iota(jnp.int32, (DEC_SEQ, LANES), 1) < HEAD_DIM
    nt = (((1,), (1,)), ((), ()))

    def to_half(piece, src, dst):
        return piece if src == dst else pltpu.roll(piece, HEAD_DIM, 1)

    def one_sequence(n, carry):
        new = pl.ds(pl.multiple_of(n * DEC_SEQ, DEC_SEQ), DEC_SEQ)
        for cache_ref, new_ref, all_ref, out_ref in ((ck_ref, kn_ref, ak_ref, ko_ref),
                                                     (cv_ref, vn_ref, av_ref, vo_ref)):
            all_ref[:WINDOW, :] = cache_ref[n]
            all_ref[WINDOW:WINDOW + DEC_SEQ, :] = new_ref[new, :]
            out_ref[n] = all_ref[DEC_SEQ:WINDOW + DEC_SEQ, :]
        k16, v16 = ak_ref[...].astype(bf16), av_ref[...].astype(bf16)
        scores = []
        for j in range(N_KV_HEADS):
            pieces = []
            for h in range(j * Q_PER_KV, (j + 1) * Q_PER_KV):
                piece = q_ref[new, (h // _HALF) * LANES:(h // _HALF + 1) * LANES] * HEAD_DIM ** -0.5
                piece = to_half(piece, h % _HALF, j % _HALF)
                pieces.append(jnp.where(low_half == (j % _HALF == 0), piece, 0.0))
            qj = jnp.concatenate(pieces, axis=0).astype(bf16)
            kj = k16[:, (j // _HALF) * LANES:(j // _HALF + 1) * LANES]
            scores.append(lax.dot_general(qj, kj, nt, preferred_element_type=f32))
        t = jnp.concatenate(scores, axis=0) - pen_ref[...]
        sinks = sink_ref[...]
        m = jnp.maximum(jnp.max(t, axis=-1, keepdims=True), sinks)
        p = jnp.exp(t - m)
        inv = 1.0 / (jnp.sum(p, axis=-1, keepdims=True) + jnp.exp(sinks - m))
        p16 = p.astype(bf16)
        rows_per_kv = Q_PER_KV * DEC_SEQ
        outs = []
        for j in range(N_KV_HEADS):
            rows = slice(j * rows_per_kv, (j + 1) * rows_per_kv)
            vj = v16[:, (j // _HALF) * LANES:(j // _HALF + 1) * LANES]
            oj = jnp.dot(p16[rows], vj, preferred_element_type=f32) * inv[rows]
            for g in range(Q_PER_KV):
                h = j * Q_PER_KV + g
                outs.append(to_half(oj[g * DEC_SEQ:(g + 1) * DEC_SEQ], j % _HALF, h % _HALF))
        o = jnp.concatenate([jnp.where(low_half, outs[h], outs[h + 1]) for h in range(0, N_HEADS, _HALF)],
                            axis=1)
        za = jnp.concatenate([za0_ref[new, :], za1_ref[new, :]], axis=1)
        a_ref[new, :] = o * _silu(za)
        return carry

    lax.fori_loop(0, _ATTN_SEQS, one_sequence, 0)


def _attn_sample(p, sinks, cache_k, cache_v):
    nseq = cache_k.shape[0]
    rows = _ATTN_SEQS * DEC_SEQ
    cache_spec = pl.BlockSpec((_ATTN_SEQS, WINDOW, KV_WIDTH), lambda s: (s, 0, 0))
    sink_col = jnp.repeat(sinks.astype(f32), DEC_SEQ).reshape(N_HEADS * DEC_SEQ, 1)
    zcol = _COL["z_a"] // HALF_ATTN
    half_block = lambda col: pl.BlockSpec((rows, HALF_ATTN), lambda s: (s, col))
    return pl.pallas_call(
        _attn_sample_kernel,
        grid=(nseq // _ATTN_SEQS,),
        in_specs=[
            pl.BlockSpec((N_HEADS * DEC_SEQ, 2 * WINDOW), lambda s: (0, 0)),
            pl.BlockSpec((N_HEADS * DEC_SEQ, 1), lambda s: (0, 0)),
            pl.BlockSpec((rows, ATTN_WIDTH), lambda s: (s, _COL["q"] // ATTN_WIDTH)),
            pl.BlockSpec((rows, KV_WIDTH), lambda s: (s, _COL["k"] // KV_WIDTH)),
            pl.BlockSpec((rows, KV_WIDTH), lambda s: (s, _COL["v"] // KV_WIDTH)),
            half_block(zcol), half_block(zcol + 1),
            cache_spec, cache_spec,
        ],
        out_specs=[pl.BlockSpec((rows, ATTN_WIDTH), lambda s: (s, 0)), cache_spec, cache_spec],
        out_shape=[
            jax.ShapeDtypeStruct((nseq * DEC_SEQ, ATTN_WIDTH), f32),
            jax.ShapeDtypeStruct(cache_k.shape, f32),
            jax.ShapeDtypeStruct(cache_v.shape, f32),
        ],
        scratch_shapes=[pltpu.VMEM((2 * WINDOW, KV_WIDTH), f32), pltpu.VMEM((2 * WINDOW, KV_WIDTH), f32)],
        compiler_params=_params(("parallel",)),
        name="attn_sample",
    )(_sample_penalty(), sink_col, p, p, p, p, p, cache_k, cache_v)


def _expand_heads(v):
    low = lax.broadcasted_iota(jnp.int32, (v.shape[0], LANES), 1) < SSM_HEAD_DIM
    tiles = [jnp.where(low, v[:, r:r + 1], v[:, r + 1:r + 2]) for r in range(0, HEADS_PER_GROUP, 2)]
    return jnp.concatenate(tiles, axis=1)


def _split3(v):
    hi = v.astype(bf16)
    rest = v - hi.astype(f32)
    mid = rest.astype(bf16)
    return hi, mid, (rest - mid.astype(f32)).astype(bf16)


def _masked_sums(mask, v_c, v_r):
    m16 = jnp.where(mask, 1.0, 0.0).astype(bf16)
    out_c = sum(jnp.dot(m16, piece, preferred_element_type=f32) for piece in _split3(v_c))
    out_r = sum(lax.dot_general(piece, m16, (((1,), (1,)), ((), ())), preferred_element_type=f32)
                for piece in _split3(v_r))
    return out_c, out_r


def _select_rows(select, v):
    s16 = jnp.where(select, 1.0, 0.0).astype(bf16)
    return sum(jnp.dot(s16, piece, preferred_element_type=f32) for piece in _split3(v))


_ROW_BIAS, _ROW_D, _ROW_GAIN, _ROW_ALOG, _ROW_ALOG_COL = CONV_W, CONV_W + 1, CONV_W + 2, CONV_W + 3, CONV_W + 4
_CONST_ROWS = _ROW_ALOG_COL + HEADS_PER_GROUP
_CONST_WIDTH = GROUP_WIDTH + 2 * D_STATE
_X_COLS, _B_COLS, _C_COLS = (slice(0, GROUP_WIDTH), slice(GROUP_WIDTH, GROUP_WIDTH + D_STATE),
                             slice(GROUP_WIDTH + D_STATE, _CONST_WIDTH))


def _ssd_constants(conv_w, conv_b, a_log, d_skip, ssm_norm):
    grouped = lambda t, width: t.reshape(t.shape[0], SSM_GROUPS, width).transpose(1, 0, 2)
    padded = lambda t: jnp.pad(t, ((0, 0), (0, 0), (0, _CONST_WIDTH - t.shape[2])))
    taps = jnp.concatenate([conv_w, conv_b[None]], axis=0)
    top = jnp.concatenate([grouped(taps[:, :SSM_WIDTH], GROUP_WIDTH),
                           grouped(taps[:, SSM_WIDTH:SSM_WIDTH + BC_WIDTH], D_STATE),
                           grouped(taps[:, SSM_WIDTH + BC_WIDTH:], D_STATE)], axis=2)
    al = a_log.reshape(SSM_GROUPS, HEADS_PER_GROUP)
    return jnp.concatenate([top, padded(grouped(jnp.repeat(d_skip, SSM_HEAD_DIM)[None], GROUP_WIDTH)),
                            padded(grouped(ssm_norm[None], GROUP_WIDTH)),
                            padded(al[:, None, :]), padded(al[:, :, None])], axis=1)


def _intra_chunk(cb, causal, acum_c, acum_r, dt_r, xc):
    lane = lax.broadcasted_iota(jnp.int32, (CHUNK, LANES), 1)
    pieces = []
    for pair in range(HEADS_PER_GROUP // 2):
        x2 = xc[:, pair * LANES:(pair + 1) * LANES]
        acc = jnp.zeros((CHUNK, LANES), f32)
        for half in range(2):
            r = 2 * pair + half
            keep = (lane < SSM_HEAD_DIM) if half == 0 else (lane >= SSM_HEAD_DIM)
            decay = jnp.exp(jnp.where(causal, acum_c[:, r:r + 1] - acum_r[r:r + 1, :], -jnp.inf))
            acc += jnp.dot((cb * decay * dt_r[r:r + 1, :]).astype(bf16), jnp.where(keep, x2, 0.0).astype(bf16),
                           preferred_element_type=f32)
        pieces.append(acc)
    return jnp.concatenate(pieces, axis=1)


def _gated_norm(y, z, gain):
    u = y * _silu(z)
    ms = jnp.mean(u * u, axis=-1, keepdims=True)
    return u * lax.rsqrt(ms + NORM_EPS) * gain


def _conv_silu(cur, shifted, cst_ref, cols):
    y = cst_ref[_ROW_BIAS:_ROW_BIAS + 1, cols] + cst_ref[CONV_W - 1:CONV_W, cols] * cur
    for k in range(1, CONV_W):
        y = y + cst_ref[CONV_W - 1 - k:CONV_W - k, cols] * shifted[k - 1]
    return _silu(y)


def _decay_sums(mask, dt_c, dt_r, cst_ref):
    a_c = -jnp.exp(cst_ref[_ROW_ALOG:_ROW_ALOG + 1, 0:HEADS_PER_GROUP])
    a_r = -jnp.exp(cst_ref[_ROW_ALOG_COL:_ROW_ALOG_COL + HEADS_PER_GROUP, 0:1])
    return _masked_sums(mask, dt_c * a_c, dt_r * a_r)


_SSM_PARTS = SSM_WIDTH // BC_WIDTH


def _ssd_prompt_kernel(*refs):
    x_refs, refs = refs[:_SSM_PARTS], refs[_SSM_PARTS:]
    z_refs, refs = refs[:_SSM_PARTS], refs[_SSM_PARTS:]
    b_ref, c_ref, dt_ref, cst_ref, y_ref, st_ref, xpad, bpad, cpad, state = refs
    groups_per_part = SSM_GROUPS // _SSM_PARTS
    c = pl.program_id(1)
    tail = 8

    @pl.when(c == 0)
    def _():
        state[...] = jnp.zeros_like(state)
        for pad in (xpad, bpad, cpad):
            pad[:tail, :] = jnp.zeros((tail, pad.shape[1]), f32)

    @pl.when(c > 0)
    def _():
        for pad in (xpad, bpad, cpad):
            pad[:tail, :] = pad[CHUNK:CHUNK + tail, :]

    for i, part_ref in enumerate(x_refs):
        xpad[tail:, i * BC_WIDTH:(i + 1) * BC_WIDTH] = part_ref[...]
    bpad[tail:, :] = b_ref[...]
    cpad[tail:, :] = c_ref[...]
    row = lax.broadcasted_iota(jnp.int32, (CHUNK, CHUNK), 0)
    col = lax.broadcasted_iota(jnp.int32, (CHUNK, CHUNK), 1)
    causal = row >= col
    dt = dt_ref[...]
    dt_t = dt.T

    for g in range(SSM_GROUPS):
        cst = cst_ref.at[g]
        xs = slice(g * GROUP_WIDTH, (g + 1) * GROUP_WIDTH)
        ns = slice(g * D_STATE, (g + 1) * D_STATE)
        conv = []
        for pad, lanes, cols in ((xpad, xs, _X_COLS), (bpad, ns, _B_COLS), (cpad, ns, _C_COLS)):
            shifted = [pad[tail - k:tail - k + CHUNK, lanes] for k in range(1, CONV_W)]
            conv.append(_conv_silu(pad[tail:, lanes], shifted, cst, cols))
        xc, bm, cm = conv
        z = z_refs[g // groups_per_part][:, (g % groups_per_part) * GROUP_WIDTH:
                                         (g % groups_per_part + 1) * GROUP_WIDTH]

        heads = slice(g * HEADS_PER_GROUP, (g + 1) * HEADS_PER_GROUP)
        dt_c, dt_r = dt[:, heads], dt_t[heads, :]
        acum_c, acum_r = _decay_sums(causal, dt_c, dt_r, cst)
        cb = lax.dot_general(cm.astype(bf16), bm.astype(bf16), (((1,), (1,)), ((), ())),
                             preferred_element_type=f32)
        y = _intra_chunk(cb, causal, acum_c, acum_r, dt_r, xc)

        s_in = state[g]
        decay_in = _expand_heads(jnp.exp(acum_c))
        y += decay_in * jnp.dot(cm.astype(bf16), s_in.astype(bf16), preferred_element_type=f32)
        weight_out = _expand_heads(jnp.exp(acum_c[CHUNK - 1:CHUNK, :] - acum_c) * dt_c)
        state[g] = decay_in[CHUNK - 1:CHUNK, :] * s_in + jnp.dot(
            bm.T.astype(bf16), (xc * weight_out).astype(bf16), preferred_element_type=f32)

        y += cst[_ROW_D:_ROW_D + 1, _X_COLS] * xc
        y_ref[:, xs] = _gated_norm(y, z, cst[_ROW_GAIN:_ROW_GAIN + 1, _X_COLS]).astype(y_ref.dtype)

    @pl.when(c == pl.num_programs(1) - 1)
    def _():
        for g in range(SSM_GROUPS):
            st_ref[g * GROUP_WIDTH:(g + 1) * GROUP_WIDTH, :] = state[g].T


def _ssd_prompt(p, dt, consts, batch):
    nc = SEQ // CHUNK
    rb = lambda b, c: b * nc + c
    part = lambda name, i=0: pl.BlockSpec((CHUNK, BC_WIDTH), lambda b, c: (rb(b, c), _COL[name] // BC_WIDTH + i))
    return pl.pallas_call(
        _ssd_prompt_kernel,
        grid=(batch, nc),
        in_specs=[
            *[part("x", i) for i in range(_SSM_PARTS)],
            *[part("z_s", i) for i in range(_SSM_PARTS)],
            part("B"), part("C"),
            pl.BlockSpec((CHUNK, LANES), lambda b, c: (rb(b, c), 0)),
            pl.BlockSpec((SSM_GROUPS, _CONST_ROWS, _CONST_WIDTH), lambda b, c: (0, 0, 0)),
        ],
        out_specs=[
            pl.BlockSpec((CHUNK, SSM_WIDTH), lambda b, c: (rb(b, c), 0)),
            pl.BlockSpec((None, SSM_WIDTH, D_STATE), lambda b, c: (b, 0, 0)),
        ],
        out_shape=[
            jax.ShapeDtypeStruct((batch * SEQ, SSM_WIDTH), bf16),
            jax.ShapeDtypeStruct((batch, SSM_WIDTH, D_STATE), f32),
        ],
        scratch_shapes=[
            pltpu.VMEM((CHUNK + 8, SSM_WIDTH), f32),
            pltpu.VMEM((CHUNK + 8, BC_WIDTH), f32),
            pltpu.VMEM((CHUNK + 8, BC_WIDTH), f32),
            pltpu.VMEM((SSM_GROUPS, D_STATE, GROUP_WIDTH), f32),
        ],
        compiler_params=_params(("parallel", "arbitrary")),
        name="ssd_prompt",
    )(*[p] * (2 * _SSM_PARTS + 2), dt, consts)


def _ssd_sample_kernel(x_ref, b_ref, c_ref, z_ref, dtc_ref, dtr_ref, cst_ref,
                       cx_ref, cbuf_ref, cc_ref, s0_ref, y_ref, s1_ref, nx_ref, nb_ref, nc_ref):
    row1 = lax.broadcasted_iota(jnp.int32, (CHUNK, 1), 0)
    keep_rows = CONV_W - 1
    r = lax.broadcasted_iota(jnp.int32, (CHUNK, CHUNK), 0)
    c = lax.broadcasted_iota(jnp.int32, (CHUNK, CHUNK), 1)
    scatter = (c == _mod(r, DEC_SEQ) * SEQS_PER_STEP + _div(r, DEC_SEQ)) & (_mod(r, DEC_SEQ) < keep_rows)
    gather = ((c == _mod(r, SEQS_PER_STEP) * DEC_SEQ + DEC_SEQ - keep_rows + _div(r, SEQS_PER_STEP))
              & (r < keep_rows * SEQS_PER_STEP))
    conv = []
    for raw_ref, buf_ref, new_ref, cols in ((x_ref, cx_ref, nx_ref, _X_COLS), (b_ref, cbuf_ref, nb_ref, _B_COLS),
                                            (c_ref, cc_ref, nc_ref, _C_COLS)):
        cur = raw_ref[...]
        width = cur.shape[1]
        state_rows = jnp.concatenate([buf_ref[j] for j in range(keep_rows)]
                                     + [jnp.zeros((CHUNK - keep_rows * SEQS_PER_STEP, width), f32)], axis=0)
        cached = _select_rows(scatter, state_rows)
        moved = _select_rows(gather, cur)
        for j in range(keep_rows):
            new_ref[j] = moved[j * SEQS_PER_STEP:(j + 1) * SEQS_PER_STEP]
        shifted = []
        for k in range(1, CONV_W):
            from_cache = pltpu.roll(cached, (k - (CONV_W - 1)) % CHUNK, 0)
            shifted.append(jnp.where(_mod(row1, DEC_SEQ) >= k, pltpu.roll(cur, k, 0), from_cache))
        conv.append(_conv_silu(cur, shifted, cst_ref, cols))
    xc, bm, cm = conv

    dt_c, dt_r = dtc_ref[...], dtr_ref[...]
    row = lax.broadcasted_iota(jnp.int32, (CHUNK, CHUNK), 0)
    col = lax.broadcasted_iota(jnp.int32, (CHUNK, CHUNK), 1)
    same_seq = _div(row, DEC_SEQ) == _div(col, DEC_SEQ)
    causal = same_seq & (row >= col)
    acum_c, acum_r = _decay_sums(causal, dt_c, dt_r, cst_ref)
    cb = lax.dot_general(cm.astype(bf16), bm.astype(bf16), (((1,), (1,)), ((), ())),
                         preferred_element_type=f32)
    y = _intra_chunk(cb, causal, acum_c, acum_r, dt_r, xc)

    to_end, _ = _decay_sums(same_seq & (row < col), dt_c, dt_r, cst_ref)
    decay_in = _expand_heads(jnp.exp(acum_c))
    xw = (xc * _expand_heads(jnp.exp(to_end) * dt_c)).T.astype(bf16)
    keep = jnp.exp(acum_r)
    seq_of_row = _div(row1, DEC_SEQ)
    inter = jnp.zeros((CHUNK, GROUP_WIDTH), f32)
    for n in range(SEQS_PER_STEP):
        mine = seq_of_row == n
        s0 = s0_ref[n]
        inter += lax.dot_general(jnp.where(mine, cm, 0.0).astype(bf16), s0.astype(bf16),
                                 (((1,), (1,)), ((), ())), preferred_element_type=f32)
        update = jnp.dot(xw, jnp.where(mine, bm, 0.0).astype(bf16), preferred_element_type=f32)
        last_lane = (n + 1) * DEC_SEQ - 1
        for r in range(HEADS_PER_GROUP):
            rows = slice(r * SSM_HEAD_DIM, (r + 1) * SSM_HEAD_DIM)
            s1_ref[n, rows, :] = keep[r:r + 1, last_lane:last_lane + 1] * s0[rows] + update[rows]
    y += decay_in * inter + cst_ref[_ROW_D:_ROW_D + 1, _X_COLS] * xc
    y_ref[...] = _gated_norm(y, z_ref[...], cst_ref[_ROW_GAIN:_ROW_GAIN + 1, _X_COLS]).astype(y_ref.dtype)


def _ssd_sample(p, dt_c, dt_r, consts, conv_state, s0):
    nseq = s0.shape[0]
    nsb = nseq // SEQS_PER_STEP
    xcol, zcol = _COL["x"] // GROUP_WIDTH, _COL["z_s"] // GROUP_WIDTH
    bcol, ccol = _COL["B"] // D_STATE, _COL["C"] // D_STATE
    conv_block = lambda width, first: pl.BlockSpec(
        (CONV_W - 1, SEQS_PER_STEP, width), lambda s, g: (0, s, first + g))
    new_conv = lambda width: jax.ShapeDtypeStruct((CONV_W - 1, nseq, SSM_GROUPS * width), f32)
    state_spec = pl.BlockSpec((SEQS_PER_STEP, GROUP_WIDTH, D_STATE), lambda s, g: (s, g, 0))
    return pl.pallas_call(
        _ssd_sample_kernel,
        grid=(nsb, SSM_GROUPS),
        in_specs=[
            pl.BlockSpec((CHUNK, GROUP_WIDTH), lambda s, g: (s, xcol + g)),
            pl.BlockSpec((CHUNK, D_STATE), lambda s, g: (s, bcol + g)),
            pl.BlockSpec((CHUNK, D_STATE), lambda s, g: (s, ccol + g)),
            pl.BlockSpec((CHUNK, GROUP_WIDTH), lambda s, g: (s, zcol + g)),
            pl.BlockSpec((None, CHUNK, HEADS_PER_GROUP), lambda s, g: (g, s, 0)),
            pl.BlockSpec((None, HEADS_PER_GROUP, CHUNK), lambda s, g: (g, 0, s)),
            pl.BlockSpec((None, _CONST_ROWS, _CONST_WIDTH), lambda s, g: (g, 0, 0)),
            conv_block(GROUP_WIDTH, 0),
            conv_block(D_STATE, SSM_WIDTH // D_STATE),
            conv_block(D_STATE, (SSM_WIDTH + BC_WIDTH) // D_STATE),
            state_spec,
        ],
        out_specs=[pl.BlockSpec((CHUNK, GROUP_WIDTH), lambda s, g: (s, g)), state_spec,
                   conv_block(GROUP_WIDTH, 0), conv_block(D_STATE, 0), conv_block(D_STATE, 0)],
        out_shape=[
            jax.ShapeDtypeStruct((nseq * DEC_SEQ, SSM_WIDTH), bf16),
            jax.ShapeDtypeStruct(s0.shape, f32),
            new_conv(GROUP_WIDTH), new_conv(D_STATE), new_conv(D_STATE),
        ],
        compiler_params=_params(("parallel", "parallel")),
        name="ssd_sample",
    )(p, p, p, p, dt_c, dt_r, consts, conv_state, conv_state, conv_state, s0)


def _merge_kernel(a_ref, s_ref, ga_ref, gs_ref, wa_ref, ws_ref, o_ref):
    ya = jnp.dot(a_ref[...].astype(bf16), wa_ref[...], preferred_element_type=f32)
    ys = jnp.dot(s_ref[...], ws_ref[...], preferred_element_type=f32)
    o_ref[...] = (jax.nn.sigmoid(ga_ref[...]) * ya + jax.nn.sigmoid(gs_ref[...]) * ys).astype(o_ref.dtype)


def _merge(attn, ssm, p, wa, ws, tm=1024, tn=512):
    m = attn.shape[0]
    return pl.pallas_call(
        _merge_kernel,
        grid=(m // tm, D_MODEL // tn),
        in_specs=[
            pl.BlockSpec((tm, ATTN_WIDTH), lambda i, j: (i, 0)),
            pl.BlockSpec((tm, SSM_WIDTH), lambda i, j: (i, 0)),
            pl.BlockSpec((tm, tn), lambda i, j: (i, _COL["g_a"] // tn + j)),
            pl.BlockSpec((tm, tn), lambda i, j: (i, _COL["g_s"] // tn + j)),
            pl.BlockSpec((ATTN_WIDTH, tn), lambda i, j: (0, j)),
            pl.BlockSpec((SSM_WIDTH, tn), lambda i, j: (0, j)),
        ],
        out_specs=pl.BlockSpec((tm, tn), lambda i, j: (i, j)),
        out_shape=jax.ShapeDtypeStruct((m, D_MODEL), bf16),
        compiler_params=_params(("parallel", "arbitrary")),
        name="merge",
    )(attn, ssm, p, p, wa, ws)


def _out_kernel(m_ref, wo_ref, x_ref, g_ref, o_ref):
    y = jnp.dot(m_ref[...], wo_ref[...], preferred_element_type=f32)
    ms = jnp.mean(y * y, axis=-1, keepdims=True)
    o_ref[...] = x_ref[...] + y * lax.rsqrt(ms + NORM_EPS) * g_ref[...]


def _out(merged, wo, x, g, tm=512):
    m = x.shape[0]
    return pl.pallas_call(
        _out_kernel,
        grid=(m // tm,),
        in_specs=[
            pl.BlockSpec((tm, D_MODEL), lambda i: (i, 0)),
            pl.BlockSpec((D_MODEL, D_MODEL), lambda i: (0, 0)),
            pl.BlockSpec((tm, D_MODEL), lambda i: (i, 0)),
            pl.BlockSpec((1, D_MODEL), lambda i: (0, 0)),
        ],
        out_specs=pl.BlockSpec((tm, D_MODEL), lambda i: (i, 0)),
        out_shape=jax.ShapeDtypeStruct((m, D_MODEL), f32),
        compiler_params=_params(("parallel",)),
        name="outproj",
    )(merged, wo, x, g)


def _group_layouts(dt, rows):
    d = dt[:, :SSM_HEADS].reshape(rows, SSM_GROUPS, HEADS_PER_GROUP)
    return d.transpose(1, 0, 2), d.transpose(1, 2, 0)


def _layer(xp, xs, cache_k, cache_v, state_ssm, state_conv, norm_pre, w_in, conv_w, conv_b, dt_bias,
           a_log, d_skip, ssm_norm, attn_sinks, w_attn_br, w_ssm_br, w_out, norm_post):
    batch, nseq = xp.shape[0], xs.shape[0]
    mp, ms = batch * SEQ, nseq * DEC_SEQ
    w_t = w_in.T
    w_main = _cast_weights(w_t)
    w_dt = jnp.pad(w_t[_SRC["dt"]:_SRC["dt"] + SSM_HEADS], ((0, LANES - SSM_HEADS), (0, 0))).astype(bf16)
    dtb = jnp.pad(dt_bias, (0, LANES - SSM_HEADS)).reshape(1, LANES)
    g_pre = norm_pre.reshape(1, D_MODEL)

    ssd_consts = _ssd_constants(conv_w, conv_b, a_log, d_skip, ssm_norm)
    wa, ws, wo = w_attn_br.astype(bf16), w_ssm_br.astype(bf16), w_out.astype(bf16)
    g_post = norm_post.reshape(1, D_MODEL)

    xp2, xs2 = xp.reshape(mp, D_MODEL), xs.reshape(ms, D_MODEL)
    pp, dtp = _inproj(xp2, g_pre, w_main, w_dt, dtb)
    ps, dts = _inproj(xs2, g_pre, w_main, w_dt, dtb)

    attn_p = _attn_prompt(pp, attn_sinks, batch)
    ssm_p, st_p = _ssd_prompt(pp, dtp, ssd_consts, batch)
    yp = _out(_merge(attn_p, ssm_p, pp, wa, ws), wo, xp2, g_post)
    pp3 = pp.reshape(batch, SEQ, P_WIDTH)
    k_p = pp3[:, SEQ - WINDOW:, _COL["k"]:_COL["k"] + KV_WIDTH]
    v_p = pp3[:, SEQ - WINDOW:, _COL["v"]:_COL["v"] + KV_WIDTH]
    conv_p = pp3[:, SEQ - (CONV_W - 1):, _COL["x"]:_COL["x"] + CONV_DIM]

    attn_s, k_s, v_s = _attn_sample(ps, attn_sinks, cache_k.reshape(nseq, WINDOW, KV_WIDTH),
                                    cache_v.reshape(nseq, WINDOW, KV_WIDTH))
    ssm_s, st_s, *new_conv = _ssd_sample(ps, *_group_layouts(dts, ms), ssd_consts, state_conv.transpose(1, 0, 2),
                                         state_ssm.reshape(nseq, SSM_WIDTH, D_STATE))
    ys = _out(_merge(attn_s, ssm_s, ps, wa, ws), wo, xs2, g_post)
    conv_s = jnp.concatenate(new_conv, axis=2).transpose(1, 0, 2)

    kv = lambda t, n: t.reshape(1, n, WINDOW, N_KV_HEADS, HEAD_DIM)
    st = lambda t, n: t.reshape(1, n, SSM_HEADS, SSM_HEAD_DIM, D_STATE)
    return (yp.reshape(xp.shape), ys.reshape(xs.shape), kv(k_p, batch), kv(v_p, batch), st(st_p, batch),
            conv_p[None], kv(k_s, nseq), kv(v_s, nseq), st(st_s, nseq), conv_s[None])


def kernel(x_prompt, x_sample, cache_k, cache_v, state_ssm, state_conv, norm_pre, w_in, conv_w, conv_b,
           dt_bias, a_log, d_skip, ssm_norm, attn_sinks, w_attn_br, w_ssm_br, w_out, norm_post):
    assert w_in.shape[0] == 1, "single-layer trunk"
    return _layer(x_prompt, x_sample, cache_k[0], cache_v[0], state_ssm[0], state_conv[0], norm_pre[0],
                  w_in[0], conv_w[0], conv_b[0], dt_bias[0], a_log[0], d_skip[0], ssm_norm[0],
                  attn_sinks[0], w_attn_br[0], w_ssm_br[0], w_out[0], norm_post[0])
```

```python
import functools

import jax
import jax.numpy as jnp
from jax import lax
from jax.experimental import pallas as pl
from jax.experimental.pallas import tpu as pltpu

f32 = jnp.float32
bf16 = jnp.bfloat16

D_MODEL = 2048
SEQ = 4096
DEC_SEQ = 8
N_HEADS = 32
N_KV_HEADS = 8
HEAD_DIM = 64
Q_PER_KV = N_HEADS // N_KV_HEADS
ATTN_WIDTH = N_HEADS * HEAD_DIM
KV_WIDTH = N_KV_HEADS * HEAD_DIM
WINDOW = 128
SSM_WIDTH = 2 * D_MODEL
SSM_HEAD_DIM = 64
SSM_HEADS = SSM_WIDTH // SSM_HEAD_DIM
SSM_GROUPS = 8
HEADS_PER_GROUP = SSM_HEADS // SSM_GROUPS
GROUP_WIDTH = HEADS_PER_GROUP * SSM_HEAD_DIM
D_STATE = 128
CONV_W = 4
BC_WIDTH = SSM_GROUPS * D_STATE
CONV_DIM = SSM_WIDTH + 2 * BC_WIDTH
CHUNK = 128
NORM_EPS = 1e-6

_SRC = dict(q=0, k=2048, v=2560, z_a=3072, xbc=5120, z_s=11264, dt=15360, g_a=15424, g_s=17472)
_COL = dict(q=0, k=2048, v=2560, z_a=3072, x=5120, B=9216, C=10240, z_s=11264, g_a=15360, g_s=17408)
P_WIDTH = 19456
_GATE_COL = _COL["g_a"]
HALF_ATTN = ATTN_WIDTH // 2
LANES = 128
SEQS_PER_STEP = 16

_VMEM_LIMIT = 56 * 1024 * 1024


def _params(sem):
    return pltpu.CompilerParams(dimension_semantics=sem, vmem_limit_bytes=_VMEM_LIMIT)


def _silu(v):
    half = 0.5 * v
    return half + half * jnp.tanh(half)


def _div(v, n):
    assert n & (n - 1) == 0
    return v >> (n.bit_length() - 1)


def _mod(v, n):
    assert n & (n - 1) == 0
    return v & (n - 1)


_NORM_ROWS = 128


def _cast_weights_kernel(w_ref, o_ref):
    o_ref[...] = w_ref[...].astype(o_ref.dtype)


def _cast_weights(w_t, tn=1024):
    main_blocks = _SRC["dt"] // tn
    skip = _SRC["g_a"] - _SRC["dt"]
    return pl.pallas_call(
        _cast_weights_kernel,
        grid=(P_WIDTH // tn,),
        in_specs=[pl.BlockSpec((pl.Element(tn), pl.Element(D_MODEL)),
                               lambda j: (pl.multiple_of(j * tn + jnp.where(j < main_blocks, 0, skip), 8), 0))],
        out_specs=pl.BlockSpec((tn, D_MODEL), lambda j: (j, 0)),
        out_shape=jax.ShapeDtypeStruct((P_WIDTH, D_MODEL), bf16),
        compiler_params=_params(("parallel",)),
        name="cast_weights",
    )(w_t)


_NT = (((1,), (1,)), ((), ()))


def _inproj_kernel(x_ref, g_ref, w_ref, wdt_ref, dtb_ref, p_ref, dt_ref, h_ref):
    @pl.when(pl.program_id(1) == 0)
    def _():
        def norm_rows(i, carry):
            rows = pl.ds(pl.multiple_of(i * _NORM_ROWS, _NORM_ROWS), _NORM_ROWS)
            x = x_ref[rows, :]
            ms = jnp.mean(x * x, axis=-1, keepdims=True)
            h = (x * lax.rsqrt(ms + NORM_EPS) * g_ref[...]).astype(bf16)
            h_ref[rows, :] = h
            v = lax.dot_general(h, wdt_ref[...], _NT, preferred_element_type=f32) + dtb_ref[...]
            dt_ref[rows, :] = jnp.maximum(v, 0.0) + jnp.log1p(jnp.exp(-jnp.abs(v)))
            return carry

        lax.fori_loop(0, x_ref.shape[0] // _NORM_ROWS, norm_rows, 0)

    p_ref[...] = lax.dot_general(h_ref[...], w_ref[...], _NT, preferred_element_type=f32)


def _inproj(x, g, w, wdt, dtb, tm=1024, tn=1024):
    m = x.shape[0]
    return pl.pallas_call(
        _inproj_kernel,
        grid=(m // tm, P_WIDTH // tn),
        in_specs=[
            pl.BlockSpec((tm, D_MODEL), lambda i, j: (i, 0)),
            pl.BlockSpec((1, D_MODEL), lambda i, j: (0, 0)),
            pl.BlockSpec((tn, D_MODEL), lambda i, j: (j, 0)),
            pl.BlockSpec((LANES, D_MODEL), lambda i, j: (0, 0)),
            pl.BlockSpec((1, LANES), lambda i, j: (0, 0)),
        ],
        out_specs=[
            pl.BlockSpec((tm, tn), lambda i, j: (i, j)),
            pl.BlockSpec((tm, LANES), lambda i, j: (i, 0)),
        ],
        out_shape=[jax.ShapeDtypeStruct((m, P_WIDTH), f32), jax.ShapeDtypeStruct((m, LANES), f32)],
        scratch_shapes=[pltpu.VMEM((tm, D_MODEL), bf16)],
        compiler_params=_params(("parallel", "arbitrary")),
        name="inproj",
    )(x, g, w, wdt, dtb)


_HALF = LANES // HEAD_DIM
assert _HALF == 2


def _alibi_slopes():
    return jnp.exp2(-8.0 * jnp.arange(1, N_HEADS + 1, dtype=f32) / N_HEADS)


def _prompt_penalty():
    s = jnp.arange(WINDOW)[:, None]
    q = jnp.arange(WINDOW)[None, :]
    dist = jnp.where(s <= q, q - s, WINDOW + q - s).astype(f32)
    pen = (_alibi_slopes()[:, None, None] * dist[None]).reshape(N_KV_HEADS, Q_PER_KV, WINDOW, WINDOW)
    return pen.transpose(0, 2, 1, 3).reshape(N_KV_HEADS, WINDOW, Q_PER_KV * WINDOW)


def _sample_penalty():
    i = jnp.arange(DEC_SEQ)[:, None]
    c = jnp.arange(2 * WINDOW)[None, :]
    dist = WINDOW + i - c
    valid = (dist >= 0) & (dist < WINDOW) & (c < WINDOW + DEC_SEQ)
    pen = _alibi_slopes()[:, None, None] * dist.astype(f32)[None]
    return jnp.where(valid[None], pen, jnp.inf).reshape(N_HEADS * DEC_SEQ, 2 * WINDOW)


def _attn_prompt_kernel(sink_ref, pen_ref, q_ref, kc_ref, kp_ref, vc_ref, vp_ref, za0_ref, za1_ref, a_ref):
    cols4 = Q_PER_KV * WINDOW
    key = lax.broadcasted_iota(jnp.int32, (WINDOW, cols4), 0)
    qry = _mod(lax.broadcasted_iota(jnp.int32, (WINDOW, cols4), 1), WINDOW)
    from_cur = key <= qry
    low_half = lax.broadcasted_iota(jnp.int32, (WINDOW, LANES), 1) < HEAD_DIM
    prev_off = jnp.where(pl.program_id(1) > 0, 0.0, -jnp.inf)
    nt = (((1,), (1,)), ((), ()))
    kv_tiles = {}
    for j in range(N_KV_HEADS):
        if j % _HALF == 0:
            tile = slice((j // _HALF) * LANES, (j // _HALF + 1) * LANES)
            kv_tiles = dict(k_cur=kc_ref[:, tile].astype(bf16), k_prev=kp_ref[:, tile].astype(bf16),
                            v_cur=vc_ref[:, tile].T.astype(bf16), v_prev=vp_ref[:, tile].T.astype(bf16))
        mine = low_half == (j % _HALF == 0)
        pieces = []
        for h in range(j * Q_PER_KV, (j + 1) * Q_PER_KV):
            piece = q_ref[:, (h // _HALF) * LANES:(h // _HALF + 1) * LANES] * HEAD_DIM ** -0.5
            if h % _HALF != j % _HALF:
                piece = pltpu.roll(piece, HEAD_DIM, 1)
            pieces.append(jnp.where(mine, piece, 0.0))
        q = jnp.concatenate(pieces, axis=0).astype(bf16)
        s_cur = lax.dot_general(kv_tiles["k_cur"], q, nt, preferred_element_type=f32)
        s_prev = lax.dot_general(kv_tiles["k_prev"], q, nt, preferred_element_type=f32)
        t = jnp.where(from_cur, s_cur, s_prev + prev_off) - pen_ref[j]
        sinks = jnp.concatenate([jnp.full((1, WINDOW), sink_ref[j * Q_PER_KV + g], f32)
                                 for g in range(Q_PER_KV)], axis=1)
        m = jnp.maximum(jnp.max(t, axis=0, keepdims=True), sinks)
        p = jnp.exp(t - m)
        inv = 1.0 / (jnp.sum(p, axis=0, keepdims=True) + jnp.exp(sinks - m))
        o = jnp.dot(kv_tiles["v_cur"], jnp.where(from_cur, p, 0.0).astype(bf16), preferred_element_type=f32)
        o += jnp.dot(kv_tiles["v_prev"], jnp.where(from_cur, 0.0, p).astype(bf16), preferred_element_type=f32)
        o = o * inv
        dims = slice((j % _HALF) * HEAD_DIM, (j % _HALF + 1) * HEAD_DIM)
        for pair in range(Q_PER_KV // _HALF):
            g0 = pair * _HALF
            two_heads = jnp.concatenate([o[dims, g * WINDOW:(g + 1) * WINDOW] for g in (g0, g0 + 1)], axis=0)
            first = (j * Q_PER_KV // _HALF + pair) * LANES
            za_ref = (za0_ref, za1_ref)[first // HALF_ATTN]
            za = za_ref[:, first % HALF_ATTN:first % HALF_ATTN + LANES]
            a_ref[:, first:first + LANES] = (two_heads.T * _silu(za)).astype(a_ref.dtype)


def _attn_prompt(p, sinks, batch):
    nb = SEQ // WINDOW
    kcol, vcol = _COL["k"] // KV_WIDTH, _COL["v"] // KV_WIDTH
    zcol = _COL["z_a"] // HALF_ATTN
    cur = lambda b, i: b * nb + i
    prev = lambda b, i: b * nb + jnp.maximum(i - 1, 0)
    half_block = lambda col: pl.BlockSpec((WINDOW, HALF_ATTN), lambda b, i: (cur(b, i), col))
    return pl.pallas_call(
        _attn_prompt_kernel,
        grid=(batch, nb),
        in_specs=[
            pl.BlockSpec(memory_space=pltpu.SMEM),
            pl.BlockSpec((N_KV_HEADS, WINDOW, Q_PER_KV * WINDOW), lambda b, i: (0, 0, 0)),
            pl.BlockSpec((WINDOW, ATTN_WIDTH), lambda b, i: (cur(b, i), _COL["q"] // ATTN_WIDTH)),
            pl.BlockSpec((WINDOW, KV_WIDTH), lambda b, i: (cur(b, i), kcol)),
            pl.BlockSpec((WINDOW, KV_WIDTH), lambda b, i: (prev(b, i), kcol)),
            pl.BlockSpec((WINDOW, KV_WIDTH), lambda b, i: (cur(b, i), vcol)),
            pl.BlockSpec((WINDOW, KV_WIDTH), lambda b, i: (prev(b, i), vcol)),
            half_block(zcol), half_block(zcol + 1),
        ],
        out_specs=pl.BlockSpec((WINDOW, ATTN_WIDTH), lambda b, i: (cur(b, i), 0)),
        out_shape=jax.ShapeDtypeStruct((batch * SEQ, ATTN_WIDTH), bf16),
        compiler_params=_params(("parallel", "parallel")),
        name="attn_prompt",
    )(sinks, _prompt_penalty(), p, p, p, p, p, p, p)


_ATTN_SEQS = 8


def _attn_sample_kernel(pen_ref, sink_ref, q_ref, kn_ref, vn_ref, za0_ref, za1_ref, ck_ref, cv_ref,
                        a_ref, ko_ref, vo_ref, ak_ref, av_ref):
    keys = 2 * WINDOW
    pad = jnp.zeros((keys - WINDOW - DEC_SEQ, KV_WIDTH), f32)
    ak_ref[WINDOW + DEC_SEQ:, :] = pad
    av_ref[WINDOW + DEC_SEQ:, :] = pad
    low_half = lax.broadcasted_iota(jnp.int32, (DEC_SEQ, LANES), 1) < HEAD_DIM
    nt = (((1,), (1,)), ((), ()))

    def to_half(piece, src, dst):
        return piece if src == dst else pltpu.roll(piece, HEAD_DIM, 1)

    def one_sequence(n, carry):
        new = pl.ds(pl.multiple_of(n * DEC_SEQ, DEC_SEQ), DEC_SEQ)
        for cache_ref, new_ref, all_ref, out_ref in ((ck_ref, kn_ref, ak_ref, ko_ref),
                                                     (cv_ref, vn_ref, av_ref, vo_ref)):
            all_ref[:WINDOW, :] = cache_ref[n]
            all_ref[WINDOW:WINDOW + DEC_SEQ, :] = new_ref[new, :]
            out_ref[n] = all_ref[DEC_SEQ:WINDOW + DEC_SEQ, :]
        k16, v16 = ak_ref[...].astype(bf16), av_ref[...].astype(bf16)
        scores = []
        for j in range(N_KV_HEADS):
            pieces = []
            for h in range(j * Q_PER_KV, (j + 1) * Q_PER_KV):
                piece = q_ref[new, (h // _HALF) * LANES:(h // _HALF + 1) * LANES] * HEAD_DIM ** -0.5
                piece = to_half(piece, h % _HALF, j % _HALF)
                pieces.append(jnp.where(low_half == (j % _HALF == 0), piece, 0.0))
            qj = jnp.concatenate(pieces, axis=0).astype(bf16)
            kj = k16[:, (j // _HALF) * LANES:(j // _HALF + 1) * LANES]
            scores.append(lax.dot_general(qj, kj, nt, preferred_element_type=f32))
        t = jnp.concatenate(scores, axis=0) - pen_ref[...]
        sinks = sink_ref[...]
        m = jnp.maximum(jnp.max(t, axis=-1, keepdims=True), sinks)
        p = jnp.exp(t - m)
        inv = 1.0 / (jnp.sum(p, axis=-1, keepdims=True) + jnp.exp(sinks - m))
        p16 = p.astype(bf16)
        rows_per_kv = Q_PER_KV * DEC_SEQ
        outs = []
        for j in range(N_KV_HEADS):
            rows = slice(j * rows_per_kv, (j + 1) * rows_per_kv)
            vj = v16[:, (j // _HALF) * LANES:(j // _HALF + 1) * LANES]
            oj = jnp.dot(p16[rows], vj, preferred_element_type=f32) * inv[rows]
            for g in range(Q_PER_KV):
                h = j * Q_PER_KV + g
                outs.append(to_half(oj[g * DEC_SEQ:(g + 1) * DEC_SEQ], j % _HALF, h % _HALF))
        o = jnp.concatenate([jnp.where(low_half, outs[h], outs[h + 1]) for h in range(0, N_HEADS, _HALF)],
                            axis=1)
        za = jnp.concatenate([za0_ref[new, :], za1_ref[new, :]], axis=1)
        a_ref[new, :] = o * _silu(za)
        return carry

    lax.fori_loop(0, _ATTN_SEQS, one_sequence, 0)


def _attn_sample(p, sinks, cache_k, cache_v):
    nseq = cache_k.shape[0]
    rows = _ATTN_SEQS * DEC_SEQ
    cache_spec = pl.BlockSpec((_ATTN_SEQS, WINDOW, KV_WIDTH), lambda s: (s, 0, 0))
    sink_col = jnp.repeat(sinks.astype(f32), DEC_SEQ).reshape(N_HEADS * DEC_SEQ, 1)
    zcol = _COL["z_a"] // HALF_ATTN
    half_block = lambda col: pl.BlockSpec((rows, HALF_ATTN), lambda s: (s, col))
    return pl.pallas_call(
        _attn_sample_kernel,
        grid=(nseq // _ATTN_SEQS,),
        in_specs=[
            pl.BlockSpec((N_HEADS * DEC_SEQ, 2 * WINDOW), lambda s: (0, 0)),
            pl.BlockSpec((N_HEADS * DEC_SEQ, 1), lambda s: (0, 0)),
            pl.BlockSpec((rows, ATTN_WIDTH), lambda s: (s, _COL["q"] // ATTN_WIDTH)),
            pl.BlockSpec((rows, KV_WIDTH), lambda s: (s, _COL["k"] // KV_WIDTH)),
            pl.BlockSpec((rows, KV_WIDTH), lambda s: (s, _COL["v"] // KV_WIDTH)),
            half_block(zcol), half_block(zcol + 1),
            cache_spec, cache_spec,
        ],
        out_specs=[pl.BlockSpec((rows, ATTN_WIDTH), lambda s: (s, 0)), cache_spec, cache_spec],
        out_shape=[
            jax.ShapeDtypeStruct((nseq * DEC_SEQ, ATTN_WIDTH), f32),
            jax.ShapeDtypeStruct(cache_k.shape, f32),
            jax.ShapeDtypeStruct(cache_v.shape, f32),
        ],
        scratch_shapes=[pltpu.VMEM((2 * WINDOW, KV_WIDTH), f32), pltpu.VMEM((2 * WINDOW, KV_WIDTH), f32)],
        compiler_params=_params(("parallel",)),
        name="attn_sample",
    )(_sample_penalty(), sink_col, p, p, p, p, p, cache_k, cache_v)


def _expand_heads(v):
    low = lax.broadcasted_iota(jnp.int32, (v.shape[0], LANES), 1) < SSM_HEAD_DIM
    tiles = [jnp.where(low, v[:, r:r + 1], v[:, r + 1:r + 2]) for r in range(0, HEADS_PER_GROUP, 2)]
    return jnp.concatenate(tiles, axis=1)


def _split3(v):
    hi = v.astype(bf16)
    rest = v - hi.astype(f32)
    mid = rest.astype(bf16)
    return hi, mid, (rest - mid.astype(f32)).astype(bf16)


def _masked_sums(mask, v_c, v_r):
    m16 = jnp.where(mask, 1.0, 0.0).astype(bf16)
    out_c = sum(jnp.dot(m16, piece, preferred_element_type=f32) for piece in _split3(v_c))
    out_r = sum(lax.dot_general(piece, m16, (((1,), (1,)), ((), ())), preferred_element_type=f32)
                for piece in _split3(v_r))
    return out_c, out_r


def _select_rows(select, v):
    s16 = jnp.where(select, 1.0, 0.0).astype(bf16)
    return sum(jnp.dot(s16, piece, preferred_element_type=f32) for piece in _split3(v))


_ROW_BIAS, _ROW_D, _ROW_GAIN, _ROW_ALOG, _ROW_ALOG_COL = CONV_W, CONV_W + 1, CONV_W + 2, CONV_W + 3, CONV_W + 4
_CONST_ROWS = _ROW_ALOG_COL + HEADS_PER_GROUP
_CONST_WIDTH = GROUP_WIDTH + 2 * D_STATE
_X_COLS, _B_COLS, _C_COLS = (slice(0, GROUP_WIDTH), slice(GROUP_WIDTH, GROUP_WIDTH + D_STATE),
                             slice(GROUP_WIDTH + D_STATE, _CONST_WIDTH))


def _ssd_constants(conv_w, conv_b, a_log, d_skip, ssm_norm):
    grouped = lambda t, width: t.reshape(t.shape[0], SSM_GROUPS, width).transpose(1, 0, 2)
    padded = lambda t: jnp.pad(t, ((0, 0), (0, 0), (0, _CONST_WIDTH - t.shape[2])))
    taps = jnp.concatenate([conv_w, conv_b[None]], axis=0)
    top = jnp.concatenate([grouped(taps[:, :SSM_WIDTH], GROUP_WIDTH),
                           grouped(taps[:, SSM_WIDTH:SSM_WIDTH + BC_WIDTH], D_STATE),
                           grouped(taps[:, SSM_WIDTH + BC_WIDTH:], D_STATE)], axis=2)
    al = a_log.reshape(SSM_GROUPS, HEADS_PER_GROUP)
    return jnp.concatenate([top, padded(grouped(jnp.repeat(d_skip, SSM_HEAD_DIM)[None], GROUP_WIDTH)),
                            padded(grouped(ssm_norm[None], GROUP_WIDTH)),
                            padded(al[:, None, :]), padded(al[:, :, None])], axis=1)


def _intra_chunk(cb, causal, acum_c, acum_r, dt_r, xc):
    lane = lax.broadcasted_iota(jnp.int32, (CHUNK, LANES), 1)
    pieces = []
    for pair in range(HEADS_PER_GROUP // 2):
        x2 = xc[:, pair * LANES:(pair + 1) * LANES]
        acc = jnp.zeros((CHUNK, LANES), f32)
        for half in range(2):
            r = 2 * pair + half
            keep = (lane < SSM_HEAD_DIM) if half == 0 else (lane >= SSM_HEAD_DIM)
            decay = jnp.exp(jnp.where(causal, acum_c[:, r:r + 1] - acum_r[r:r + 1, :], -jnp.inf))
            acc += jnp.dot((cb * decay * dt_r[r:r + 1, :]).astype(bf16), jnp.where(keep, x2, 0.0).astype(bf16),
                           preferred_element_type=f32)
        pieces.append(acc)
    return jnp.concatenate(pieces, axis=1)


def _gated_norm(y, z, gain):
    u = y * _silu(z)
    ms = jnp.mean(u * u, axis=-1, keepdims=True)
    return u * lax.rsqrt(ms + NORM_EPS) * gain


def _conv_silu(cur, shifted, cst_ref, cols):
    y = cst_ref[_ROW_BIAS:_ROW_BIAS + 1, cols] + cst_ref[CONV_W - 1:CONV_W, cols] * cur
    for k in range(1, CONV_W):
        y = y + cst_ref[CONV_W - 1 - k:CONV_W - k, cols] * shifted[k - 1]
    return _silu(y)


def _decay_sums(mask, dt_c, dt_r, cst_ref):
    a_c = -jnp.exp(cst_ref[_ROW_ALOG:_ROW_ALOG + 1, 0:HEADS_PER_GROUP])
    a_r = -jnp.exp(cst_ref[_ROW_ALOG_COL:_ROW_ALOG_COL + HEADS_PER_GROUP, 0:1])
    return _masked_sums(mask, dt_c * a_c, dt_r * a_r)


_SSM_PARTS = SSM_WIDTH // BC_WIDTH


def _ssd_prompt_kernel(*refs):
    x_refs, refs = refs[:_SSM_PARTS], refs[_SSM_PARTS:]
    z_refs, refs = refs[:_SSM_PARTS], refs[_SSM_PARTS:]
    b_ref, c_ref, dtc_ref, dtr_ref, cst_ref, y_ref, st_ref, xpad, bpad, cpad, state = refs
    groups_per_part = SSM_GROUPS // _SSM_PARTS
    c = pl.program_id(1)
    tail = 8

    @pl.when(c == 0)
    def _():
        state[...] = jnp.zeros_like(state)
        for pad in (xpad, bpad, cpad):
            pad[:tail, :] = jnp.zeros((tail, pad.shape[1]), f32)

    @pl.when(c > 0)
    def _():
        for pad in (xpad, bpad, cpad):
            pad[:tail, :] = pad[CHUNK:CHUNK + tail, :]

    for i, part_ref in enumerate(x_refs):
        xpad[tail:, i * BC_WIDTH:(i + 1) * BC_WIDTH] = part_ref[...]
    bpad[tail:, :] = b_ref[...]
    cpad[tail:, :] = c_ref[...]
    row = lax.broadcasted_iota(jnp.int32, (CHUNK, CHUNK), 0)
    col = lax.broadcasted_iota(jnp.int32, (CHUNK, CHUNK), 1)
    causal = row >= col

    for g in range(SSM_GROUPS):
        cst = cst_ref.at[g]
        xs = slice(g * GROUP_WIDTH, (g + 1) * GROUP_WIDTH)
        ns = slice(g * D_STATE, (g + 1) * D_STATE)
        conv = []
        for pad, lanes, cols in ((xpad, xs, _X_COLS), (bpad, ns, _B_COLS), (cpad, ns, _C_COLS)):
            shifted = [pad[tail - k:tail - k + CHUNK, lanes] for k in range(1, CONV_W)]
            conv.append(_conv_silu(pad[tail:, lanes], shifted, cst, cols))
        xc, bm, cm = conv
        z = z_refs[g // groups_per_part][:, (g % groups_per_part) * GROUP_WIDTH:
                                         (g % groups_per_part + 1) * GROUP_WIDTH]

        dt_c, dt_r = dtc_ref[g], dtr_ref[g]
        acum_c, acum_r = _decay_sums(causal, dt_c, dt_r, cst)
        cb = lax.dot_general(cm.astype(bf16), bm.astype(bf16), (((1,), (1,)), ((), ())),
                             preferred_element_type=f32)
        y = _intra_chunk(cb, causal, acum_c, acum_r, dt_r, xc)

        s_in = state[g]
        decay_in = _expand_heads(jnp.exp(acum_c))
        y += decay_in * jnp.dot(cm.astype(bf16), s_in.astype(bf16), preferred_element_type=f32)
        weight_out = _expand_heads(jnp.exp(acum_c[CHUNK - 1:CHUNK, :] - acum_c) * dt_c)
        state[g] = decay_in[CHUNK - 1:CHUNK, :] * s_in + jnp.dot(
            bm.T.astype(bf16), (xc * weight_out).astype(bf16), preferred_element_type=f32)

        y += cst[_ROW_D:_ROW_D + 1, _X_COLS] * xc
        y_ref[:, xs] = _gated_norm(y, z, cst[_ROW_GAIN:_ROW_GAIN + 1, _X_COLS]).astype(y_ref.dtype)

    @pl.when(c == pl.num_programs(1) - 1)
    def _():
        for g in range(SSM_GROUPS):
            st_ref[g * GROUP_WIDTH:(g + 1) * GROUP_WIDTH, :] = state[g].T


def _ssd_prompt(p, dt_c, dt_r, consts, batch):
    nc = SEQ // CHUNK
    rb = lambda b, c: b * nc + c
    part = lambda name, i=0: pl.BlockSpec((CHUNK, BC_WIDTH), lambda b, c: (rb(b, c), _COL[name] // BC_WIDTH + i))
    return pl.pallas_call(
        _ssd_prompt_kernel,
        grid=(batch, nc),
        in_specs=[
            *[part("x", i) for i in range(_SSM_PARTS)],
            *[part("z_s", i) for i in range(_SSM_PARTS)],
            part("B"), part("C"),
            pl.BlockSpec((SSM_GROUPS, CHUNK, HEADS_PER_GROUP), lambda b, c: (0, rb(b, c), 0)),
            pl.BlockSpec((SSM_GROUPS, HEADS_PER_GROUP, CHUNK), lambda b, c: (0, 0, rb(b, c))),
            pl.BlockSpec((SSM_GROUPS, _CONST_ROWS, _CONST_WIDTH), lambda b, c: (0, 0, 0)),
        ],
        out_specs=[
            pl.BlockSpec((CHUNK, SSM_WIDTH), lambda b, c: (rb(b, c), 0)),
            pl.BlockSpec((None, SSM_WIDTH, D_STATE), lambda b, c: (b, 0, 0)),
        ],
        out_shape=[
            jax.ShapeDtypeStruct((batch * SEQ, SSM_WIDTH), bf16),
            jax.ShapeDtypeStruct((batch, SSM_WIDTH, D_STATE), f32),
        ],
        scratch_shapes=[
            pltpu.VMEM((CHUNK + 8, SSM_WIDTH), f32),
            pltpu.VMEM((CHUNK + 8, BC_WIDTH), f32),
            pltpu.VMEM((CHUNK + 8, BC_WIDTH), f32),
            pltpu.VMEM((SSM_GROUPS, D_STATE, GROUP_WIDTH), f32),
        ],
        compiler_params=_params(("parallel", "arbitrary")),
        name="ssd_prompt",
    )(*[p] * (2 * _SSM_PARTS + 2), dt_c, dt_r, consts)


def _ssd_sample_kernel(x_ref, b_ref, c_ref, z_ref, dtc_ref, dtr_ref, cst_ref,
                       cx_ref, cbuf_ref, cc_ref, s0_ref, y_ref, s1_ref, nx_ref, nb_ref, nc_ref):
    row1 = lax.broadcasted_iota(jnp.int32, (CHUNK, 1), 0)
    keep_rows = CONV_W - 1
    r = lax.broadcasted_iota(jnp.int32, (CHUNK, CHUNK), 0)
    c = lax.broadcasted_iota(jnp.int32, (CHUNK, CHUNK), 1)
    scatter = (c == _mod(r, DEC_SEQ) * SEQS_PER_STEP + _div(r, DEC_SEQ)) & (_mod(r, DEC_SEQ) < keep_rows)
    gather = ((c == _mod(r, SEQS_PER_STEP) * DEC_SEQ + DEC_SEQ - keep_rows + _div(r, SEQS_PER_STEP))
              & (r < keep_rows * SEQS_PER_STEP))
    conv = []
    for raw_ref, buf_ref, new_ref, cols in ((x_ref, cx_ref, nx_ref, _X_COLS), (b_ref, cbuf_ref, nb_ref, _B_COLS),
                                            (c_ref, cc_ref, nc_ref, _C_COLS)):
        cur = raw_ref[...]
        width = cur.shape[1]
        state_rows = jnp.concatenate([buf_ref[j] for j in range(keep_rows)]
                                     + [jnp.zeros((CHUNK - keep_rows * SEQS_PER_STEP, width), f32)], axis=0)
        cached = _select_rows(scatter, state_rows)
        moved = _select_rows(gather, cur)
        for j in range(keep_rows):
            new_ref[j] = moved[j * SEQS_PER_STEP:(j + 1) * SEQS_PER_STEP]
        shifted = []
        for k in range(1, CONV_W):
            from_cache = pltpu.roll(cached, (k - (CONV_W - 1)) % CHUNK, 0)
            shifted.append(jnp.where(_mod(row1, DEC_SEQ) >= k, pltpu.roll(cur, k, 0), from_cache))
        conv.append(_conv_silu(cur, shifted, cst_ref, cols))
    xc, bm, cm = conv

    dt_c, dt_r = dtc_ref[...], dtr_ref[...]
    row = lax.broadcasted_iota(jnp.int32, (CHUNK, CHUNK), 0)
    col = lax.broadcasted_iota(jnp.int32, (CHUNK, CHUNK), 1)
    same_seq = _div(row, DEC_SEQ) == _div(col, DEC_SEQ)
    causal = same_seq & (row >= col)
    acum_c, acum_r = _decay_sums(causal, dt_c, dt_r, cst_ref)
    cb = lax.dot_general(cm.astype(bf16), bm.astype(bf16), (((1,), (1,)), ((), ())),
                         preferred_element_type=f32)
    y = _intra_chunk(cb, causal, acum_c, acum_r, dt_r, xc)

    to_end, _ = _decay_sums(same_seq & (row < col), dt_c, dt_r, cst_ref)
    decay_in = _expand_heads(jnp.exp(acum_c))
    xw = (xc * _expand_heads(jnp.exp(to_end) * dt_c)).T.astype(bf16)
    keep = jnp.exp(acum_r)
    seq_of_row = _div(row1, DEC_SEQ)
    inter = jnp.zeros((CHUNK, GROUP_WIDTH), f32)
    for n in range(SEQS_PER_STEP):
        mine = seq_of_row == n
        s0 = s0_ref[n]
        inter += lax.dot_general(jnp.where(mine, cm, 0.0).astype(bf16), s0.astype(bf16),
                                 (((1,), (1,)), ((), ())), preferred_element_type=f32)
        update = jnp.dot(xw, jnp.where(mine, bm, 0.0).astype(bf16), preferred_element_type=f32)
        last_lane = (n + 1) * DEC_SEQ - 1
        for r in range(HEADS_PER_GROUP):
            rows = slice(r * SSM_HEAD_DIM, (r + 1) * SSM_HEAD_DIM)
            s1_ref[n, rows, :] = keep[r:r + 1, last_lane:last_lane + 1] * s0[rows] + update[rows]
    y += decay_in * inter + cst_ref[_ROW_D:_ROW_D + 1, _X_COLS] * xc
    y_ref[...] = _gated_norm(y, z_ref[...], cst_ref[_ROW_GAIN:_ROW_GAIN + 1, _X_COLS]).astype(y_ref.dtype)


def _ssd_sample(p, dt_c, dt_r, consts, conv_state, s0):
    nseq = s0.shape[0]
    nsb = nseq // SEQS_PER_STEP
    xcol, zcol = _COL["x"] // GROUP_WIDTH, _COL["z_s"] // GROUP_WIDTH
    bcol, ccol = _COL["B"] // D_STATE, _COL["C"] // D_STATE
    conv_block = lambda width, first: pl.BlockSpec(
        (CONV_W - 1, SEQS_PER_STEP, width), lambda s, g: (0, s, first + g))
    new_conv = lambda width: jax.ShapeDtypeStruct((CONV_W - 1, nseq, SSM_GROUPS * width), f32)
    state_spec = pl.BlockSpec((SEQS_PER_STEP, GROUP_WIDTH, D_STATE), lambda s, g: (s, g, 0))
    return pl.pallas_call(
        _ssd_sample_kernel,
        grid=(nsb, SSM_GROUPS),
        in_specs=[
            pl.BlockSpec((CHUNK, GROUP_WIDTH), lambda s, g: (s, xcol + g)),
            pl.BlockSpec((CHUNK, D_STATE), lambda s, g: (s, bcol + g)),
            pl.BlockSpec((CHUNK, D_STATE), lambda s, g: (s, ccol + g)),
            pl.BlockSpec((CHUNK, GROUP_WIDTH), lambda s, g: (s, zcol + g)),
            pl.BlockSpec((None, CHUNK, HEADS_PER_GROUP), lambda s, g: (g, s, 0)),
            pl.BlockSpec((None, HEADS_PER_GROUP, CHUNK), lambda s, g: (g, 0, s)),
            pl.BlockSpec((None, _CONST_ROWS, _CONST_WIDTH), lambda s, g: (g, 0, 0)),
            conv_block(GROUP_WIDTH, 0),
            conv_block(D_STATE, SSM_WIDTH // D_STATE),
            conv_block(D_STATE, (SSM_WIDTH + BC_WIDTH) // D_STATE),
            state_spec,
        ],
        out_specs=[pl.BlockSpec((CHUNK, GROUP_WIDTH), lambda s, g: (s, g)), state_spec,
                   conv_block(GROUP_WIDTH, 0), conv_block(D_STATE, 0), conv_block(D_STATE, 0)],
        out_shape=[
            jax.ShapeDtypeStruct((nseq * DEC_SEQ, SSM_WIDTH), bf16),
            jax.ShapeDtypeStruct(s0.shape, f32),
            new_conv(GROUP_WIDTH), new_conv(D_STATE), new_conv(D_STATE),
        ],
        compiler_params=_params(("parallel", "parallel")),
        name="ssd_sample",
    )(p, p, p, p, dt_c, dt_r, consts, conv_state, conv_state, conv_state, s0)


def _merge_kernel(a_ref, s_ref, ga_ref, gs_ref, wa_ref, ws_ref, o_ref):
    ya = jnp.dot(a_ref[...].astype(bf16), wa_ref[...], preferred_element_type=f32)
    ys = jnp.dot(s_ref[...], ws_ref[...], preferred_element_type=f32)
    o_ref[...] = (jax.nn.sigmoid(ga_ref[...]) * ya + jax.nn.sigmoid(gs_ref[...]) * ys).astype(o_ref.dtype)


def _merge(attn, ssm, p, wa, ws, tm=1024, tn=512):
    m = attn.shape[0]
    return pl.pallas_call(
        _merge_kernel,
        grid=(m // tm, D_MODEL // tn),
        in_specs=[
            pl.BlockSpec((tm, ATTN_WIDTH), lambda i, j: (i, 0)),
            pl.BlockSpec((tm, SSM_WIDTH), lambda i, j: (i, 0)),
            pl.BlockSpec((tm, tn), lambda i, j: (i, _COL["g_a"] // tn + j)),
            pl.BlockSpec((tm, tn), lambda i, j: (i, _COL["g_s"] // tn + j)),
            pl.BlockSpec((ATTN_WIDTH, tn), lambda i, j: (0, j)),
            pl.BlockSpec((SSM_WIDTH, tn), lambda i, j: (0, j)),
        ],
        out_specs=pl.BlockSpec((tm, tn), lambda i, j: (i, j)),
        out_shape=jax.ShapeDtypeStruct((m, D_MODEL), bf16),
        compiler_params=_params(("parallel", "arbitrary")),
        name="merge",
    )(attn, ssm, p, p, wa, ws)


def _out_kernel(m_ref, wo_ref, x_ref, g_ref, o_ref):
    y = jnp.dot(m_ref[...], wo_ref[...], preferred_element_type=f32)
    ms = jnp.mean(y * y, axis=-1, keepdims=True)
    o_ref[...] = x_ref[...] + y * lax.rsqrt(ms + NORM_EPS) * g_ref[...]


def _out(merged, wo, x, g, tm=512):
    m = x.shape[0]
    return pl.pallas_call(
        _out_kernel,
        grid=(m // tm,),
        in_specs=[
            pl.BlockSpec((tm, D_MODEL), lambda i: (i, 0)),
            pl.BlockSpec((D_MODEL, D_MODEL), lambda i: (0, 0)),
            pl.BlockSpec((tm, D_MODEL), lambda i: (i, 0)),
            pl.BlockSpec((1, D_MODEL), lambda i: (0, 0)),
        ],
        out_specs=pl.BlockSpec((tm, D_MODEL), lambda i: (i, 0)),
        out_shape=jax.ShapeDtypeStruct((m, D_MODEL), f32),
        compiler_params=_params(("parallel",)),
        name="outproj",
    )(merged, wo, x, g)


def _group_layouts(dt, rows):
    d = dt[:, :SSM_HEADS].reshape(rows, SSM_GROUPS, HEADS_PER_GROUP)
    return d.transpose(1, 0, 2), d.transpose(1, 2, 0)


def _layer(xp, xs, cache_k, cache_v, state_ssm, state_conv, norm_pre, w_in, conv_w, conv_b, dt_bias,
           a_log, d_skip, ssm_norm, attn_sinks, w_attn_br, w_ssm_br, w_out, norm_post):
    batch, nseq = xp.shape[0], xs.shape[0]
    mp, ms = batch * SEQ, nseq * DEC_SEQ
    w_t = w_in.T
    w_main = _cast_weights(w_t)
    w_dt = jnp.pad(w_t[_SRC["dt"]:_SRC["dt"] + SSM_HEADS], ((0, LANES - SSM_HEADS), (0, 0))).astype(bf16)
    dtb = jnp.pad(dt_bias, (0, LANES - SSM_HEADS)).reshape(1, LANES)
    g_pre = norm_pre.reshape(1, D_MODEL)

    ssd_consts = _ssd_constants(conv_w, conv_b, a_log, d_skip, ssm_norm)
    wa, ws, wo = w_attn_br.astype(bf16), w_ssm_br.astype(bf16), w_out.astype(bf16)
    g_post = norm_post.reshape(1, D_MODEL)

    xp2, xs2 = xp.reshape(mp, D_MODEL), xs.reshape(ms, D_MODEL)
    pp, dtp = _inproj(xp2, g_pre, w_main, w_dt, dtb)
    ps, dts = _inproj(xs2, g_pre, w_main, w_dt, dtb)

    attn_p = _attn_prompt(pp, attn_sinks, batch)
    ssm_p, st_p = _ssd_prompt(pp, *_group_layouts(dtp, mp), ssd_consts, batch)
    yp = _out(_merge(attn_p, ssm_p, pp, wa, ws), wo, xp2, g_post)
    pp3 = pp.reshape(batch, SEQ, P_WIDTH)
    k_p = pp3[:, SEQ - WINDOW:, _COL["k"]:_COL["k"] + KV_WIDTH]
    v_p = pp3[:, SEQ - WINDOW:, _COL["v"]:_COL["v"] + KV_WIDTH]
    conv_p = pp3[:, SEQ - (CONV_W - 1):, _COL["x"]:_COL["x"] + CONV_DIM]

    attn_s, k_s, v_s = _attn_sample(ps, attn_sinks, cache_k.reshape(nseq, WINDOW, KV_WIDTH),
                                    cache_v.reshape(nseq, WINDOW, KV_WIDTH))
    ssm_s, st_s, *new_conv = _ssd_sample(ps, *_group_layouts(dts, ms), ssd_consts, state_conv.transpose(1, 0, 2),
                                         state_ssm.reshape(nseq, SSM_WIDTH, D_STATE))
    ys = _out(_merge(attn_s, ssm_s, ps, wa, ws), wo, xs2, g_post)
    conv_s = jnp.concatenate(new_conv, axis=2).transpose(1, 0, 2)

    kv = lambda t, n: t.reshape(1, n, WINDOW, N_KV_HEADS, HEAD_DIM)
    st = lambda t, n: t.reshape(1, n, SSM_HEADS, SSM_HEAD_DIM, D_STATE)
    return (yp.reshape(xp.shape), ys.reshape(xs.shape), kv(k_p, batch), kv(v_p, batch), st(st_p, batch),
            conv_p[None], kv(k_s, nseq), kv(v_s, nseq), st(st_s, nseq), conv_s[None])


def kernel(x_prompt, x_sample, cache_k, cache_v, state_ssm, state_conv, norm_pre, w_in, conv_w, conv_b,
           dt_bias, a_log, d_skip, ssm_norm, attn_sinks, w_attn_br, w_ssm_br, w_out, norm_post):
    assert w_in.shape[0] == 1, "single-layer trunk"
    return _layer(x_prompt, x_sample, cache_k[0], cache_v[0], state_ssm[0], state_conv[0], norm_pre[0],
                  w_in[0], conv_w[0], conv_b[0], dt_bias[0], a_log[0], d_skip[0], ssm_norm[0],
                  attn_sinks[0], w_attn_br[0], w_ssm_br[0], w_out[0], norm_post[0])
```

```python
import jax
import jax.numpy as jnp
from jax import lax
from jax.experimental import pallas as pl
from jax.experimental.pallas import tpu as pltpu

f32 = jnp.float32
bf16 = jnp.bfloat16

D_MODEL = 2048
SEQ = 4096
DEC_SEQ = 8
N_HEADS = 32
N_KV_HEADS = 8
HEAD_DIM = 64
Q_PER_KV = N_HEADS // N_KV_HEADS
ATTN_WIDTH = N_HEADS * HEAD_DIM
KV_WIDTH = N_KV_HEADS * HEAD_DIM
WINDOW = 128
SSM_WIDTH = 2 * D_MODEL
SSM_HEAD_DIM = 64
SSM_HEADS = SSM_WIDTH // SSM_HEAD_DIM
SSM_GROUPS = 8
HEADS_PER_GROUP = SSM_HEADS // SSM_GROUPS
GROUP_WIDTH = HEADS_PER_GROUP * SSM_HEAD_DIM
D_STATE = 128
CONV_W = 4
BC_WIDTH = SSM_GROUPS * D_STATE
CONV_DIM = SSM_WIDTH + 2 * BC_WIDTH
CHUNK = 128
NORM_EPS = 1e-6

_SRC = dict(q=0, k=2048, v=2560, z_a=3072, xbc=5120, z_s=11264, dt=15360, g_a=15424, g_s=17472)
_COL = dict(q=0, k=2048, v=2560, z_a=3072, x=5120, B=9216, C=10240, z_s=11264, g_a=15360, g_s=17408)
P_WIDTH = 19456
HALF_ATTN = ATTN_WIDTH // 2
LANES = 128
SEQS_PER_STEP = 16

_VMEM_LIMIT = 56 * 1024 * 1024


def _params(sem):
    return pltpu.CompilerParams(dimension_semantics=sem, vmem_limit_bytes=_VMEM_LIMIT)


def _silu(v):
    half = 0.5 * v
    return half + half * jnp.tanh(half)


def _div(v, n):
    assert n & (n - 1) == 0
    return v >> (n.bit_length() - 1)


def _mod(v, n):
    assert n & (n - 1) == 0
    return v & (n - 1)


_NORM_ROWS = 128


def _cast_weights_kernel(w_ref, o_ref):
    o_ref[...] = w_ref[...].astype(o_ref.dtype)


def _cast_weights(w_t, tn=1024):
    main_blocks = _SRC["dt"] // tn
    skip = _SRC["g_a"] - _SRC["dt"]
    return pl.pallas_call(
        _cast_weights_kernel,
        grid=(P_WIDTH // tn,),
        in_specs=[pl.BlockSpec((pl.Element(tn), pl.Element(D_MODEL)),
                               lambda j: (pl.multiple_of(j * tn + jnp.where(j < main_blocks, 0, skip), 8), 0))],
        out_specs=pl.BlockSpec((tn, D_MODEL), lambda j: (j, 0)),
        out_shape=jax.ShapeDtypeStruct((P_WIDTH, D_MODEL), bf16),
        compiler_params=_params(("parallel",)),
        name="cast_weights",
    )(w_t)


_NT = (((1,), (1,)), ((), ()))


def _inproj_kernel(x_ref, g_ref, w_ref, wdt_ref, dtb_ref, p_ref, dt_ref, h_ref):
    @pl.when(pl.program_id(1) == 0)
    def _():
        def norm_rows(i, carry):
            rows = pl.ds(pl.multiple_of(i * _NORM_ROWS, _NORM_ROWS), _NORM_ROWS)
            x = x_ref[rows, :]
            ms = jnp.mean(x * x, axis=-1, keepdims=True)
            h = (x * lax.rsqrt(ms + NORM_EPS) * g_ref[...]).astype(bf16)
            h_ref[rows, :] = h
            v = lax.dot_general(h, wdt_ref[...], _NT, preferred_element_type=f32) + dtb_ref[...]
            dt_ref[rows, :] = jnp.maximum(v, 0.0) + jnp.log1p(jnp.exp(-jnp.abs(v)))
            return carry

        lax.fori_loop(0, x_ref.shape[0] // _NORM_ROWS, norm_rows, 0)

    p_ref[...] = lax.dot_general(h_ref[...], w_ref[...], _NT, preferred_element_type=f32)


def _inproj(x, g, w, wdt, dtb, tm=1024, tn=1024):
    m = x.shape[0]
    return pl.pallas_call(
        _inproj_kernel,
        grid=(m // tm, P_WIDTH // tn),
        in_specs=[
            pl.BlockSpec((tm, D_MODEL), lambda i, j: (i, 0)),
            pl.BlockSpec((1, D_MODEL), lambda i, j: (0, 0)),
            pl.BlockSpec((tn, D_MODEL), lambda i, j: (j, 0)),
            pl.BlockSpec((LANES, D_MODEL), lambda i, j: (0, 0)),
            pl.BlockSpec((1, LANES), lambda i, j: (0, 0)),
        ],
        out_specs=[
            pl.BlockSpec((tm, tn), lambda i, j: (i, j)),
            pl.BlockSpec((tm, LANES), lambda i, j: (i, 0)),
        ],
        out_shape=[jax.ShapeDtypeStruct((m, P_WIDTH), f32), jax.ShapeDtypeStruct((m, LANES), f32)],
        scratch_shapes=[pltpu.VMEM((tm, D_MODEL), bf16)],
        compiler_params=_params(("parallel", "arbitrary")),
        name="inproj",
    )(x, g, w, wdt, dtb)


_HALF = LANES // HEAD_DIM
assert _HALF == 2


def _alibi_slopes():
    return jnp.exp2(-8.0 * jnp.arange(1, N_HEADS + 1, dtype=f32) / N_HEADS)


def _prompt_penalty():
    s = jnp.arange(WINDOW)[:, None]
    q = jnp.arange(WINDOW)[None, :]
    dist = jnp.where(s <= q, q - s, WINDOW + q - s).astype(f32)
    pen = (_alibi_slopes()[:, None, None] * dist[None]).reshape(N_KV_HEADS, Q_PER_KV, WINDOW, WINDOW)
    return pen.transpose(0, 2, 1, 3).reshape(N_KV_HEADS, WINDOW, Q_PER_KV * WINDOW)


def _sample_penalty():
    i = jnp.arange(DEC_SEQ)[:, None]
    c = jnp.arange(2 * WINDOW)[None, :]
    dist = WINDOW + i - c
    valid = (dist >= 0) & (dist < WINDOW) & (c < WINDOW + DEC_SEQ)
    pen = _alibi_slopes()[:, None, None] * dist.astype(f32)[None]
    return jnp.where(valid[None], pen, jnp.inf).reshape(N_HEADS * DEC_SEQ, 2 * WINDOW)


_BLOCKS_PER_STEP = 2


def _attn_prompt_kernel(sink_ref, pen_ref, q_ref, kc_ref, kp_ref, vc_ref, vp_ref, za0_ref, za1_ref, a_ref):
    for sub in range(_BLOCKS_PER_STEP):
        rows = pl.ds(sub * WINDOW, WINDOW)
        before = pl.ds((sub - 1) * WINDOW, WINDOW)
        first_of_sequence = (pl.program_id(1) == 0) if sub == 0 else False
        _attn_prompt_block(
            sink_ref, pen_ref, q_ref.at[rows], kc_ref.at[rows], kp_ref if sub == 0 else kc_ref.at[before],
            vc_ref.at[rows], vp_ref if sub == 0 else vc_ref.at[before], za0_ref.at[rows], za1_ref.at[rows],
            a_ref.at[rows], jnp.where(first_of_sequence, -jnp.inf, 0.0))


def _attn_prompt_block(sink_ref, pen_ref, q_ref, kc_ref, kp_ref, vc_ref, vp_ref, za0_ref, za1_ref, a_ref, prev_off):
    cols4 = Q_PER_KV * WINDOW
    key = lax.broadcasted_iota(jnp.int32, (WINDOW, cols4), 0)
    qry = _mod(lax.broadcasted_iota(jnp.int32, (WINDOW, cols4), 1), WINDOW)
    from_cur = key <= qry
    low_half = lax.broadcasted_iota(jnp.int32, (WINDOW, LANES), 1) < HEAD_DIM
    nt = _NT
    kv_tiles = {}
    for j in range(N_KV_HEADS):
        if j % _HALF == 0:
            tile = slice((j // _HALF) * LANES, (j // _HALF + 1) * LANES)
            kv_tiles = dict(k_cur=kc_ref[:, tile].astype(bf16), k_prev=kp_ref[:, tile].astype(bf16),
                            v_cur=vc_ref[:, tile].T.astype(bf16), v_prev=vp_ref[:, tile].T.astype(bf16))
        mine = low_half == (j % _HALF == 0)
        pieces = []
        for h in range(j * Q_PER_KV, (j + 1) * Q_PER_KV):
            piece = q_ref[:, (h // _HALF) * LANES:(h // _HALF + 1) * LANES] * HEAD_DIM ** -0.5
            if h % _HALF != j % _HALF:
                piece = pltpu.roll(piece, HEAD_DIM, 1)
            pieces.append(jnp.where(mine, piece, 0.0))
        q = jnp.concatenate(pieces, axis=0).astype(bf16)
        s_cur = lax.dot_general(kv_tiles["k_cur"], q, nt, preferred_element_type=f32)
        s_prev = lax.dot_general(kv_tiles["k_prev"], q, nt, preferred_element_type=f32)
        t = jnp.where(from_cur, s_cur, s_prev + prev_off) - pen_ref[j]
        sinks = jnp.concatenate([jnp.full((1, WINDOW), sink_ref[j * Q_PER_KV + g], f32)
                                 for g in range(Q_PER_KV)], axis=1)
        m = jnp.maximum(jnp.max(t, axis=0, keepdims=True), sinks)
        p = jnp.exp(t - m)
        inv = 1.0 / (jnp.sum(p, axis=0, keepdims=True) + jnp.exp(sinks - m))
        o = jnp.dot(kv_tiles["v_cur"], jnp.where(from_cur, p, 0.0).astype(bf16), preferred_element_type=f32)
        o += jnp.dot(kv_tiles["v_prev"], jnp.where(from_cur, 0.0, p).astype(bf16), preferred_element_type=f32)
        o = o * inv
        dims = slice((j % _HALF) * HEAD_DIM, (j % _HALF + 1) * HEAD_DIM)
        for pair in range(Q_PER_KV // _HALF):
            g0 = pair * _HALF
            two_heads = jnp.concatenate([o[dims, g * WINDOW:(g + 1) * WINDOW] for g in (g0, g0 + 1)], axis=0)
            first = (j * Q_PER_KV // _HALF + pair) * LANES
            za_ref = (za0_ref, za1_ref)[first // HALF_ATTN]
            za = za_ref[:, first % HALF_ATTN:first % HALF_ATTN + LANES]
            a_ref[:, first:first + LANES] = (two_heads.T * _silu(za)).astype(a_ref.dtype)


def _attn_prompt(p, sinks, batch):
    step_rows = _BLOCKS_PER_STEP * WINDOW
    nb = SEQ // step_rows
    kcol, vcol = _COL["k"] // KV_WIDTH, _COL["v"] // KV_WIDTH
    zcol = _COL["z_a"] // HALF_ATTN
    cur = lambda b, i: b * nb + i
    prev = lambda b, i: (b * nb + i) * _BLOCKS_PER_STEP - jnp.where(i > 0, 1, 0)
    half_block = lambda col: pl.BlockSpec((step_rows, HALF_ATTN), lambda b, i: (cur(b, i), col))
    return pl.pallas_call(
        _attn_prompt_kernel,
        grid=(batch, nb),
        in_specs=[
            pl.BlockSpec(memory_space=pltpu.SMEM),
            pl.BlockSpec((N_KV_HEADS, WINDOW, Q_PER_KV * WINDOW), lambda b, i: (0, 0, 0)),
            pl.BlockSpec((step_rows, ATTN_WIDTH), lambda b, i: (cur(b, i), _COL["q"] // ATTN_WIDTH)),
            pl.BlockSpec((step_rows, KV_WIDTH), lambda b, i: (cur(b, i), kcol)),
            pl.BlockSpec((WINDOW, KV_WIDTH), lambda b, i: (prev(b, i), kcol)),
            pl.BlockSpec((step_rows, KV_WIDTH), lambda b, i: (cur(b, i), vcol)),
            pl.BlockSpec((WINDOW, KV_WIDTH), lambda b, i: (prev(b, i), vcol)),
            half_block(zcol), half_block(zcol + 1),
        ],
        out_specs=pl.BlockSpec((step_rows, ATTN_WIDTH), lambda b, i: (cur(b, i), 0)),
        out_shape=jax.ShapeDtypeStruct((batch * SEQ, ATTN_WIDTH), bf16),
        compiler_params=_params(("parallel", "parallel")),
        name="attn_prompt",
    )(sinks, _prompt_penalty(), p, p, p, p, p, p, p)


_ATTN_SEQS = 8


def _attn_sample_kernel(pen_ref, sink_ref, q_ref, kn_ref, vn_ref, za0_ref, za1_ref, ck_ref, cv_ref,
                        a_ref, ko_ref, vo_ref):
    pad = jnp.zeros((WINDOW - DEC_SEQ, KV_WIDTH), f32)
    low_half = lax.broadcasted_iota(jnp.int32, (DEC_SEQ, LANES), 1) < HEAD_DIM
    is_new = lax.broadcasted_iota(jnp.int32, (LANES, WINDOW), 1) < DEC_SEQ
    pairs = N_KV_HEADS // _HALF

    def to_half(piece, src, dst):
        return piece if src == dst else pltpu.roll(piece, HEAD_DIM, 1)

    def one_sequence(n, carry):
        new = pl.ds(pl.multiple_of(n * DEC_SEQ, DEC_SEQ), DEC_SEQ)
        k_new = jnp.concatenate([kn_ref[new, :], pad], axis=0)
        v_new = jnp.concatenate([vn_ref[new, :], pad], axis=0)
        kt16, vt16 = [], []
        for i in range(pairs):
            tile = slice(i * LANES, (i + 1) * LANES)
            for cache_ref, fresh, out_ref, as16 in ((ck_ref, k_new, ko_ref, kt16), (cv_ref, v_new, vo_ref, vt16)):
                old = cache_ref[n, i]
                as16.append(old.astype(bf16))
                out_ref[n, i] = pltpu.roll(jnp.where(is_new, fresh[:, tile].T, old), WINDOW - DEC_SEQ, 1)
        k_new16, v_new16 = k_new.astype(bf16), v_new.astype(bf16)
        scores = []
        for j in range(N_KV_HEADS):
            pieces = []
            for h in range(j * Q_PER_KV, (j + 1) * Q_PER_KV):
                piece = q_ref[new, (h // _HALF) * LANES:(h // _HALF + 1) * LANES] * HEAD_DIM ** -0.5
                piece = to_half(piece, h % _HALF, j % _HALF)
                pieces.append(jnp.where(low_half == (j % _HALF == 0), piece, 0.0))
            qj = jnp.concatenate(pieces, axis=0).astype(bf16)
            tile = slice((j // _HALF) * LANES, (j // _HALF + 1) * LANES)
            scores.append(jnp.concatenate(
                [jnp.dot(qj, kt16[j // _HALF], preferred_element_type=f32),
                 lax.dot_general(qj, k_new16[:, tile], _NT, preferred_element_type=f32)], axis=1))
        t = jnp.concatenate(scores, axis=0) - pen_ref[...]
        sinks = sink_ref[...]
        m = jnp.maximum(jnp.max(t, axis=-1, keepdims=True), sinks)
        p = jnp.exp(t - m)
        inv = 1.0 / (jnp.sum(p, axis=-1, keepdims=True) + jnp.exp(sinks - m))
        p16 = p.astype(bf16)
        rows_per_kv = Q_PER_KV * DEC_SEQ
        outs = []
        for j in range(N_KV_HEADS):
            rows = slice(j * rows_per_kv, (j + 1) * rows_per_kv)
            tile = slice((j // _HALF) * LANES, (j // _HALF + 1) * LANES)
            oj = lax.dot_general(p16[rows, :WINDOW], vt16[j // _HALF], _NT, preferred_element_type=f32)
            oj += jnp.dot(p16[rows, WINDOW:], v_new16[:, tile], preferred_element_type=f32)
            oj = oj * inv[rows]
            for g in range(Q_PER_KV):
                h = j * Q_PER_KV + g
                outs.append(to_half(oj[g * DEC_SEQ:(g + 1) * DEC_SEQ], j % _HALF, h % _HALF))
        o = jnp.concatenate([jnp.where(low_half, outs[h], outs[h + 1]) for h in range(0, N_HEADS, _HALF)],
                            axis=1)
        za = jnp.concatenate([za0_ref[new, :], za1_ref[new, :]], axis=1)
        a_ref[new, :] = o * _silu(za)
        return carry

    lax.fori_loop(0, _ATTN_SEQS, one_sequence, 0)


def _attn_sample(p, sinks, cache_k, cache_v):
    nseq = cache_k.shape[0]
    rows = _ATTN_SEQS * DEC_SEQ
    cache_spec = pl.BlockSpec((_ATTN_SEQS,) + cache_k.shape[1:], lambda s: (s, 0, 0, 0))
    sink_col = jnp.repeat(sinks.astype(f32), DEC_SEQ).reshape(N_HEADS * DEC_SEQ, 1)
    zcol = _COL["z_a"] // HALF_ATTN
    half_block = lambda col: pl.BlockSpec((rows, HALF_ATTN), lambda s: (s, col))
    return pl.pallas_call(
        _attn_sample_kernel,
        grid=(nseq // _ATTN_SEQS,),
        in_specs=[
            pl.BlockSpec((N_HEADS * DEC_SEQ, 2 * WINDOW), lambda s: (0, 0)),
            pl.BlockSpec((N_HEADS * DEC_SEQ, 1), lambda s: (0, 0)),
            pl.BlockSpec((rows, ATTN_WIDTH), lambda s: (s, _COL["q"] // ATTN_WIDTH)),
            pl.BlockSpec((rows, KV_WIDTH), lambda s: (s, _COL["k"] // KV_WIDTH)),
            pl.BlockSpec((rows, KV_WIDTH), lambda s: (s, _COL["v"] // KV_WIDTH)),
            half_block(zcol), half_block(zcol + 1),
            cache_spec, cache_spec,
        ],
        out_specs=[pl.BlockSpec((rows, ATTN_WIDTH), lambda s: (s, 0)), cache_spec, cache_spec],
        out_shape=[
            jax.ShapeDtypeStruct((nseq * DEC_SEQ, ATTN_WIDTH), f32),
            jax.ShapeDtypeStruct(cache_k.shape, f32),
            jax.ShapeDtypeStruct(cache_v.shape, f32),
        ],
        compiler_params=_params(("parallel",)),
        name="attn_sample",
    )(_sample_penalty(), sink_col, p, p, p, p, p, cache_k, cache_v)


def _expand_heads(v):
    low = lax.broadcasted_iota(jnp.int32, (v.shape[0], LANES), 1) < SSM_HEAD_DIM
    tiles = [jnp.where(low, v[:, r:r + 1], v[:, r + 1:r + 2]) for r in range(0, HEADS_PER_GROUP, 2)]
    return jnp.concatenate(tiles, axis=1)


def _split3(v):
    hi = v.astype(bf16)
    rest = v - hi.astype(f32)
    mid = rest.astype(bf16)
    return hi, mid, (rest - mid.astype(f32)).astype(bf16)


def _masked_sums(mask, v_c, v_r):
    m16 = jnp.where(mask, 1.0, 0.0).astype(bf16)
    out_c = sum(jnp.dot(m16, piece, preferred_element_type=f32) for piece in _split3(v_c))
    out_r = sum(lax.dot_general(piece, m16, (((1,), (1,)), ((), ())), preferred_element_type=f32)
                for piece in _split3(v_r))
    return out_c, out_r


def _select_rows(select, v):
    s16 = jnp.where(select, 1.0, 0.0).astype(bf16)
    return sum(jnp.dot(s16, piece, preferred_element_type=f32) for piece in _split3(v))


_ROW_BIAS, _ROW_D, _ROW_GAIN, _ROW_ALOG, _ROW_ALOG_COL = CONV_W, CONV_W + 1, CONV_W + 2, CONV_W + 3, CONV_W + 4
_CONST_ROWS = _ROW_ALOG_COL + HEADS_PER_GROUP
_CONST_WIDTH = GROUP_WIDTH + 2 * D_STATE
_X_COLS, _B_COLS, _C_COLS = (slice(0, GROUP_WIDTH), slice(GROUP_WIDTH, GROUP_WIDTH + D_STATE),
                             slice(GROUP_WIDTH + D_STATE, _CONST_WIDTH))


def _ssd_constants(conv_w, conv_b, a_log, d_skip, ssm_norm):
    grouped = lambda t, width: t.reshape(t.shape[0], SSM_GROUPS, width).transpose(1, 0, 2)
    padded = lambda t: jnp.pad(t, ((0, 0), (0, 0), (0, _CONST_WIDTH - t.shape[2])))
    taps = jnp.concatenate([conv_w, conv_b[None]], axis=0)
    top = jnp.concatenate([grouped(taps[:, :SSM_WIDTH], GROUP_WIDTH),
                           grouped(taps[:, SSM_WIDTH:SSM_WIDTH + BC_WIDTH], D_STATE),
                           grouped(taps[:, SSM_WIDTH + BC_WIDTH:], D_STATE)], axis=2)
    al = a_log.reshape(SSM_GROUPS, HEADS_PER_GROUP)
    return jnp.concatenate([top, padded(grouped(jnp.repeat(d_skip, SSM_HEAD_DIM)[None], GROUP_WIDTH)),
                            padded(grouped(ssm_norm[None], GROUP_WIDTH)),
                            padded(al[:, None, :]), padded(al[:, :, None])], axis=1)


def _intra_chunk(cb, causal, acum_c, acum_r, dt_r, xc):
    lane = lax.broadcasted_iota(jnp.int32, (CHUNK, LANES), 1)
    pieces = []
    for pair in range(HEADS_PER_GROUP // 2):
        x2 = xc[:, pair * LANES:(pair + 1) * LANES]
        acc = jnp.zeros((CHUNK, LANES), f32)
        for half in range(2):
            r = 2 * pair + half
            keep = (lane < SSM_HEAD_DIM) if half == 0 else (lane >= SSM_HEAD_DIM)
            decay = jnp.exp(jnp.where(causal, acum_c[:, r:r + 1] - acum_r[r:r + 1, :], -jnp.inf))
            acc += jnp.dot((cb * decay * dt_r[r:r + 1, :]).astype(bf16), jnp.where(keep, x2, 0.0).astype(bf16),
                           preferred_element_type=f32)
        pieces.append(acc)
    return jnp.concatenate(pieces, axis=1)


def _gated_norm(y, z, gain):
    u = y * _silu(z)
    ms = jnp.mean(u * u, axis=-1, keepdims=True)
    return u * lax.rsqrt(ms + NORM_EPS) * gain


def _conv_silu(cur, shifted, cst_ref, cols):
    y = cst_ref[_ROW_BIAS:_ROW_BIAS + 1, cols] + cst_ref[CONV_W - 1:CONV_W, cols] * cur
    for k in range(1, CONV_W):
        y = y + cst_ref[CONV_W - 1 - k:CONV_W - k, cols] * shifted[k - 1]
    return _silu(y)


def _decay_sums(mask, dt_c, dt_r, cst_ref):
    a_c = -jnp.exp(cst_ref[_ROW_ALOG:_ROW_ALOG + 1, 0:HEADS_PER_GROUP])
    a_r = -jnp.exp(cst_ref[_ROW_ALOG_COL:_ROW_ALOG_COL + HEADS_PER_GROUP, 0:1])
    return _masked_sums(mask, dt_c * a_c, dt_r * a_r)


_SSM_PARTS = SSM_WIDTH // BC_WIDTH


def _ssd_prompt_kernel(*refs):
    x_refs, refs = refs[:_SSM_PARTS], refs[_SSM_PARTS:]
    z_refs, refs = refs[:_SSM_PARTS], refs[_SSM_PARTS:]
    b_ref, c_ref, dtc_ref, dtr_ref, cst_ref, y_ref, st_ref, xpad, bpad, cpad, state = refs
    groups_per_part = SSM_GROUPS // _SSM_PARTS
    c = pl.program_id(1)
    tail = 8

    @pl.when(c == 0)
    def _():
        state[...] = jnp.zeros_like(state)
        for pad in (xpad, bpad, cpad):
            pad[:tail, :] = jnp.zeros((tail, pad.shape[1]), f32)

    @pl.when(c > 0)
    def _():
        for pad in (xpad, bpad, cpad):
            pad[:tail, :] = pad[CHUNK:CHUNK + tail, :]

    for i, part_ref in enumerate(x_refs):
        xpad[tail:, i * BC_WIDTH:(i + 1) * BC_WIDTH] = part_ref[...]
    bpad[tail:, :] = b_ref[...]
    cpad[tail:, :] = c_ref[...]
    row = lax.broadcasted_iota(jnp.int32, (CHUNK, CHUNK), 0)
    col = lax.broadcasted_iota(jnp.int32, (CHUNK, CHUNK), 1)
    causal = row >= col

    for g in range(SSM_GROUPS):
        cst = cst_ref.at[g]
        xs = slice(g * GROUP_WIDTH, (g + 1) * GROUP_WIDTH)
        ns = slice(g * D_STATE, (g + 1) * D_STATE)
        conv = []
        for pad, lanes, cols in ((xpad, xs, _X_COLS), (bpad, ns, _B_COLS), (cpad, ns, _C_COLS)):
            shifted = [pad[tail - k:tail - k + CHUNK, lanes] for k in range(1, CONV_W)]
            conv.append(_conv_silu(pad[tail:, lanes], shifted, cst, cols))
        xc, bm, cm = conv
        z = z_refs[g // groups_per_part][:, (g % groups_per_part) * GROUP_WIDTH:
                                         (g % groups_per_part + 1) * GROUP_WIDTH]

        dt_c, dt_r = dtc_ref[g], dtr_ref[g]
        acum_c, acum_r = _decay_sums(causal, dt_c, dt_r, cst)
        cb = lax.dot_general(cm.astype(bf16), bm.astype(bf16), (((1,), (1,)), ((), ())),
                             preferred_element_type=f32)
        y = _intra_chunk(cb, causal, acum_c, acum_r, dt_r, xc)

        s_in = state[g]
        decay_in = _expand_heads(jnp.exp(acum_c))
        y += decay_in * jnp.dot(cm.astype(bf16), s_in.astype(bf16), preferred_element_type=f32)
        weight_out = _expand_heads(jnp.exp(acum_c[CHUNK - 1:CHUNK, :] - acum_c) * dt_c)
        state[g] = decay_in[CHUNK - 1:CHUNK, :] * s_in + jnp.dot(
            bm.T.astype(bf16), (xc * weight_out).astype(bf16), preferred_element_type=f32)

        y += cst[_ROW_D:_ROW_D + 1, _X_COLS] * xc
        y_ref[:, xs] = _gated_norm(y, z, cst[_ROW_GAIN:_ROW_GAIN + 1, _X_COLS]).astype(y_ref.dtype)

    @pl.when(c == pl.num_programs(1) - 1)
    def _():
        for g in range(SSM_GROUPS):
            st_ref[g * GROUP_WIDTH:(g + 1) * GROUP_WIDTH, :] = state[g].T


def _ssd_prompt(p, dt_c, dt_r, consts, batch):
    nc = SEQ // CHUNK
    rb = lambda b, c: b * nc + c
    part = lambda name, i=0: pl.BlockSpec((CHUNK, BC_WIDTH), lambda b, c: (rb(b, c), _COL[name] // BC_WIDTH + i))
    return pl.pallas_call(
        _ssd_prompt_kernel,
        grid=(batch, nc),
        in_specs=[
            *[part("x", i) for i in range(_SSM_PARTS)],
            *[part("z_s", i) for i in range(_SSM_PARTS)],
            part("B"), part("C"),
            pl.BlockSpec((SSM_GROUPS, CHUNK, HEADS_PER_GROUP), lambda b, c: (0, rb(b, c), 0)),
            pl.BlockSpec((SSM_GROUPS, HEADS_PER_GROUP, CHUNK), lambda b, c: (0, 0, rb(b, c))),
            pl.BlockSpec((SSM_GROUPS, _CONST_ROWS, _CONST_WIDTH), lambda b, c: (0, 0, 0)),
        ],
        out_specs=[
            pl.BlockSpec((CHUNK, SSM_WIDTH), lambda b, c: (rb(b, c), 0)),
            pl.BlockSpec((None, SSM_WIDTH, D_STATE), lambda b, c: (b, 0, 0)),
        ],
        out_shape=[
            jax.ShapeDtypeStruct((batch * SEQ, SSM_WIDTH), bf16),
            jax.ShapeDtypeStruct((batch, SSM_WIDTH, D_STATE), f32),
        ],
        scratch_shapes=[
            pltpu.VMEM((CHUNK + 8, SSM_WIDTH), f32),
            pltpu.VMEM((CHUNK + 8, BC_WIDTH), f32),
            pltpu.VMEM((CHUNK + 8, BC_WIDTH), f32),
            pltpu.VMEM((SSM_GROUPS, D_STATE, GROUP_WIDTH), f32),
        ],
        compiler_params=_params(("parallel", "arbitrary")),
        name="ssd_prompt",
    )(*[p] * (2 * _SSM_PARTS + 2), dt_c, dt_r, consts)


def _ssd_sample_kernel(x_ref, b_ref, c_ref, z_ref, dtc_ref, dtr_ref, cst_ref,
                       cx_ref, cbuf_ref, cc_ref, s0_ref, y_ref, s1_ref, nx_ref, nb_ref, nc_ref):
    row1 = lax.broadcasted_iota(jnp.int32, (CHUNK, 1), 0)
    keep_rows = CONV_W - 1
    r = lax.broadcasted_iota(jnp.int32, (CHUNK, CHUNK), 0)
    c = lax.broadcasted_iota(jnp.int32, (CHUNK, CHUNK), 1)
    scatter = (c == _mod(r, DEC_SEQ) * SEQS_PER_STEP + _div(r, DEC_SEQ)) & (_mod(r, DEC_SEQ) < keep_rows)
    gather = ((c == _mod(r, SEQS_PER_STEP) * DEC_SEQ + DEC_SEQ - keep_rows + _div(r, SEQS_PER_STEP))
              & (r < keep_rows * SEQS_PER_STEP))
    conv = []
    for raw_ref, buf_ref, new_ref, cols in ((x_ref, cx_ref, nx_ref, _X_COLS), (b_ref, cbuf_ref, nb_ref, _B_COLS),
                                            (c_ref, cc_ref, nc_ref, _C_COLS)):
        cur = raw_ref[...]
        width = cur.shape[1]
        state_rows = jnp.concatenate([buf_ref[j] for j in range(keep_rows)]
                                     + [jnp.zeros((CHUNK - keep_rows * SEQS_PER_STEP, width), f32)], axis=0)
        cached = _select_rows(scatter, state_rows)
        moved = _select_rows(gather, cur)
        for j in range(keep_rows):
            new_ref[j] = moved[j * SEQS_PER_STEP:(j + 1) * SEQS_PER_STEP]
        shifted = []
        for k in range(1, CONV_W):
            from_cache = pltpu.roll(cached, (k - (CONV_W - 1)) % CHUNK, 0)
            shifted.append(jnp.where(_mod(row1, DEC_SEQ) >= k, pltpu.roll(cur, k, 0), from_cache))
        conv.append(_conv_silu(cur, shifted, cst_ref, cols))
    xc, bm, cm = conv

    dt_c, dt_r = dtc_ref[...], dtr_ref[...]
    row = lax.broadcasted_iota(jnp.int32, (CHUNK, CHUNK), 0)
    col = lax.broadcasted_iota(jnp.int32, (CHUNK, CHUNK), 1)
    same_seq = _div(row, DEC_SEQ) == _div(col, DEC_SEQ)
    causal = same_seq & (row >= col)
    acum_c, acum_r = _decay_sums(causal, dt_c, dt_r, cst_ref)
    cb = lax.dot_general(cm.astype(bf16), bm.astype(bf16), (((1,), (1,)), ((), ())),
                         preferred_element_type=f32)
    y = _intra_chunk(cb, causal, acum_c, acum_r, dt_r, xc)

    to_end, _ = _decay_sums(same_seq & (row < col), dt_c, dt_r, cst_ref)
    decay_in = _expand_heads(jnp.exp(acum_c))
    xw = (xc * _expand_heads(jnp.exp(to_end) * dt_c)).T.astype(bf16)
    keep = jnp.exp(acum_r)
    seq_of_row = _div(row1, DEC_SEQ)
    inter = jnp.zeros((CHUNK, GROUP_WIDTH), f32)
    for n in range(SEQS_PER_STEP):
        mine = seq_of_row == n
        s0 = s0_ref[n]
        inter += lax.dot_general(jnp.where(mine, cm, 0.0).astype(bf16), s0.astype(bf16),
                                 (((1,), (1,)), ((), ())), preferred_element_type=f32)
        update = jnp.dot(xw, jnp.where(mine, bm, 0.0).astype(bf16), preferred_element_type=f32)
        last_lane = (n + 1) * DEC_SEQ - 1
        for r in range(HEADS_PER_GROUP):
            rows = slice(r * SSM_HEAD_DIM, (r + 1) * SSM_HEAD_DIM)
            s1_ref[n, rows, :] = keep[r:r + 1, last_lane:last_lane + 1] * s0[rows] + update[rows]
    y += decay_in * inter + cst_ref[_ROW_D:_ROW_D + 1, _X_COLS] * xc
    y_ref[...] = _gated_norm(y, z_ref[...], cst_ref[_ROW_GAIN:_ROW_GAIN + 1, _X_COLS]).astype(y_ref.dtype)


def _ssd_sample(p, dt_c, dt_r, consts, conv_state, s0):
    nseq = s0.shape[0]
    nsb = nseq // SEQS_PER_STEP
    xcol, zcol = _COL["x"] // GROUP_WIDTH, _COL["z_s"] // GROUP_WIDTH
    bcol, ccol = _COL["B"] // D_STATE, _COL["C"] // D_STATE
    conv_block = lambda width, first: pl.BlockSpec(
        (CONV_W - 1, SEQS_PER_STEP, width), lambda s, g: (0, s, first + g))
    new_conv = lambda width: jax.ShapeDtypeStruct((CONV_W - 1, nseq, SSM_GROUPS * width), f32)
    state_spec = pl.BlockSpec((SEQS_PER_STEP, GROUP_WIDTH, D_STATE), lambda s, g: (s, g, 0))
    return pl.pallas_call(
        _ssd_sample_kernel,
        grid=(nsb, SSM_GROUPS),
        in_specs=[
            pl.BlockSpec((CHUNK, GROUP_WIDTH), lambda s, g: (s, xcol + g)),
            pl.BlockSpec((CHUNK, D_STATE), lambda s, g: (s, bcol + g)),
            pl.BlockSpec((CHUNK, D_STATE), lambda s, g: (s, ccol + g)),
            pl.BlockSpec((CHUNK, GROUP_WIDTH), lambda s, g: (s, zcol + g)),
            pl.BlockSpec((None, CHUNK, HEADS_PER_GROUP), lambda s, g: (g, s, 0)),
            pl.BlockSpec((None, HEADS_PER_GROUP, CHUNK), lambda s, g: (g, 0, s)),
            pl.BlockSpec((None, _CONST_ROWS, _CONST_WIDTH), lambda s, g: (g, 0, 0)),
            conv_block(GROUP_WIDTH, 0),
            conv_block(D_STATE, SSM_WIDTH // D_STATE),
            conv_block(D_STATE, (SSM_WIDTH + BC_WIDTH) // D_STATE),
            state_spec,
        ],
        out_specs=[pl.BlockSpec((CHUNK, GROUP_WIDTH), lambda s, g: (s, g)), state_spec,
                   conv_block(GROUP_WIDTH, 0), conv_block(D_STATE, 0), conv_block(D_STATE, 0)],
        out_shape=[
            jax.ShapeDtypeStruct((nseq * DEC_SEQ, SSM_WIDTH), bf16),
            jax.ShapeDtypeStruct(s0.shape, f32),
            new_conv(GROUP_WIDTH), new_conv(D_STATE), new_conv(D_STATE),
        ],
        compiler_params=_params(("parallel", "parallel")),
        name="ssd_sample",
    )(p, p, p, p, dt_c, dt_r, consts, conv_state, conv_state, conv_state, s0)


def _merge_kernel(a_ref, s_ref, ga_ref, gs_ref, wa_ref, ws_ref, o_ref):
    ya = jnp.dot(a_ref[...].astype(bf16), wa_ref[...], preferred_element_type=f32)
    ys = jnp.dot(s_ref[...], ws_ref[...], preferred_element_type=f32)
    o_ref[...] = (jax.nn.sigmoid(ga_ref[...]) * ya + jax.nn.sigmoid(gs_ref[...]) * ys).astype(o_ref.dtype)


def _merge(attn, ssm, p, wa, ws, tm=1024, tn=512):
    m = attn.shape[0]
    return pl.pallas_call(
        _merge_kernel,
        grid=(m // tm, D_MODEL // tn),
        in_specs=[
            pl.BlockSpec((tm, ATTN_WIDTH), lambda i, j: (i, 0)),
            pl.BlockSpec((tm, SSM_WIDTH), lambda i, j: (i, 0)),
            pl.BlockSpec((tm, tn), lambda i, j: (i, _COL["g_a"] // tn + j)),
            pl.BlockSpec((tm, tn), lambda i, j: (i, _COL["g_s"] // tn + j)),
            pl.BlockSpec((ATTN_WIDTH, tn), lambda i, j: (0, j)),
            pl.BlockSpec((SSM_WIDTH, tn), lambda i, j: (0, j)),
        ],
        out_specs=pl.BlockSpec((tm, tn), lambda i, j: (i, j)),
        out_shape=jax.ShapeDtypeStruct((m, D_MODEL), bf16),
        compiler_params=_params(("parallel", "arbitrary")),
        name="merge",
    )(attn, ssm, p, p, wa, ws)


def _out_kernel(m_ref, wo_ref, x_ref, g_ref, o_ref):
    y = jnp.dot(m_ref[...], wo_ref[...], preferred_element_type=f32)
    ms = jnp.mean(y * y, axis=-1, keepdims=True)
    o_ref[...] = x_ref[...] + y * lax.rsqrt(ms + NORM_EPS) * g_ref[...]


def _out(merged, wo, x, g, tm=512):
    m = x.shape[0]
    return pl.pallas_call(
        _out_kernel,
        grid=(m // tm,),
        in_specs=[
            pl.BlockSpec((tm, D_MODEL), lambda i: (i, 0)),
            pl.BlockSpec((D_MODEL, D_MODEL), lambda i: (0, 0)),
            pl.BlockSpec((tm, D_MODEL), lambda i: (i, 0)),
            pl.BlockSpec((1, D_MODEL), lambda i: (0, 0)),
        ],
        out_specs=pl.BlockSpec((tm, D_MODEL), lambda i: (i, 0)),
        out_shape=jax.ShapeDtypeStruct((m, D_MODEL), f32),
        compiler_params=_params(("parallel",)),
        name="outproj",
    )(merged, wo, x, g)


def _group_layouts(dt, rows):
    d = dt[:, :SSM_HEADS].reshape(rows, SSM_GROUPS, HEADS_PER_GROUP)
    return d.transpose(1, 0, 2), d.transpose(1, 2, 0)


def _layer(xp, xs, cache_k, cache_v, state_ssm, state_conv, norm_pre, w_in, conv_w, conv_b, dt_bias,
           a_log, d_skip, ssm_norm, attn_sinks, w_attn_br, w_ssm_br, w_out, norm_post):
    batch, nseq = xp.shape[0], xs.shape[0]
    mp, ms = batch * SEQ, nseq * DEC_SEQ
    w_t = w_in.T
    w_main = _cast_weights(w_t)
    w_dt = jnp.pad(w_t[_SRC["dt"]:_SRC["dt"] + SSM_HEADS], ((0, LANES - SSM_HEADS), (0, 0))).astype(bf16)
    dtb = jnp.pad(dt_bias, (0, LANES - SSM_HEADS)).reshape(1, LANES)
    g_pre = norm_pre.reshape(1, D_MODEL)

    ssd_consts = _ssd_constants(conv_w, conv_b, a_log, d_skip, ssm_norm)
    wa, ws, wo = w_attn_br.astype(bf16), w_ssm_br.astype(bf16), w_out.astype(bf16)
    g_post = norm_post.reshape(1, D_MODEL)

    xp2, xs2 = xp.reshape(mp, D_MODEL), xs.reshape(ms, D_MODEL)
    pp, dtp = _inproj(xp2, g_pre, w_main, w_dt, dtb)
    ps, dts = _inproj(xs2, g_pre, w_main, w_dt, dtb)

    attn_p = _attn_prompt(pp, attn_sinks, batch)
    ssm_p, st_p = _ssd_prompt(pp, *_group_layouts(dtp, mp), ssd_consts, batch)
    yp = _out(_merge(attn_p, ssm_p, pp, wa, ws), wo, xp2, g_post)
    pp3 = pp.reshape(batch, SEQ, P_WIDTH)
    k_p = pp3[:, SEQ - WINDOW:, _COL["k"]:_COL["k"] + KV_WIDTH]
    v_p = pp3[:, SEQ - WINDOW:, _COL["v"]:_COL["v"] + KV_WIDTH]
    conv_p = pp3[:, SEQ - (CONV_W - 1):, _COL["x"]:_COL["x"] + CONV_DIM]

    to_pairs = lambda t: t.transpose(0, 2, 3, 1).reshape(nseq, N_KV_HEADS // _HALF, LANES, WINDOW)
    from_pairs = lambda t: t.reshape(nseq, N_KV_HEADS, HEAD_DIM, WINDOW).transpose(0, 3, 1, 2)
    attn_s, k_s, v_s = _attn_sample(ps, attn_sinks, to_pairs(cache_k), to_pairs(cache_v))
    k_s, v_s = from_pairs(k_s), from_pairs(v_s)
    ssm_s, st_s, *new_conv = _ssd_sample(ps, *_group_layouts(dts, ms), ssd_consts, state_conv.transpose(1, 0, 2),
                                         state_ssm.reshape(nseq, SSM_WIDTH, D_STATE))
    ys = _out(_merge(attn_s, ssm_s, ps, wa, ws), wo, xs2, g_post)
    conv_s = jnp.concatenate(new_conv, axis=2).transpose(1, 0, 2)

    kv = lambda t, n: t.reshape(1, n, WINDOW, N_KV_HEADS, HEAD_DIM)
    st = lambda t, n: t.reshape(1, n, SSM_HEADS, SSM_HEAD_DIM, D_STATE)
    return (yp.reshape(xp.shape), ys.reshape(xs.shape), kv(k_p, batch), kv(v_p, batch), st(st_p, batch),
            conv_p[None], kv(k_s, nseq), kv(v_s, nseq), st(st_s, nseq), conv_s[None])


def kernel(x_prompt, x_sample, cache_k, cache_v, state_ssm, state_conv, norm_pre, w_in, conv_w, conv_b,
           dt_bias, a_log, d_skip, ssm_norm, attn_sinks, w_attn_br, w_ssm_br, w_out, norm_post):
    assert w_in.shape[0] == 1, "single-layer trunk"
    return _layer(x_prompt, x_sample, cache_k[0], cache_v[0], state_ssm[0], state_conv[0], norm_pre[0],
                  w_in[0], conv_w[0], conv_b[0], dt_bias[0], a_log[0], d_skip[0], ssm_norm[0],
                  attn_sinks[0], w_attn_br[0], w_ssm_br[0], w_out[0], norm_post[0])
```

```python
import jax
import jax.numpy as jnp
from jax import lax
from jax.experimental import pallas as pl
from jax.experimental.pallas import tpu as pltpu

f32 = jnp.float32
bf16 = jnp.bfloat16

D_MODEL = 2048
SEQ = 4096
DEC_SEQ = 8
N_HEADS = 32
N_KV_HEADS = 8
HEAD_DIM = 64
Q_PER_KV = N_HEADS // N_KV_HEADS
ATTN_WIDTH = N_HEADS * HEAD_DIM
KV_WIDTH = N_KV_HEADS * HEAD_DIM
WINDOW = 128
SSM_WIDTH = 2 * D_MODEL
SSM_HEAD_DIM = 64
SSM_HEADS = SSM_WIDTH // SSM_HEAD_DIM
SSM_GROUPS = 8
HEADS_PER_GROUP = SSM_HEADS // SSM_GROUPS
GROUP_WIDTH = HEADS_PER_GROUP * SSM_HEAD_DIM
D_STATE = 128
CONV_W = 4
BC_WIDTH = SSM_GROUPS * D_STATE
CONV_DIM = SSM_WIDTH + 2 * BC_WIDTH
CHUNK = 128
NORM_EPS = 1e-6

_SRC = dict(q=0, k=2048, v=2560, z_a=3072, xbc=5120, z_s=11264, dt=15360, g_a=15424, g_s=17472)
_COL = dict(q=0, k=2048, v=2560, z_a=3072, x=5120, B=9216, C=10240, z_s=11264, g_a=15360, g_s=17408)
P_WIDTH = 19456
HALF_ATTN = ATTN_WIDTH // 2
LANES = 128
SEQS_PER_STEP = 16

_VMEM_LIMIT = 56 * 1024 * 1024


def _params(sem):
    return pltpu.CompilerParams(dimension_semantics=sem, vmem_limit_bytes=_VMEM_LIMIT)


def _silu(v):
    half = 0.5 * v
    return half + half * jnp.tanh(half)


def _div(v, n):
    assert n & (n - 1) == 0
    return v >> (n.bit_length() - 1)


def _mod(v, n):
    assert n & (n - 1) == 0
    return v & (n - 1)


_NORM_ROWS = 128


_NT = (((1,), (1,)), ((), ()))


def _prenorm(x_ref, g_ref, wdt_ref, dtb_ref, h_ref, dt_ref):
    def norm_rows(i, carry):
        rows = pl.ds(pl.multiple_of(i * _NORM_ROWS, _NORM_ROWS), _NORM_ROWS)
        x = x_ref[rows, :]
        ms = jnp.mean(x * x, axis=-1, keepdims=True)
        h = (x * lax.rsqrt(ms + NORM_EPS) * g_ref[...]).astype(bf16)
        h_ref[rows, :] = h
        v = lax.dot_general(h, wdt_ref[...], _NT, preferred_element_type=f32) + dtb_ref[...]
        dt_ref[rows, :] = jnp.maximum(v, 0.0) + jnp.log1p(jnp.exp(-jnp.abs(v)))
        return carry

    lax.fori_loop(0, x_ref.shape[0] // _NORM_ROWS, norm_rows, 0)


def _inproj_kernel(x_ref, g_ref, w_ref, wdt_ref, dtb_ref, p_ref, dt_ref, h_ref):
    @pl.when(pl.program_id(1) == 0)
    def _():
        _prenorm(x_ref, g_ref, wdt_ref, dtb_ref, h_ref, dt_ref)

    p_ref[...] = lax.dot_general(h_ref[...], w_ref[...], _NT, preferred_element_type=f32)


def _inproj_casting_kernel(x_ref, g_ref, w32_ref, wdt_ref, dtb_ref, p_ref, dt_ref, w16_ref, h_ref):
    @pl.when(pl.program_id(1) == 0)
    def _():
        _prenorm(x_ref, g_ref, wdt_ref, dtb_ref, h_ref, dt_ref)

    w16 = w32_ref[...].astype(bf16)
    w16_ref[...] = w16
    p_ref[...] = lax.dot_general(h_ref[...], w16, _NT, preferred_element_type=f32)


def _inproj_casting(x, g, w_t, wdt, dtb, tm=1024, tn=512):
    m = x.shape[0]
    assert m == tm, "one row block, so every weight block is cast exactly once"
    main_blocks = _SRC["dt"] // tn
    skip = _SRC["g_a"] - _SRC["dt"]
    return pl.pallas_call(
        _inproj_casting_kernel,
        grid=(m // tm, P_WIDTH // tn),
        in_specs=[
            pl.BlockSpec((tm, D_MODEL), lambda i, j: (i, 0)),
            pl.BlockSpec((1, D_MODEL), lambda i, j: (0, 0)),
            pl.BlockSpec((pl.Element(tn), pl.Element(D_MODEL)),
                         lambda i, j: (pl.multiple_of(j * tn + jnp.where(j < main_blocks, 0, skip), 8), 0)),
            pl.BlockSpec((LANES, D_MODEL), lambda i, j: (0, 0)),
            pl.BlockSpec((1, LANES), lambda i, j: (0, 0)),
        ],
        out_specs=[
            pl.BlockSpec((tm, tn), lambda i, j: (i, j)),
            pl.BlockSpec((tm, LANES), lambda i, j: (i, 0)),
            pl.BlockSpec((tn, D_MODEL), lambda i, j: (j, 0)),
        ],
        out_shape=[jax.ShapeDtypeStruct((m, P_WIDTH), f32), jax.ShapeDtypeStruct((m, LANES), f32),
                   jax.ShapeDtypeStruct((P_WIDTH, D_MODEL), bf16)],
        scratch_shapes=[pltpu.VMEM((tm, D_MODEL), bf16)],
        compiler_params=_params(("arbitrary", "arbitrary")),
        name="inproj_casting",
    )(x, g, w_t, wdt, dtb)


def _inproj(x, g, w, wdt, dtb, tm=1024, tn=1024):
    m = x.shape[0]
    return pl.pallas_call(
        _inproj_kernel,
        grid=(m // tm, P_WIDTH // tn),
        in_specs=[
            pl.BlockSpec((tm, D_MODEL), lambda i, j: (i, 0)),
            pl.BlockSpec((1, D_MODEL), lambda i, j: (0, 0)),
            pl.BlockSpec((tn, D_MODEL), lambda i, j: (j, 0)),
            pl.BlockSpec((LANES, D_MODEL), lambda i, j: (0, 0)),
            pl.BlockSpec((1, LANES), lambda i, j: (0, 0)),
        ],
        out_specs=[
            pl.BlockSpec((tm, tn), lambda i, j: (i, j)),
            pl.BlockSpec((tm, LANES), lambda i, j: (i, 0)),
        ],
        out_shape=[jax.ShapeDtypeStruct((m, P_WIDTH), f32), jax.ShapeDtypeStruct((m, LANES), f32)],
        scratch_shapes=[pltpu.VMEM((tm, D_MODEL), bf16)],
        compiler_params=_params(("parallel", "arbitrary")),
        name="inproj",
    )(x, g, w, wdt, dtb)


_HALF = LANES // HEAD_DIM
assert _HALF == 2


def _alibi_slopes():
    return jnp.exp2(-8.0 * jnp.arange(1, N_HEADS + 1, dtype=f32) / N_HEADS)


def _prompt_penalty():
    s = jnp.arange(WINDOW)[:, None]
    q = jnp.arange(WINDOW)[None, :]
    dist = jnp.where(s <= q, q - s, WINDOW + q - s).astype(f32)
    pen = (_alibi_slopes()[:, None, None] * dist[None]).reshape(N_KV_HEADS, Q_PER_KV, WINDOW, WINDOW)
    return pen.transpose(0, 2, 1, 3).reshape(N_KV_HEADS, WINDOW, Q_PER_KV * WINDOW)


def _sample_penalty():
    i = jnp.arange(DEC_SEQ)[:, None]
    c = jnp.arange(2 * WINDOW)[None, :]
    dist = WINDOW + i - c
    valid = (dist >= 0) & (dist < WINDOW) & (c < WINDOW + DEC_SEQ)
    pen = _alibi_slopes()[:, None, None] * dist.astype(f32)[None]
    return jnp.where(valid[None], pen, jnp.inf).reshape(N_HEADS * DEC_SEQ, 2 * WINDOW)


_BLOCKS_PER_STEP = 2


def _attn_prompt_kernel(sink_ref, pen_ref, q_ref, kc_ref, kp_ref, vc_ref, vp_ref, za0_ref, za1_ref, a_ref):
    for sub in range(_BLOCKS_PER_STEP):
        rows = pl.ds(sub * WINDOW, WINDOW)
        before = pl.ds((sub - 1) * WINDOW, WINDOW)
        first_of_sequence = (pl.program_id(1) == 0) if sub == 0 else False
        _attn_prompt_block(
            sink_ref, pen_ref, q_ref.at[rows], kc_ref.at[rows], kp_ref if sub == 0 else kc_ref.at[before],
            vc_ref.at[rows], vp_ref if sub == 0 else vc_ref.at[before], za0_ref.at[rows], za1_ref.at[rows],
            a_ref.at[rows], jnp.where(first_of_sequence, -jnp.inf, 0.0))


def _attn_prompt_block(sink_ref, pen_ref, q_ref, kc_ref, kp_ref, vc_ref, vp_ref, za0_ref, za1_ref, a_ref, prev_off):
    cols4 = Q_PER_KV * WINDOW
    key = lax.broadcasted_iota(jnp.int32, (WINDOW, cols4), 0)
    qry = _mod(lax.broadcasted_iota(jnp.int32, (WINDOW, cols4), 1), WINDOW)
    from_cur = key <= qry
    low_half = lax.broadcasted_iota(jnp.int32, (WINDOW, LANES), 1) < HEAD_DIM
    nt = _NT
    kv_tiles = {}
    for j in range(N_KV_HEADS):
        if j % _HALF == 0:
            tile = slice((j // _HALF) * LANES, (j // _HALF + 1) * LANES)
            kv_tiles = dict(k_cur=kc_ref[:, tile].astype(bf16), k_prev=kp_ref[:, tile].astype(bf16),
                            v_cur=vc_ref[:, tile].T.astype(bf16), v_prev=vp_ref[:, tile].T.astype(bf16))
        mine = low_half == (j % _HALF == 0)
        pieces = []
        for h in range(j * Q_PER_KV, (j + 1) * Q_PER_KV):
            piece = q_ref[:, (h // _HALF) * LANES:(h // _HALF + 1) * LANES] * HEAD_DIM ** -0.5
            if h % _HALF != j % _HALF:
                piece = pltpu.roll(piece, HEAD_DIM, 1)
            pieces.append(jnp.where(mine, piece, 0.0))
        q = jnp.concatenate(pieces, axis=0).astype(bf16)
        s_cur = lax.dot_general(kv_tiles["k_cur"], q, nt, preferred_element_type=f32)
        s_prev = lax.dot_general(kv_tiles["k_prev"], q, nt, preferred_element_type=f32)
        t = jnp.where(from_cur, s_cur, s_prev + prev_off) - pen_ref[j]
        sinks = jnp.concatenate([jnp.full((1, WINDOW), sink_ref[j * Q_PER_KV + g], f32)
                                 for g in range(Q_PER_KV)], axis=1)
        m = jnp.maximum(jnp.max(t, axis=0, keepdims=True), sinks)
        p = jnp.exp(t - m)
        inv = 1.0 / (jnp.sum(p, axis=0, keepdims=True) + jnp.exp(sinks - m))
        o = jnp.dot(kv_tiles["v_cur"], jnp.where(from_cur, p, 0.0).astype(bf16), preferred_element_type=f32)
        o += jnp.dot(kv_tiles["v_prev"], jnp.where(from_cur, 0.0, p).astype(bf16), preferred_element_type=f32)
        o = o * inv
        dims = slice((j % _HALF) * HEAD_DIM, (j % _HALF + 1) * HEAD_DIM)
        for pair in range(Q_PER_KV // _HALF):
            g0 = pair * _HALF
            two_heads = jnp.concatenate([o[dims, g * WINDOW:(g + 1) * WINDOW] for g in (g0, g0 + 1)], axis=0)
            first = (j * Q_PER_KV // _HALF + pair) * LANES
            za_ref = (za0_ref, za1_ref)[first // HALF_ATTN]
            za = za_ref[:, first % HALF_ATTN:first % HALF_ATTN + LANES]
            a_ref[:, first:first + LANES] = (two_heads.T * _silu(za)).astype(a_ref.dtype)


def _attn_prompt(p, sinks, batch):
    step_rows = _BLOCKS_PER_STEP * WINDOW
    nb = SEQ // step_rows
    kcol, vcol = _COL["k"] // KV_WIDTH, _COL["v"] // KV_WIDTH
    zcol = _COL["z_a"] // HALF_ATTN
    cur = lambda b, i: b * nb + i
    prev = lambda b, i: (b * nb + i) * _BLOCKS_PER_STEP - jnp.where(i > 0, 1, 0)
    half_block = lambda col: pl.BlockSpec((step_rows, HALF_ATTN), lambda b, i: (cur(b, i), col))
    return pl.pallas_call(
        _attn_prompt_kernel,
        grid=(batch, nb),
        in_specs=[
            pl.BlockSpec(memory_space=pltpu.SMEM),
            pl.BlockSpec((N_KV_HEADS, WINDOW, Q_PER_KV * WINDOW), lambda b, i: (0, 0, 0)),
            pl.BlockSpec((step_rows, ATTN_WIDTH), lambda b, i: (cur(b, i), _COL["q"] // ATTN_WIDTH)),
            pl.BlockSpec((step_rows, KV_WIDTH), lambda b, i: (cur(b, i), kcol)),
            pl.BlockSpec((WINDOW, KV_WIDTH), lambda b, i: (prev(b, i), kcol)),
            pl.BlockSpec((step_rows, KV_WIDTH), lambda b, i: (cur(b, i), vcol)),
            pl.BlockSpec((WINDOW, KV_WIDTH), lambda b, i: (prev(b, i), vcol)),
            half_block(zcol), half_block(zcol + 1),
        ],
        out_specs=pl.BlockSpec((step_rows, ATTN_WIDTH), lambda b, i: (cur(b, i), 0)),
        out_shape=jax.ShapeDtypeStruct((batch * SEQ, ATTN_WIDTH), bf16),
        compiler_params=_params(("parallel", "parallel")),
        name="attn_prompt",
    )(sinks, _prompt_penalty(), p, p, p, p, p, p, p)


_ATTN_SEQS = 8


def _attn_sample_kernel(pen_ref, sink_ref, q_ref, kn_ref, vn_ref, za0_ref, za1_ref, ck_ref, cv_ref,
                        a_ref, ko_ref, vo_ref):
    pad = jnp.zeros((WINDOW - DEC_SEQ, KV_WIDTH), f32)
    low_half = lax.broadcasted_iota(jnp.int32, (DEC_SEQ, LANES), 1) < HEAD_DIM
    is_new = lax.broadcasted_iota(jnp.int32, (LANES, WINDOW), 1) < DEC_SEQ
    pairs = N_KV_HEADS // _HALF

    def to_half(piece, src, dst):
        return piece if src == dst else pltpu.roll(piece, HEAD_DIM, 1)

    def one_sequence(n, carry):
        new = pl.ds(pl.multiple_of(n * DEC_SEQ, DEC_SEQ), DEC_SEQ)
        k_new = jnp.concatenate([kn_ref[new, :], pad], axis=0)
        v_new = jnp.concatenate([vn_ref[new, :], pad], axis=0)
        kt16, vt16 = [], []
        for i in range(pairs):
            tile = slice(i * LANES, (i + 1) * LANES)
            for cache_ref, fresh, out_ref, as16 in ((ck_ref, k_new, ko_ref, kt16), (cv_ref, v_new, vo_ref, vt16)):
                old = cache_ref[n, i]
                as16.append(old.astype(bf16))
                out_ref[n, i] = pltpu.roll(jnp.where(is_new, fresh[:, tile].T, old), WINDOW - DEC_SEQ, 1)
        k_new16, v_new16 = k_new.astype(bf16), v_new.astype(bf16)
        scores = []
        for j in range(N_KV_HEADS):
            pieces = []
            for h in range(j * Q_PER_KV, (j + 1) * Q_PER_KV):
                piece = q_ref[new, (h // _HALF) * LANES:(h // _HALF + 1) * LANES] * HEAD_DIM ** -0.5
                piece = to_half(piece, h % _HALF, j % _HALF)
                pieces.append(jnp.where(low_half == (j % _HALF == 0), piece, 0.0))
            qj = jnp.concatenate(pieces, axis=0).astype(bf16)
            tile = slice((j // _HALF) * LANES, (j // _HALF + 1) * LANES)
            scores.append(jnp.concatenate(
                [jnp.dot(qj, kt16[j // _HALF], preferred_element_type=f32),
                 lax.dot_general(qj, k_new16[:, tile], _NT, preferred_element_type=f32)], axis=1))
        t = jnp.concatenate(scores, axis=0) - pen_ref[...]
        sinks = sink_ref[...]
        m = jnp.maximum(jnp.max(t, axis=-1, keepdims=True), sinks)
        p = jnp.exp(t - m)
        inv = 1.0 / (jnp.sum(p, axis=-1, keepdims=True) + jnp.exp(sinks - m))
        p16 = p.astype(bf16)
        rows_per_kv = Q_PER_KV * DEC_SEQ
        outs = []
        for j in range(N_KV_HEADS):
            rows = slice(j * rows_per_kv, (j + 1) * rows_per_kv)
            tile = slice((j // _HALF) * LANES, (j // _HALF + 1) * LANES)
            oj = lax.dot_general(p16[rows, :WINDOW], vt16[j // _HALF], _NT, preferred_element_type=f32)
            oj += jnp.dot(p16[rows, WINDOW:], v_new16[:, tile], preferred_element_type=f32)
            oj = oj * inv[rows]
            for g in range(Q_PER_KV):
                h = j * Q_PER_KV + g
                outs.append(to_half(oj[g * DEC_SEQ:(g + 1) * DEC_SEQ], j % _HALF, h % _HALF))
        o = jnp.concatenate([jnp.where(low_half, outs[h], outs[h + 1]) for h in range(0, N_HEADS, _HALF)],
                            axis=1)
        za = jnp.concatenate([za0_ref[new, :], za1_ref[new, :]], axis=1)
        a_ref[new, :] = o * _silu(za)
        return carry

    lax.fori_loop(0, _ATTN_SEQS, one_sequence, 0)


def _attn_sample(p, sinks, cache_k, cache_v):
    nseq = cache_k.shape[0]
    rows = _ATTN_SEQS * DEC_SEQ
    cache_spec = pl.BlockSpec((_ATTN_SEQS,) + cache_k.shape[1:], lambda s: (s, 0, 0, 0))
    sink_col = jnp.repeat(sinks.astype(f32), DEC_SEQ).reshape(N_HEADS * DEC_SEQ, 1)
    zcol = _COL["z_a"] // HALF_ATTN
    half_block = lambda col: pl.BlockSpec((rows, HALF_ATTN), lambda s: (s, col))
    return pl.pallas_call(
        _attn_sample_kernel,
        grid=(nseq // _ATTN_SEQS,),
        in_specs=[
            pl.BlockSpec((N_HEADS * DEC_SEQ, 2 * WINDOW), lambda s: (0, 0)),
            pl.BlockSpec((N_HEADS * DEC_SEQ, 1), lambda s: (0, 0)),
            pl.BlockSpec((rows, ATTN_WIDTH), lambda s: (s, _COL["q"] // ATTN_WIDTH)),
            pl.BlockSpec((rows, KV_WIDTH), lambda s: (s, _COL["k"] // KV_WIDTH)),
            pl.BlockSpec((rows, KV_WIDTH), lambda s: (s, _COL["v"] // KV_WIDTH)),
            half_block(zcol), half_block(zcol + 1),
            cache_spec, cache_spec,
        ],
        out_specs=[pl.BlockSpec((rows, ATTN_WIDTH), lambda s: (s, 0)), cache_spec, cache_spec],
        out_shape=[
            jax.ShapeDtypeStruct((nseq * DEC_SEQ, ATTN_WIDTH), f32),
            jax.ShapeDtypeStruct(cache_k.shape, f32),
            jax.ShapeDtypeStruct(cache_v.shape, f32),
        ],
        compiler_params=_params(("parallel",)),
        name="attn_sample",
    )(_sample_penalty(), sink_col, p, p, p, p, p, cache_k, cache_v)


def _expand_heads(v):
    low = lax.broadcasted_iota(jnp.int32, (v.shape[0], LANES), 1) < SSM_HEAD_DIM
    tiles = [jnp.where(low, v[:, r:r + 1], v[:, r + 1:r + 2]) for r in range(0, HEADS_PER_GROUP, 2)]
    return jnp.concatenate(tiles, axis=1)


def _split3(v):
    hi = v.astype(bf16)
    rest = v - hi.astype(f32)
    mid = rest.astype(bf16)
    return hi, mid, (rest - mid.astype(f32)).astype(bf16)


def _masked_sums(mask, v_c, v_r):
    m16 = jnp.where(mask, 1.0, 0.0).astype(bf16)
    out_c = sum(jnp.dot(m16, piece, preferred_element_type=f32) for piece in _split3(v_c))
    out_r = sum(lax.dot_general(piece, m16, (((1,), (1,)), ((), ())), preferred_element_type=f32)
                for piece in _split3(v_r))
    return out_c, out_r


def _select_rows(select, v):
    s16 = jnp.where(select, 1.0, 0.0).astype(bf16)
    return sum(jnp.dot(s16, piece, preferred_element_type=f32) for piece in _split3(v))


_ROW_BIAS, _ROW_D, _ROW_GAIN, _ROW_ALOG, _ROW_ALOG_COL = CONV_W, CONV_W + 1, CONV_W + 2, CONV_W + 3, CONV_W + 4
_CONST_ROWS = _ROW_ALOG_COL + HEADS_PER_GROUP
_CONST_WIDTH = GROUP_WIDTH + 2 * D_STATE
_X_COLS, _B_COLS, _C_COLS = (slice(0, GROUP_WIDTH), slice(GROUP_WIDTH, GROUP_WIDTH + D_STATE),
                             slice(GROUP_WIDTH + D_STATE, _CONST_WIDTH))


def _ssd_constants(conv_w, conv_b, a_log, d_skip, ssm_norm):
    grouped = lambda t, width: t.reshape(t.shape[0], SSM_GROUPS, width).transpose(1, 0, 2)
    padded = lambda t: jnp.pad(t, ((0, 0), (0, 0), (0, _CONST_WIDTH - t.shape[2])))
    taps = jnp.concatenate([conv_w, conv_b[None]], axis=0)
    top = jnp.concatenate([grouped(taps[:, :SSM_WIDTH], GROUP_WIDTH),
                           grouped(taps[:, SSM_WIDTH:SSM_WIDTH + BC_WIDTH], D_STATE),
                           grouped(taps[:, SSM_WIDTH + BC_WIDTH:], D_STATE)], axis=2)
    al = a_log.reshape(SSM_GROUPS, HEADS_PER_GROUP)
    return jnp.concatenate([top, padded(grouped(jnp.repeat(d_skip, SSM_HEAD_DIM)[None], GROUP_WIDTH)),
                            padded(grouped(ssm_norm[None], GROUP_WIDTH)),
                            padded(al[:, None, :]), padded(al[:, :, None])], axis=1)


def _intra_chunk(cb, causal, acum_c, acum_r, dt_r, xc):
    lane = lax.broadcasted_iota(jnp.int32, (CHUNK, LANES), 1)
    pieces = []
    for pair in range(HEADS_PER_GROUP // 2):
        x2 = xc[:, pair * LANES:(pair + 1) * LANES]
        acc = jnp.zeros((CHUNK, LANES), f32)
        for half in range(2):
            r = 2 * pair + half
            keep = (lane < SSM_HEAD_DIM) if half == 0 else (lane >= SSM_HEAD_DIM)
            decay = jnp.exp(jnp.where(causal, acum_c[:, r:r + 1] - acum_r[r:r + 1, :], -jnp.inf))
            acc += jnp.dot((cb * decay * dt_r[r:r + 1, :]).astype(bf16), jnp.where(keep, x2, 0.0).astype(bf16),
                           preferred_element_type=f32)
        pieces.append(acc)
    return jnp.concatenate(pieces, axis=1)


def _gated_norm(y, z, gain):
    u = y * _silu(z)
    ms = jnp.mean(u * u, axis=-1, keepdims=True)
    return u * lax.rsqrt(ms + NORM_EPS) * gain


def _conv_silu(cur, shifted, cst_ref, cols):
    y = cst_ref[_ROW_BIAS:_ROW_BIAS + 1, cols] + cst_ref[CONV_W - 1:CONV_W, cols] * cur
    for k in range(1, CONV_W):
        y = y + cst_ref[CONV_W - 1 - k:CONV_W - k, cols] * shifted[k - 1]
    return _silu(y)


def _decay_sums(mask, dt_c, dt_r, cst_ref):
    a_c = -jnp.exp(cst_ref[_ROW_ALOG:_ROW_ALOG + 1, 0:HEADS_PER_GROUP])
    a_r = -jnp.exp(cst_ref[_ROW_ALOG_COL:_ROW_ALOG_COL + HEADS_PER_GROUP, 0:1])
    return _masked_sums(mask, dt_c * a_c, dt_r * a_r)


_SSM_PARTS = SSM_WIDTH // BC_WIDTH


def _ssd_prompt_kernel(*refs):
    x_refs, refs = refs[:_SSM_PARTS], refs[_SSM_PARTS:]
    z_refs, refs = refs[:_SSM_PARTS], refs[_SSM_PARTS:]
    b_ref, c_ref, dtc_ref, dtr_ref, cst_ref, y_ref, st_ref, xpad, bpad, cpad, state = refs
    groups_per_part = SSM_GROUPS // _SSM_PARTS
    c = pl.program_id(1)
    tail = 8

    @pl.when(c == 0)
    def _():
        state[...] = jnp.zeros_like(state)
        for pad in (xpad, bpad, cpad):
            pad[:tail, :] = jnp.zeros((tail, pad.shape[1]), f32)

    @pl.when(c > 0)
    def _():
        for pad in (xpad, bpad, cpad):
            pad[:tail, :] = pad[CHUNK:CHUNK + tail, :]

    for i, part_ref in enumerate(x_refs):
        xpad[tail:, i * BC_WIDTH:(i + 1) * BC_WIDTH] = part_ref[...]
    bpad[tail:, :] = b_ref[...]
    cpad[tail:, :] = c_ref[...]
    row = lax.broadcasted_iota(jnp.int32, (CHUNK, CHUNK), 0)
    col = lax.broadcasted_iota(jnp.int32, (CHUNK, CHUNK), 1)
    causal = row >= col

    for g in range(SSM_GROUPS):
        cst = cst_ref.at[g]
        xs = slice(g * GROUP_WIDTH, (g + 1) * GROUP_WIDTH)
        ns = slice(g * D_STATE, (g + 1) * D_STATE)
        conv = []
        for pad, lanes, cols in ((xpad, xs, _X_COLS), (bpad, ns, _B_COLS), (cpad, ns, _C_COLS)):
            shifted = [pad[tail - k:tail - k + CHUNK, lanes] for k in range(1, CONV_W)]
            conv.append(_conv_silu(pad[tail:, lanes], shifted, cst, cols))
        xc, bm, cm = conv
        z = z_refs[g // groups_per_part][:, (g % groups_per_part) * GROUP_WIDTH:
                                         (g % groups_per_part + 1) * GROUP_WIDTH]

        dt_c, dt_r = dtc_ref[g], dtr_ref[g]
        acum_c, acum_r = _decay_sums(causal, dt_c, dt_r, cst)
        cb = lax.dot_general(cm.astype(bf16), bm.astype(bf16), (((1,), (1,)), ((), ())),
                             preferred_element_type=f32)
        y = _intra_chunk(cb, causal, acum_c, acum_r, dt_r, xc)

        s_in = state[g]
        decay_in = _expand_heads(jnp.exp(acum_c))
        y += decay_in * jnp.dot(cm.astype(bf16), s_in.astype(bf16), preferred_element_type=f32)
        weight_out = _expand_heads(jnp.exp(acum_c[CHUNK - 1:CHUNK, :] - acum_c) * dt_c)
        state[g] = decay_in[CHUNK - 1:CHUNK, :] * s_in + jnp.dot(
            bm.T.astype(bf16), (xc * weight_out).astype(bf16), preferred_element_type=f32)

        y += cst[_ROW_D:_ROW_D + 1, _X_COLS] * xc
        y_ref[:, xs] = _gated_norm(y, z, cst[_ROW_GAIN:_ROW_GAIN + 1, _X_COLS]).astype(y_ref.dtype)

    @pl.when(c == pl.num_programs(1) - 1)
    def _():
        for g in range(SSM_GROUPS):
            st_ref[g * GROUP_WIDTH:(g + 1) * GROUP_WIDTH, :] = state[g].T


def _ssd_prompt(p, dt_c, dt_r, consts, batch):
    nc = SEQ // CHUNK
    rb = lambda b, c: b * nc + c
    part = lambda name, i=0: pl.BlockSpec((CHUNK, BC_WIDTH), lambda b, c: (rb(b, c), _COL[name] // BC_WIDTH + i))
    return pl.pallas_call(
        _ssd_prompt_kernel,
        grid=(batch, nc),
        in_specs=[
            *[part("x", i) for i in range(_SSM_PARTS)],
            *[part("z_s", i) for i in range(_SSM_PARTS)],
            part("B"), part("C"),
            pl.BlockSpec((SSM_GROUPS, CHUNK, HEADS_PER_GROUP), lambda b, c: (0, rb(b, c), 0)),
            pl.BlockSpec((SSM_GROUPS, HEADS_PER_GROUP, CHUNK), lambda b, c: (0, 0, rb(b, c))),
            pl.BlockSpec((SSM_GROUPS, _CONST_ROWS, _CONST_WIDTH), lambda b, c: (0, 0, 0)),
        ],
        out_specs=[
            pl.BlockSpec((CHUNK, SSM_WIDTH), lambda b, c: (rb(b, c), 0)),
            pl.BlockSpec((None, SSM_WIDTH, D_STATE), lambda b, c: (b, 0, 0)),
        ],
        out_shape=[
            jax.ShapeDtypeStruct((batch * SEQ, SSM_WIDTH), bf16),
            jax.ShapeDtypeStruct((batch, SSM_WIDTH, D_STATE), f32),
        ],
        scratch_shapes=[
            pltpu.VMEM((CHUNK + 8, SSM_WIDTH), f32),
            pltpu.VMEM((CHUNK + 8, BC_WIDTH), f32),
            pltpu.VMEM((CHUNK + 8, BC_WIDTH), f32),
            pltpu.VMEM((SSM_GROUPS, D_STATE, GROUP_WIDTH), f32),
        ],
        compiler_params=_params(("parallel", "arbitrary")),
        name="ssd_prompt",
    )(*[p] * (2 * _SSM_PARTS + 2), dt_c, dt_r, consts)


def _ssd_sample_kernel(x_ref, b_ref, c_ref, z_ref, dtc_ref, dtr_ref, cst_ref,
                       cx_ref, cbuf_ref, cc_ref, s0_ref, y_ref, s1_ref, nx_ref, nb_ref, nc_ref):
    row1 = lax.broadcasted_iota(jnp.int32, (CHUNK, 1), 0)
    keep_rows = CONV_W - 1
    r = lax.broadcasted_iota(jnp.int32, (CHUNK, CHUNK), 0)
    c = lax.broadcasted_iota(jnp.int32, (CHUNK, CHUNK), 1)
    scatter = (c == _mod(r, DEC_SEQ) * SEQS_PER_STEP + _div(r, DEC_SEQ)) & (_mod(r, DEC_SEQ) < keep_rows)
    gather = ((c == _mod(r, SEQS_PER_STEP) * DEC_SEQ + DEC_SEQ - keep_rows + _div(r, SEQS_PER_STEP))
              & (r < keep_rows * SEQS_PER_STEP))
    conv = []
    for raw_ref, buf_ref, new_ref, cols in ((x_ref, cx_ref, nx_ref, _X_COLS), (b_ref, cbuf_ref, nb_ref, _B_COLS),
                                            (c_ref, cc_ref, nc_ref, _C_COLS)):
        cur = raw_ref[...]
        width = cur.shape[1]
        state_rows = jnp.concatenate([buf_ref[j] for j in range(keep_rows)]
                                     + [jnp.zeros((CHUNK - keep_rows * SEQS_PER_STEP, width), f32)], axis=0)
        cached = _select_rows(scatter, state_rows)
        moved = _select_rows(gather, cur)
        for j in range(keep_rows):
            new_ref[j] = moved[j * SEQS_PER_STEP:(j + 1) * SEQS_PER_STEP]
        shifted = []
        for k in range(1, CONV_W):
            from_cache = pltpu.roll(cached, (k - (CONV_W - 1)) % CHUNK, 0)
            shifted.append(jnp.where(_mod(row1, DEC_SEQ) >= k, pltpu.roll(cur, k, 0), from_cache))
        conv.append(_conv_silu(cur, shifted, cst_ref, cols))
    xc, bm, cm = conv

    dt_c, dt_r = dtc_ref[...], dtr_ref[...]
    row = lax.broadcasted_iota(jnp.int32, (CHUNK, CHUNK), 0)
    col = lax.broadcasted_iota(jnp.int32, (CHUNK, CHUNK), 1)
    same_seq = _div(row, DEC_SEQ) == _div(col, DEC_SEQ)
    causal = same_seq & (row >= col)
    acum_c, acum_r = _decay_sums(causal, dt_c, dt_r, cst_ref)
    cb = lax.dot_general(cm.astype(bf16), bm.astype(bf16), (((1,), (1,)), ((), ())),
                         preferred_element_type=f32)
    y = _intra_chunk(cb, causal, acum_c, acum_r, dt_r, xc)

    to_end, _ = _decay_sums(same_seq & (row < col), dt_c, dt_r, cst_ref)
    decay_in = _expand_heads(jnp.exp(acum_c))
    xw = (xc * _expand_heads(jnp.exp(to_end) * dt_c)).T.astype(bf16)
    keep = jnp.exp(acum_r)
    seq_of_row = _div(row1, DEC_SEQ)
    inter = jnp.zeros((CHUNK, GROUP_WIDTH), f32)
    for n in range(SEQS_PER_STEP):
        mine = seq_of_row == n
        s0 = s0_ref[n]
        inter += lax.dot_general(jnp.where(mine, cm, 0.0).astype(bf16), s0.astype(bf16),
                                 (((1,), (1,)), ((), ())), preferred_element_type=f32)
        update = jnp.dot(xw, jnp.where(mine, bm, 0.0).astype(bf16), preferred_element_type=f32)
        last_lane = (n + 1) * DEC_SEQ - 1
        for r in range(HEADS_PER_GROUP):
            rows = slice(r * SSM_HEAD_DIM, (r + 1) * SSM_HEAD_DIM)
            s1_ref[n, rows, :] = keep[r:r + 1, last_lane:last_lane + 1] * s0[rows] + update[rows]
    y += decay_in * inter + cst_ref[_ROW_D:_ROW_D + 1, _X_COLS] * xc
    y_ref[...] = _gated_norm(y, z_ref[...], cst_ref[_ROW_GAIN:_ROW_GAIN + 1, _X_COLS]).astype(y_ref.dtype)


def _ssd_sample(p, dt_c, dt_r, consts, conv_state, s0):
    nseq = s0.shape[0]
    nsb = nseq // SEQS_PER_STEP
    xcol, zcol = _COL["x"] // GROUP_WIDTH, _COL["z_s"] // GROUP_WIDTH
    bcol, ccol = _COL["B"] // D_STATE, _COL["C"] // D_STATE
    conv_block = lambda width, first: pl.BlockSpec(
        (CONV_W - 1, SEQS_PER_STEP, width), lambda s, g: (0, s, first + g))
    new_conv = lambda width: jax.ShapeDtypeStruct((CONV_W - 1, nseq, SSM_GROUPS * width), f32)
    state_spec = pl.BlockSpec((SEQS_PER_STEP, GROUP_WIDTH, D_STATE), lambda s, g: (s, g, 0))
    return pl.pallas_call(
        _ssd_sample_kernel,
        grid=(nsb, SSM_GROUPS),
        in_specs=[
            pl.BlockSpec((CHUNK, GROUP_WIDTH), lambda s, g: (s, xcol + g)),
            pl.BlockSpec((CHUNK, D_STATE), lambda s, g: (s, bcol + g)),
            pl.BlockSpec((CHUNK, D_STATE), lambda s, g: (s, ccol + g)),
            pl.BlockSpec((CHUNK, GROUP_WIDTH), lambda s, g: (s, zcol + g)),
            pl.BlockSpec((None, CHUNK, HEADS_PER_GROUP), lambda s, g: (g, s, 0)),
            pl.BlockSpec((None, HEADS_PER_GROUP, CHUNK), lambda s, g: (g, 0, s)),
            pl.BlockSpec((None, _CONST_ROWS, _CONST_WIDTH), lambda s, g: (g, 0, 0)),
            conv_block(GROUP_WIDTH, 0),
            conv_block(D_STATE, SSM_WIDTH // D_STATE),
            conv_block(D_STATE, (SSM_WIDTH + BC_WIDTH) // D_STATE),
            state_spec,
        ],
        out_specs=[pl.BlockSpec((CHUNK, GROUP_WIDTH), lambda s, g: (s, g)), state_spec,
                   conv_block(GROUP_WIDTH, 0), conv_block(D_STATE, 0), conv_block(D_STATE, 0)],
        out_shape=[
            jax.ShapeDtypeStruct((nseq * DEC_SEQ, SSM_WIDTH), bf16),
            jax.ShapeDtypeStruct(s0.shape, f32),
            new_conv(GROUP_WIDTH), new_conv(D_STATE), new_conv(D_STATE),
        ],
        compiler_params=_params(("parallel", "parallel")),
        name="ssd_sample",
    )(p, p, p, p, dt_c, dt_r, consts, conv_state, conv_state, conv_state, s0)


def _merge_kernel(a_ref, s_ref, ga_ref, gs_ref, wa_ref, ws_ref, o_ref):
    ya = jnp.dot(a_ref[...].astype(bf16), wa_ref[...], preferred_element_type=f32)
    ys = jnp.dot(s_ref[...], ws_ref[...], preferred_element_type=f32)
    o_ref[...] = (jax.nn.sigmoid(ga_ref[...]) * ya + jax.nn.sigmoid(gs_ref[...]) * ys).astype(o_ref.dtype)


def _merge(attn, ssm, p, wa, ws, tm=1024, tn=512):
    m = attn.shape[0]
    return pl.pallas_call(
        _merge_kernel,
        grid=(m // tm, D_MODEL // tn),
        in_specs=[
            pl.BlockSpec((tm, ATTN_WIDTH), lambda i, j: (i, 0)),
            pl.BlockSpec((tm, SSM_WIDTH), lambda i, j: (i, 0)),
            pl.BlockSpec((tm, tn), lambda i, j: (i, _COL["g_a"] // tn + j)),
            pl.BlockSpec((tm, tn), lambda i, j: (i, _COL["g_s"] // tn + j)),
            pl.BlockSpec((ATTN_WIDTH, tn), lambda i, j: (0, j)),
            pl.BlockSpec((SSM_WIDTH, tn), lambda i, j: (0, j)),
        ],
        out_specs=pl.BlockSpec((tm, tn), lambda i, j: (i, j)),
        out_shape=jax.ShapeDtypeStruct((m, D_MODEL), bf16),
        compiler_params=_params(("parallel", "arbitrary")),
        name="merge",
    )(attn, ssm, p, p, wa, ws)


def _out_kernel(m_ref, wo_ref, x_ref, g_ref, o_ref):
    y = jnp.dot(m_ref[...], wo_ref[...], preferred_element_type=f32)
    ms = jnp.mean(y * y, axis=-1, keepdims=True)
    o_ref[...] = x_ref[...] + y * lax.rsqrt(ms + NORM_EPS) * g_ref[...]


def _out(merged, wo, x, g, tm=512):
    m = x.shape[0]
    return pl.pallas_call(
        _out_kernel,
        grid=(m // tm,),
        in_specs=[
            pl.BlockSpec((tm, D_MODEL), lambda i: (i, 0)),
            pl.BlockSpec((D_MODEL, D_MODEL), lambda i: (0, 0)),
            pl.BlockSpec((tm, D_MODEL), lambda i: (i, 0)),
            pl.BlockSpec((1, D_MODEL), lambda i: (0, 0)),
        ],
        out_specs=pl.BlockSpec((tm, D_MODEL), lambda i: (i, 0)),
        out_shape=jax.ShapeDtypeStruct((m, D_MODEL), f32),
        compiler_params=_params(("parallel",)),
        name="outproj",
    )(merged, wo, x, g)


def _group_layouts(dt, rows):
    d = dt[:, :SSM_HEADS].reshape(rows, SSM_GROUPS, HEADS_PER_GROUP)
    return d.transpose(1, 0, 2), d.transpose(1, 2, 0)


def _layer(xp, xs, cache_k, cache_v, state_ssm, state_conv, norm_pre, w_in, conv_w, conv_b, dt_bias,
           a_log, d_skip, ssm_norm, attn_sinks, w_attn_br, w_ssm_br, w_out, norm_post):
    batch, nseq = xp.shape[0], xs.shape[0]
    mp, ms = batch * SEQ, nseq * DEC_SEQ
    w_t = w_in.T
    w_dt = jnp.pad(w_t[_SRC["dt"]:_SRC["dt"] + SSM_HEADS], ((0, LANES - SSM_HEADS), (0, 0))).astype(bf16)
    dtb = jnp.pad(dt_bias, (0, LANES - SSM_HEADS)).reshape(1, LANES)
    g_pre = norm_pre.reshape(1, D_MODEL)

    ssd_consts = _ssd_constants(conv_w, conv_b, a_log, d_skip, ssm_norm)
    wa, ws, wo = w_attn_br.astype(bf16), w_ssm_br.astype(bf16), w_out.astype(bf16)
    g_post = norm_post.reshape(1, D_MODEL)

    xp2, xs2 = xp.reshape(mp, D_MODEL), xs.reshape(ms, D_MODEL)
    ps, dts, w_main = _inproj_casting(xs2, g_pre, w_t, w_dt, dtb)
    pp, dtp = _inproj(xp2, g_pre, w_main, w_dt, dtb)

    attn_p = _attn_prompt(pp, attn_sinks, batch)
    ssm_p, st_p = _ssd_prompt(pp, *_group_layouts(dtp, mp), ssd_consts, batch)
    yp = _out(_merge(attn_p, ssm_p, pp, wa, ws), wo, xp2, g_post)
    pp3 = pp.reshape(batch, SEQ, P_WIDTH)
    k_p = pp3[:, SEQ - WINDOW:, _COL["k"]:_COL["k"] + KV_WIDTH]
    v_p = pp3[:, SEQ - WINDOW:, _COL["v"]:_COL["v"] + KV_WIDTH]
    conv_p = pp3[:, SEQ - (CONV_W - 1):, _COL["x"]:_COL["x"] + CONV_DIM]

    to_pairs = lambda t: t.transpose(0, 2, 3, 1).reshape(nseq, N_KV_HEADS // _HALF, LANES, WINDOW)
    from_pairs = lambda t: t.reshape(nseq, N_KV_HEADS, HEAD_DIM, WINDOW).transpose(0, 3, 1, 2)
    attn_s, k_s, v_s = _attn_sample(ps, attn_sinks, to_pairs(cache_k), to_pairs(cache_v))
    k_s, v_s = from_pairs(k_s), from_pairs(v_s)
    ssm_s, st_s, *new_conv = _ssd_sample(ps, *_group_layouts(dts, ms), ssd_consts, state_conv.transpose(1, 0, 2),
                                         state_ssm.reshape(nseq, SSM_WIDTH, D_STATE))
    ys = _out(_merge(attn_s, ssm_s, ps, wa, ws), wo, xs2, g_post)
    conv_s = jnp.concatenate(new_conv, axis=2).transpose(1, 0, 2)

    kv = lambda t, n: t.reshape(1, n, WINDOW, N_KV_HEADS, HEAD_DIM)
    st = lambda t, n: t.reshape(1, n, SSM_HEADS, SSM_HEAD_DIM, D_STATE)
    return (yp.reshape(xp.shape), ys.reshape(xs.shape), kv(k_p, batch), kv(v_p, batch), st(st_p, batch),
            conv_p[None], kv(k_s, nseq), kv(v_s, nseq), st(st_s, nseq), conv_s[None])


def kernel(x_prompt, x_sample, cache_k, cache_v, state_ssm, state_conv, norm_pre, w_in, conv_w, conv_b,
           dt_bias, a_log, d_skip, ssm_norm, attn_sinks, w_attn_br, w_ssm_br, w_out, norm_post):
    assert w_in.shape[0] == 1, "single-layer trunk"
    return _layer(x_prompt, x_sample, cache_k[0], cache_v[0], state_ssm[0], state_conv[0], norm_pre[0],
                  w_in[0], conv_w[0], conv_b[0], dt_bias[0], a_log[0], d_skip[0], ssm_norm[0],
                  attn_sinks[0], w_attn_br[0], w_ssm_br[0], w_out[0], norm_post[0])
```

```python
import jax
import jax.numpy as jnp
from jax import lax
from jax.experimental import pallas as pl
from jax.experimental.pallas import tpu as pltpu

f32 = jnp.float32
bf16 = jnp.bfloat16

D_MODEL = 2048
SEQ = 4096
DEC_SEQ = 8
N_HEADS = 32
N_KV_HEADS = 8
HEAD_DIM = 64
Q_PER_KV = N_HEADS // N_KV_HEADS
ATTN_WIDTH = N_HEADS * HEAD_DIM
KV_WIDTH = N_KV_HEADS * HEAD_DIM
WINDOW = 128
SSM_WIDTH = 2 * D_MODEL
SSM_HEAD_DIM = 64
SSM_HEADS = SSM_WIDTH // SSM_HEAD_DIM
SSM_GROUPS = 8
HEADS_PER_GROUP = SSM_HEADS // SSM_GROUPS
GROUP_WIDTH = HEADS_PER_GROUP * SSM_HEAD_DIM
D_STATE = 128
CONV_W = 4
BC_WIDTH = SSM_GROUPS * D_STATE
CONV_DIM = SSM_WIDTH + 2 * BC_WIDTH
CHUNK = 128
NORM_EPS = 1e-6

_SRC = dict(q=0, k=2048, v=2560, z_a=3072, xbc=5120, z_s=11264, dt=15360, g_a=15424, g_s=17472)
_COL = dict(q=0, k=2048, v=2560, z_a=3072, x=5120, B=9216, C=10240, z_s=11264, g_a=15360, g_s=17408)
P_WIDTH = 19456
HALF_ATTN = ATTN_WIDTH // 2
LANES = 128
SEQS_PER_STEP = 16

_VMEM_LIMIT = 56 * 1024 * 1024


def _params(sem):
    return pltpu.CompilerParams(dimension_semantics=sem, vmem_limit_bytes=_VMEM_LIMIT)


def _silu(v):
    half = 0.5 * v
    return half + half * jnp.tanh(half)


def _div(v, n):
    assert n & (n - 1) == 0
    return v >> (n.bit_length() - 1)


def _mod(v, n):
    assert n & (n - 1) == 0
    return v & (n - 1)


_NORM_ROWS = 128


_NT = (((1,), (1,)), ((), ()))


def _prenorm(x_ref, g_ref, wdt_ref, dtb_ref, h_ref, dt_ref):
    def norm_rows(i, carry):
        rows = pl.ds(pl.multiple_of(i * _NORM_ROWS, _NORM_ROWS), _NORM_ROWS)
        x = x_ref[rows, :]
        ms = jnp.mean(x * x, axis=-1, keepdims=True)
        h = (x * lax.rsqrt(ms + NORM_EPS) * g_ref[...]).astype(bf16)
        h_ref[rows, :] = h
        v = lax.dot_general(h, wdt_ref[...], _NT, preferred_element_type=f32) + dtb_ref[...]
        dt_ref[rows, :] = jnp.maximum(v, 0.0) + jnp.log1p(jnp.exp(-jnp.abs(v)))
        return carry

    lax.fori_loop(0, x_ref.shape[0] // _NORM_ROWS, norm_rows, 0)


def _inproj_kernel(x_ref, g_ref, w_ref, wdt_ref, dtb_ref, p_ref, dt_ref, h_ref):
    @pl.when(pl.program_id(1) == 0)
    def _():
        _prenorm(x_ref, g_ref, wdt_ref, dtb_ref, h_ref, dt_ref)

    p_ref[...] = lax.dot_general(h_ref[...], w_ref[...], _NT, preferred_element_type=f32)


def _inproj_casting_kernel(x_ref, g_ref, w32_ref, wdt_ref, dtb_ref, p_ref, dt_ref, w16_ref, h_ref):
    @pl.when(pl.program_id(1) == 0)
    def _():
        _prenorm(x_ref, g_ref, wdt_ref, dtb_ref, h_ref, dt_ref)

    w16 = w32_ref[...].astype(bf16)
    w16_ref[...] = w16
    p_ref[...] = lax.dot_general(h_ref[...], w16, _NT, preferred_element_type=f32)


def _inproj_casting(x, g, w_t, wdt, dtb, tm=1024, tn=512):
    m = x.shape[0]
    assert m == tm, "one row block, so every weight block is cast exactly once"
    main_blocks = _SRC["dt"] // tn
    skip = _SRC["g_a"] - _SRC["dt"]
    return pl.pallas_call(
        _inproj_casting_kernel,
        grid=(m // tm, P_WIDTH // tn),
        in_specs=[
            pl.BlockSpec((tm, D_MODEL), lambda i, j: (i, 0)),
            pl.BlockSpec((1, D_MODEL), lambda i, j: (0, 0)),
            pl.BlockSpec((pl.Element(tn), pl.Element(D_MODEL)),
                         lambda i, j: (pl.multiple_of(j * tn + jnp.where(j < main_blocks, 0, skip), 8), 0)),
            pl.BlockSpec((LANES, D_MODEL), lambda i, j: (0, 0)),
            pl.BlockSpec((1, LANES), lambda i, j: (0, 0)),
        ],
        out_specs=[
            pl.BlockSpec((tm, tn), lambda i, j: (i, j)),
            pl.BlockSpec((tm, LANES), lambda i, j: (i, 0)),
            pl.BlockSpec((tn, D_MODEL), lambda i, j: (j, 0)),
        ],
        out_shape=[jax.ShapeDtypeStruct((m, P_WIDTH), f32), jax.ShapeDtypeStruct((m, LANES), f32),
                   jax.ShapeDtypeStruct((P_WIDTH, D_MODEL), bf16)],
        scratch_shapes=[pltpu.VMEM((tm, D_MODEL), bf16)],
        compiler_params=_params(("arbitrary", "arbitrary")),
        name="inproj_casting",
    )(x, g, w_t, wdt, dtb)


def _inproj(x, g, w, wdt, dtb, tm=1024, tn=1024):
    m = x.shape[0]
    return pl.pallas_call(
        _inproj_kernel,
        grid=(m // tm, P_WIDTH // tn),
        in_specs=[
            pl.BlockSpec((tm, D_MODEL), lambda i, j: (i, 0)),
            pl.BlockSpec((1, D_MODEL), lambda i, j: (0, 0)),
            pl.BlockSpec((tn, D_MODEL), lambda i, j: (j, 0)),
            pl.BlockSpec((LANES, D_MODEL), lambda i, j: (0, 0)),
            pl.BlockSpec((1, LANES), lambda i, j: (0, 0)),
        ],
        out_specs=[
            pl.BlockSpec((tm, tn), lambda i, j: (i, j)),
            pl.BlockSpec((tm, LANES), lambda i, j: (i, 0)),
        ],
        out_shape=[jax.ShapeDtypeStruct((m, P_WIDTH), f32), jax.ShapeDtypeStruct((m, LANES), f32)],
        scratch_shapes=[pltpu.VMEM((tm, D_MODEL), bf16)],
        compiler_params=_params(("parallel", "arbitrary")),
        name="inproj",
    )(x, g, w, wdt, dtb)


_HALF = LANES // HEAD_DIM
assert _HALF == 2


def _alibi_slopes():
    return jnp.exp2(-8.0 * jnp.arange(1, N_HEADS + 1, dtype=f32) / N_HEADS)


def _prompt_penalty():
    s = jnp.arange(WINDOW)[:, None]
    q = jnp.arange(WINDOW)[None, :]
    dist = jnp.where(s <= q, q - s, WINDOW + q - s).astype(f32)
    pen = (_alibi_slopes()[:, None, None] * dist[None]).reshape(N_KV_HEADS, Q_PER_KV, WINDOW, WINDOW)
    return pen.transpose(0, 2, 1, 3).reshape(N_KV_HEADS, WINDOW, Q_PER_KV * WINDOW)


def _sample_penalty():
    i = jnp.arange(DEC_SEQ)[:, None]
    c = jnp.arange(2 * WINDOW)[None, :]
    dist = WINDOW + i - c
    valid = (dist >= 0) & (dist < WINDOW) & (c < WINDOW + DEC_SEQ)
    pen = _alibi_slopes()[:, None, None] * dist.astype(f32)[None]
    return jnp.where(valid[None], pen, jnp.inf).reshape(N_HEADS * DEC_SEQ, 2 * WINDOW)


_BLOCKS_PER_STEP = 2


def _attn_prompt_kernel(sink_ref, pen_ref, q_ref, kc_ref, kp_ref, vc_ref, vp_ref, za0_ref, za1_ref, a_ref):
    for sub in range(_BLOCKS_PER_STEP):
        rows = pl.ds(sub * WINDOW, WINDOW)
        before = pl.ds((sub - 1) * WINDOW, WINDOW)
        first_of_sequence = (pl.program_id(1) == 0) if sub == 0 else False
        _attn_prompt_block(
            sink_ref, pen_ref, q_ref.at[rows], kc_ref.at[rows], kp_ref if sub == 0 else kc_ref.at[before],
            vc_ref.at[rows], vp_ref if sub == 0 else vc_ref.at[before], za0_ref.at[rows], za1_ref.at[rows],
            a_ref.at[rows], jnp.where(first_of_sequence, -jnp.inf, 0.0))


def _attn_prompt_block(sink_ref, pen_ref, q_ref, kc_ref, kp_ref, vc_ref, vp_ref, za0_ref, za1_ref, a_ref, prev_off):
    cols4 = Q_PER_KV * WINDOW
    key = lax.broadcasted_iota(jnp.int32, (WINDOW, cols4), 0)
    qry = _mod(lax.broadcasted_iota(jnp.int32, (WINDOW, cols4), 1), WINDOW)
    from_cur = key <= qry
    low_half = lax.broadcasted_iota(jnp.int32, (WINDOW, LANES), 1) < HEAD_DIM
    nt = _NT
    kv_tiles = {}
    for j in range(N_KV_HEADS):
        if j % _HALF == 0:
            tile = slice((j // _HALF) * LANES, (j // _HALF + 1) * LANES)
            kv_tiles = dict(k_cur=kc_ref[:, tile].astype(bf16), k_prev=kp_ref[:, tile].astype(bf16),
                            v_cur=vc_ref[:, tile].T.astype(bf16), v_prev=vp_ref[:, tile].T.astype(bf16))
        mine = low_half == (j % _HALF == 0)
        pieces = []
        for h in range(j * Q_PER_KV, (j + 1) * Q_PER_KV):
            piece = q_ref[:, (h // _HALF) * LANES:(h // _HALF + 1) * LANES] * HEAD_DIM ** -0.5
            if h % _HALF != j % _HALF:
                piece = pltpu.roll(piece, HEAD_DIM, 1)
            pieces.append(jnp.where(mine, piece, 0.0))
        q = jnp.concatenate(pieces, axis=0).astype(bf16)
        s_cur = lax.dot_general(kv_tiles["k_cur"], q, nt, preferred_element_type=f32)
        s_prev = lax.dot_general(kv_tiles["k_prev"], q, nt, preferred_element_type=f32)
        t = jnp.where(from_cur, s_cur, s_prev + prev_off) - pen_ref[j]
        sinks = jnp.concatenate([jnp.full((1, WINDOW), sink_ref[j * Q_PER_KV + g], f32)
                                 for g in range(Q_PER_KV)], axis=1)
        m = jnp.maximum(jnp.max(t, axis=0, keepdims=True), sinks)
        p = jnp.exp(t - m)
        inv = 1.0 / (jnp.sum(p, axis=0, keepdims=True) + jnp.exp(sinks - m))
        o = jnp.dot(kv_tiles["v_cur"], jnp.where(from_cur, p, 0.0).astype(bf16), preferred_element_type=f32)
        o += jnp.dot(kv_tiles["v_prev"], jnp.where(from_cur, 0.0, p).astype(bf16), preferred_element_type=f32)
        o = o * inv
        dims = slice((j % _HALF) * HEAD_DIM, (j % _HALF + 1) * HEAD_DIM)
        for pair in range(Q_PER_KV // _HALF):
            g0 = pair * _HALF
            two_heads = jnp.concatenate([o[dims, g * WINDOW:(g + 1) * WINDOW] for g in (g0, g0 + 1)], axis=0)
            first = (j * Q_PER_KV // _HALF + pair) * LANES
            za_ref = (za0_ref, za1_ref)[first // HALF_ATTN]
            za = za_ref[:, first % HALF_ATTN:first % HALF_ATTN + LANES]
            a_ref[:, first:first + LANES] = (two_heads.T * _silu(za)).astype(a_ref.dtype)


def _attn_prompt(p, sinks, batch):
    step_rows = _BLOCKS_PER_STEP * WINDOW
    nb = SEQ // step_rows
    kcol, vcol = _COL["k"] // KV_WIDTH, _COL["v"] // KV_WIDTH
    zcol = _COL["z_a"] // HALF_ATTN
    cur = lambda b, i: b * nb + i
    prev = lambda b, i: (b * nb + i) * _BLOCKS_PER_STEP - jnp.where(i > 0, 1, 0)
    half_block = lambda col: pl.BlockSpec((step_rows, HALF_ATTN), lambda b, i: (cur(b, i), col))
    return pl.pallas_call(
        _attn_prompt_kernel,
        grid=(batch, nb),
        in_specs=[
            pl.BlockSpec(memory_space=pltpu.SMEM),
            pl.BlockSpec((N_KV_HEADS, WINDOW, Q_PER_KV * WINDOW), lambda b, i: (0, 0, 0)),
            pl.BlockSpec((step_rows, ATTN_WIDTH), lambda b, i: (cur(b, i), _COL["q"] // ATTN_WIDTH)),
            pl.BlockSpec((step_rows, KV_WIDTH), lambda b, i: (cur(b, i), kcol)),
            pl.BlockSpec((WINDOW, KV_WIDTH), lambda b, i: (prev(b, i), kcol)),
            pl.BlockSpec((step_rows, KV_WIDTH), lambda b, i: (cur(b, i), vcol)),
            pl.BlockSpec((WINDOW, KV_WIDTH), lambda b, i: (prev(b, i), vcol)),
            half_block(zcol), half_block(zcol + 1),
        ],
        out_specs=pl.BlockSpec((step_rows, ATTN_WIDTH), lambda b, i: (cur(b, i), 0)),
        out_shape=jax.ShapeDtypeStruct((batch * SEQ, ATTN_WIDTH), bf16),
        compiler_params=_params(("parallel", "parallel")),
        name="attn_prompt",
    )(sinks, _prompt_penalty(), p, p, p, p, p, p, p)


_ATTN_SEQS = 8
_SEQS_TOGETHER = 2


def _attn_sample_kernel(pen_ref, sink_ref, q_ref, kn_ref, vn_ref, za0_ref, za1_ref, ck_ref, cv_ref,
                        a_ref, ko_ref, vo_ref):
    pad = jnp.zeros((WINDOW - DEC_SEQ, KV_WIDTH), f32)
    low_half = lax.broadcasted_iota(jnp.int32, (DEC_SEQ, LANES), 1) < HEAD_DIM
    is_new = lax.broadcasted_iota(jnp.int32, (LANES, WINDOW), 1) < DEC_SEQ
    pairs = N_KV_HEADS // _HALF

    def to_half(piece, src, dst):
        return piece if src == dst else pltpu.roll(piece, HEAD_DIM, 1)

    rows_per_seq = N_HEADS * DEC_SEQ

    def scores_and_cache_update(n):
        new = pl.ds(pl.multiple_of(n * DEC_SEQ, DEC_SEQ), DEC_SEQ)
        k_new = jnp.concatenate([kn_ref[new, :], pad], axis=0)
        v_new = jnp.concatenate([vn_ref[new, :], pad], axis=0)
        kt16, vt16 = [], []
        for i in range(pairs):
            tile = slice(i * LANES, (i + 1) * LANES)
            for cache_ref, fresh, out_ref, as16 in ((ck_ref, k_new, ko_ref, kt16), (cv_ref, v_new, vo_ref, vt16)):
                old = cache_ref[n, i]
                as16.append(old.astype(bf16))
                out_ref[n, i] = pltpu.roll(jnp.where(is_new, fresh[:, tile].T, old), WINDOW - DEC_SEQ, 1)
        k_new16, v_new16 = k_new.astype(bf16), v_new.astype(bf16)
        scores = []
        for j in range(N_KV_HEADS):
            pieces = []
            for h in range(j * Q_PER_KV, (j + 1) * Q_PER_KV):
                piece = q_ref[new, (h // _HALF) * LANES:(h // _HALF + 1) * LANES] * HEAD_DIM ** -0.5
                piece = to_half(piece, h % _HALF, j % _HALF)
                pieces.append(jnp.where(low_half == (j % _HALF == 0), piece, 0.0))
            qj = jnp.concatenate(pieces, axis=0).astype(bf16)
            tile = slice((j // _HALF) * LANES, (j // _HALF + 1) * LANES)
            scores.append(jnp.concatenate(
                [jnp.dot(qj, kt16[j // _HALF], preferred_element_type=f32),
                 lax.dot_general(qj, k_new16[:, tile], _NT, preferred_element_type=f32)], axis=1))
        return jnp.concatenate(scores, axis=0) - pen_ref[...], vt16, v_new16

    def weighted_values(n, p16, inv, vt16, v_new16):
        new = pl.ds(pl.multiple_of(n * DEC_SEQ, DEC_SEQ), DEC_SEQ)
        rows_per_kv = Q_PER_KV * DEC_SEQ
        outs = []
        for j in range(N_KV_HEADS):
            rows = slice(j * rows_per_kv, (j + 1) * rows_per_kv)
            tile = slice((j // _HALF) * LANES, (j // _HALF + 1) * LANES)
            oj = lax.dot_general(p16[rows, :WINDOW], vt16[j // _HALF], _NT, preferred_element_type=f32)
            oj += jnp.dot(p16[rows, WINDOW:], v_new16[:, tile], preferred_element_type=f32)
            oj = oj * inv[rows]
            for g in range(Q_PER_KV):
                h = j * Q_PER_KV + g
                outs.append(to_half(oj[g * DEC_SEQ:(g + 1) * DEC_SEQ], j % _HALF, h % _HALF))
        o = jnp.concatenate([jnp.where(low_half, outs[h], outs[h + 1]) for h in range(0, N_HEADS, _HALF)],
                            axis=1)
        za = jnp.concatenate([za0_ref[new, :], za1_ref[new, :]], axis=1)
        a_ref[new, :] = o * _silu(za)

    def several_sequences(i, carry):
        seqs = [_SEQS_TOGETHER * i + k for k in range(_SEQS_TOGETHER)]
        staged = [scores_and_cache_update(n) for n in seqs]
        t = jnp.concatenate([s[0] for s in staged], axis=0)
        sinks = jnp.concatenate([sink_ref[...]] * len(seqs), axis=0)
        m = jnp.maximum(jnp.max(t, axis=-1, keepdims=True), sinks)
        p = jnp.exp(t - m)
        inv = 1.0 / (jnp.sum(p, axis=-1, keepdims=True) + jnp.exp(sinks - m))
        p16 = p.astype(bf16)
        for k, n in enumerate(seqs):
            rows = slice(k * rows_per_seq, (k + 1) * rows_per_seq)
            weighted_values(n, p16[rows], inv[rows], staged[k][1], staged[k][2])
        return carry

    lax.fori_loop(0, _ATTN_SEQS // _SEQS_TOGETHER, several_sequences, 0)


def _attn_sample(p, sinks, cache_k, cache_v):
    nseq = cache_k.shape[0]
    rows = _ATTN_SEQS * DEC_SEQ
    cache_spec = pl.BlockSpec((_ATTN_SEQS,) + cache_k.shape[1:], lambda s: (s, 0, 0, 0))
    sink_col = jnp.repeat(sinks.astype(f32), DEC_SEQ).reshape(N_HEADS * DEC_SEQ, 1)
    zcol = _COL["z_a"] // HALF_ATTN
    half_block = lambda col: pl.BlockSpec((rows, HALF_ATTN), lambda s: (s, col))
    return pl.pallas_call(
        _attn_sample_kernel,
        grid=(nseq // _ATTN_SEQS,),
        in_specs=[
            pl.BlockSpec((N_HEADS * DEC_SEQ, 2 * WINDOW), lambda s: (0, 0)),
            pl.BlockSpec((N_HEADS * DEC_SEQ, 1), lambda s: (0, 0)),
            pl.BlockSpec((rows, ATTN_WIDTH), lambda s: (s, _COL["q"] // ATTN_WIDTH)),
            pl.BlockSpec((rows, KV_WIDTH), lambda s: (s, _COL["k"] // KV_WIDTH)),
            pl.BlockSpec((rows, KV_WIDTH), lambda s: (s, _COL["v"] // KV_WIDTH)),
            half_block(zcol), half_block(zcol + 1),
            cache_spec, cache_spec,
        ],
        out_specs=[pl.BlockSpec((rows, ATTN_WIDTH), lambda s: (s, 0)), cache_spec, cache_spec],
        out_shape=[
            jax.ShapeDtypeStruct((nseq * DEC_SEQ, ATTN_WIDTH), f32),
            jax.ShapeDtypeStruct(cache_k.shape, f32),
            jax.ShapeDtypeStruct(cache_v.shape, f32),
        ],
        compiler_params=_params(("parallel",)),
        name="attn_sample",
    )(_sample_penalty(), sink_col, p, p, p, p, p, cache_k, cache_v)


def _expand_heads(v):
    low = lax.broadcasted_iota(jnp.int32, (v.shape[0], LANES), 1) < SSM_HEAD_DIM
    tiles = [jnp.where(low, v[:, r:r + 1], v[:, r + 1:r + 2]) for r in range(0, HEADS_PER_GROUP, 2)]
    return jnp.concatenate(tiles, axis=1)


def _split3(v):
    hi = v.astype(bf16)
    rest = v - hi.astype(f32)
    mid = rest.astype(bf16)
    return hi, mid, (rest - mid.astype(f32)).astype(bf16)


def _masked_sums(mask, v_c, v_r):
    m16 = jnp.where(mask, 1.0, 0.0).astype(bf16)
    out_c = sum(jnp.dot(m16, piece, preferred_element_type=f32) for piece in _split3(v_c))
    out_r = sum(lax.dot_general(piece, m16, (((1,), (1,)), ((), ())), preferred_element_type=f32)
                for piece in _split3(v_r))
    return out_c, out_r


def _select_rows(select, v):
    s16 = jnp.where(select, 1.0, 0.0).astype(bf16)
    return sum(jnp.dot(s16, piece, preferred_element_type=f32) for piece in _split3(v))


_ROW_BIAS, _ROW_D, _ROW_GAIN, _ROW_ALOG, _ROW_ALOG_COL = CONV_W, CONV_W + 1, CONV_W + 2, CONV_W + 3, CONV_W + 4
_CONST_ROWS = _ROW_ALOG_COL + HEADS_PER_GROUP
_CONST_WIDTH = GROUP_WIDTH + 2 * D_STATE
_X_COLS, _B_COLS, _C_COLS = (slice(0, GROUP_WIDTH), slice(GROUP_WIDTH, GROUP_WIDTH + D_STATE),
                             slice(GROUP_WIDTH + D_STATE, _CONST_WIDTH))


def _ssd_constants(conv_w, conv_b, a_log, d_skip, ssm_norm):
    grouped = lambda t, width: t.reshape(t.shape[0], SSM_GROUPS, width).transpose(1, 0, 2)
    padded = lambda t: jnp.pad(t, ((0, 0), (0, 0), (0, _CONST_WIDTH - t.shape[2])))
    taps = jnp.concatenate([conv_w, conv_b[None]], axis=0)
    top = jnp.concatenate([grouped(taps[:, :SSM_WIDTH], GROUP_WIDTH),
                           grouped(taps[:, SSM_WIDTH:SSM_WIDTH + BC_WIDTH], D_STATE),
                           grouped(taps[:, SSM_WIDTH + BC_WIDTH:], D_STATE)], axis=2)
    al = a_log.reshape(SSM_GROUPS, HEADS_PER_GROUP)
    return jnp.concatenate([top, padded(grouped(jnp.repeat(d_skip, SSM_HEAD_DIM)[None], GROUP_WIDTH)),
                            padded(grouped(ssm_norm[None], GROUP_WIDTH)),
                            padded(al[:, None, :]), padded(al[:, :, None])], axis=1)


def _intra_chunk(cb, causal, acum_c, acum_r, dt_r, xc):
    lane = lax.broadcasted_iota(jnp.int32, (CHUNK, LANES), 1)
    pieces = []
    for pair in range(HEADS_PER_GROUP // 2):
        x2 = xc[:, pair * LANES:(pair + 1) * LANES]
        acc = jnp.zeros((CHUNK, LANES), f32)
        for half in range(2):
            r = 2 * pair + half
            keep = (lane < SSM_HEAD_DIM) if half == 0 else (lane >= SSM_HEAD_DIM)
            decay = jnp.exp(jnp.where(causal, acum_c[:, r:r + 1] - acum_r[r:r + 1, :], -jnp.inf))
            acc += jnp.dot((cb * decay * dt_r[r:r + 1, :]).astype(bf16), jnp.where(keep, x2, 0.0).astype(bf16),
                           preferred_element_type=f32)
        pieces.append(acc)
    return jnp.concatenate(pieces, axis=1)


def _gated_norm(y, z, gain):
    u = y * _silu(z)
    ms = jnp.mean(u * u, axis=-1, keepdims=True)
    return u * lax.rsqrt(ms + NORM_EPS) * gain


def _conv_silu(cur, shifted, cst_ref, cols):
    y = cst_ref[_ROW_BIAS:_ROW_BIAS + 1, cols] + cst_ref[CONV_W - 1:CONV_W, cols] * cur
    for k in range(1, CONV_W):
        y = y + cst_ref[CONV_W - 1 - k:CONV_W - k, cols] * shifted[k - 1]
    return _silu(y)


def _decay_sums(mask, dt_c, dt_r, cst_ref):
    a_c = -jnp.exp(cst_ref[_ROW_ALOG:_ROW_ALOG + 1, 0:HEADS_PER_GROUP])
    a_r = -jnp.exp(cst_ref[_ROW_ALOG_COL:_ROW_ALOG_COL + HEADS_PER_GROUP, 0:1])
    return _masked_sums(mask, dt_c * a_c, dt_r * a_r)


_SSM_PARTS = SSM_WIDTH // BC_WIDTH


def _ssd_prompt_kernel(*refs):
    x_refs, refs = refs[:_SSM_PARTS], refs[_SSM_PARTS:]
    z_refs, refs = refs[:_SSM_PARTS], refs[_SSM_PARTS:]
    b_ref, c_ref, dtc_ref, dtr_ref, cst_ref, y_ref, st_ref, xpad, bpad, cpad, state = refs
    groups_per_part = SSM_GROUPS // _SSM_PARTS
    c = pl.program_id(1)
    tail = 8

    @pl.when(c == 0)
    def _():
        state[...] = jnp.zeros_like(state)
        for pad in (xpad, bpad, cpad):
            pad[:tail, :] = jnp.zeros((tail, pad.shape[1]), f32)

    @pl.when(c > 0)
    def _():
        for pad in (xpad, bpad, cpad):
            pad[:tail, :] = pad[CHUNK:CHUNK + tail, :]

    for i, part_ref in enumerate(x_refs):
        xpad[tail:, i * BC_WIDTH:(i + 1) * BC_WIDTH] = part_ref[...]
    bpad[tail:, :] = b_ref[...]
    cpad[tail:, :] = c_ref[...]
    row = lax.broadcasted_iota(jnp.int32, (CHUNK, CHUNK), 0)
    col = lax.broadcasted_iota(jnp.int32, (CHUNK, CHUNK), 1)
    causal = row >= col

    for g in range(SSM_GROUPS):
        cst = cst_ref.at[g]
        xs = slice(g * GROUP_WIDTH, (g + 1) * GROUP_WIDTH)
        ns = slice(g * D_STATE, (g + 1) * D_STATE)
        conv = []
        for pad, lanes, cols in ((xpad, xs, _X_COLS), (bpad, ns, _B_COLS), (cpad, ns, _C_COLS)):
            shifted = [pad[tail - k:tail - k + CHUNK, lanes] for k in range(1, CONV_W)]
            conv.append(_conv_silu(pad[tail:, lanes], shifted, cst, cols))
        xc, bm, cm = conv
        z = z_refs[g // groups_per_part][:, (g % groups_per_part) * GROUP_WIDTH:
                                         (g % groups_per_part + 1) * GROUP_WIDTH]

        dt_c, dt_r = dtc_ref[g], dtr_ref[g]
        acum_c, acum_r = _decay_sums(causal, dt_c, dt_r, cst)
        cb = lax.dot_general(cm.astype(bf16), bm.astype(bf16), (((1,), (1,)), ((), ())),
                             preferred_element_type=f32)
        y = _intra_chunk(cb, causal, acum_c, acum_r, dt_r, xc)

        s_in = state[g]
        decay_in = _expand_heads(jnp.exp(acum_c))
        y += decay_in * jnp.dot(cm.astype(bf16), s_in.astype(bf16), preferred_element_type=f32)
        weight_out = _expand_heads(jnp.exp(acum_c[CHUNK - 1:CHUNK, :] - acum_c) * dt_c)
        state[g] = decay_in[CHUNK - 1:CHUNK, :] * s_in + jnp.dot(
            bm.T.astype(bf16), (xc * weight_out).astype(bf16), preferred_element_type=f32)

        y += cst[_ROW_D:_ROW_D + 1, _X_COLS] * xc
        y_ref[:, xs] = _gated_norm(y, z, cst[_ROW_GAIN:_ROW_GAIN + 1, _X_COLS]).astype(y_ref.dtype)

    @pl.when(c == pl.num_programs(1) - 1)
    def _():
        for g in range(SSM_GROUPS):
            st_ref[g * GROUP_WIDTH:(g + 1) * GROUP_WIDTH, :] = state[g].T


def _ssd_prompt(p, dt_c, dt_r, consts, batch):
    nc = SEQ // CHUNK
    rb = lambda b, c: b * nc + c
    part = lambda name, i=0: pl.BlockSpec((CHUNK, BC_WIDTH), lambda b, c: (rb(b, c), _COL[name] // BC_WIDTH + i))
    return pl.pallas_call(
        _ssd_prompt_kernel,
        grid=(batch, nc),
        in_specs=[
            *[part("x", i) for i in range(_SSM_PARTS)],
            *[part("z_s", i) for i in range(_SSM_PARTS)],
            part("B"), part("C"),
            pl.BlockSpec((SSM_GROUPS, CHUNK, HEADS_PER_GROUP), lambda b, c: (0, rb(b, c), 0)),
            pl.BlockSpec((SSM_GROUPS, HEADS_PER_GROUP, CHUNK), lambda b, c: (0, 0, rb(b, c))),
            pl.BlockSpec((SSM_GROUPS, _CONST_ROWS, _CONST_WIDTH), lambda b, c: (0, 0, 0)),
        ],
        out_specs=[
            pl.BlockSpec((CHUNK, SSM_WIDTH), lambda b, c: (rb(b, c), 0)),
            pl.BlockSpec((None, SSM_WIDTH, D_STATE), lambda b, c: (b, 0, 0)),
        ],
        out_shape=[
            jax.ShapeDtypeStruct((batch * SEQ, SSM_WIDTH), bf16),
            jax.ShapeDtypeStruct((batch, SSM_WIDTH, D_STATE), f32),
        ],
        scratch_shapes=[
            pltpu.VMEM((CHUNK + 8, SSM_WIDTH), f32),
            pltpu.VMEM((CHUNK + 8, BC_WIDTH), f32),
            pltpu.VMEM((CHUNK + 8, BC_WIDTH), f32),
            pltpu.VMEM((SSM_GROUPS, D_STATE, GROUP_WIDTH), f32),
        ],
        compiler_params=_params(("parallel", "arbitrary")),
        name="ssd_prompt",
    )(*[p] * (2 * _SSM_PARTS + 2), dt_c, dt_r, consts)


def _ssd_sample_kernel(x_ref, b_ref, c_ref, z_ref, dtc_ref, dtr_ref, cst_ref,
                       cx_ref, cbuf_ref, cc_ref, s0_ref, y_ref, s1_ref, nx_ref, nb_ref, nc_ref):
    row1 = lax.broadcasted_iota(jnp.int32, (CHUNK, 1), 0)
    keep_rows = CONV_W - 1
    r = lax.broadcasted_iota(jnp.int32, (CHUNK, CHUNK), 0)
    c = lax.broadcasted_iota(jnp.int32, (CHUNK, CHUNK), 1)
    scatter = (c == _mod(r, DEC_SEQ) * SEQS_PER_STEP + _div(r, DEC_SEQ)) & (_mod(r, DEC_SEQ) < keep_rows)
    gather = ((c == _mod(r, SEQS_PER_STEP) * DEC_SEQ + DEC_SEQ - keep_rows + _div(r, SEQS_PER_STEP))
              & (r < keep_rows * SEQS_PER_STEP))
    conv = []
    for raw_ref, buf_ref, new_ref, cols in ((x_ref, cx_ref, nx_ref, _X_COLS), (b_ref, cbuf_ref, nb_ref, _B_COLS),
                                            (c_ref, cc_ref, nc_ref, _C_COLS)):
        cur = raw_ref[...]
        width = cur.shape[1]
        state_rows = jnp.concatenate([buf_ref[j] for j in range(keep_rows)]
                                     + [jnp.zeros((CHUNK - keep_rows * SEQS_PER_STEP, width), f32)], axis=0)
        cached = _select_rows(scatter, state_rows)
        moved = _select_rows(gather, cur)
        for j in range(keep_rows):
            new_ref[j] = moved[j * SEQS_PER_STEP:(j + 1) * SEQS_PER_STEP]
        shifted = []
        for k in range(1, CONV_W):
            from_cache = pltpu.roll(cached, (k - (CONV_W - 1)) % CHUNK, 0)
            shifted.append(jnp.where(_mod(row1, DEC_SEQ) >= k, pltpu.roll(cur, k, 0), from_cache))
        conv.append(_conv_silu(cur, shifted, cst_ref, cols))
    xc, bm, cm = conv

    dt_c, dt_r = dtc_ref[...], dtr_ref[...]
    row = lax.broadcasted_iota(jnp.int32, (CHUNK, CHUNK), 0)
    col = lax.broadcasted_iota(jnp.int32, (CHUNK, CHUNK), 1)
    same_seq = _div(row, DEC_SEQ) == _div(col, DEC_SEQ)
    causal = same_seq & (row >= col)
    acum_c, acum_r = _decay_sums(causal, dt_c, dt_r, cst_ref)
    cb = lax.dot_general(cm.astype(bf16), bm.astype(bf16), (((1,), (1,)), ((), ())),
                         preferred_element_type=f32)
    y = _intra_chunk(cb, causal, acum_c, acum_r, dt_r, xc)

    to_end, _ = _decay_sums(same_seq & (row < col), dt_c, dt_r, cst_ref)
    decay_in = _expand_heads(jnp.exp(acum_c))
    xw = (xc * _expand_heads(jnp.exp(to_end) * dt_c)).T.astype(bf16)
    keep = jnp.exp(acum_r)
    seq_of_row = _div(row1, DEC_SEQ)
    inter = jnp.zeros((CHUNK, GROUP_WIDTH), f32)
    for n in range(SEQS_PER_STEP):
        mine = seq_of_row == n
        s0 = s0_ref[n]
        inter += lax.dot_general(jnp.where(mine, cm, 0.0).astype(bf16), s0.astype(bf16),
                                 (((1,), (1,)), ((), ())), preferred_element_type=f32)
        update = jnp.dot(xw, jnp.where(mine, bm, 0.0).astype(bf16), preferred_element_type=f32)
        last_lane = (n + 1) * DEC_SEQ - 1
        for r in range(HEADS_PER_GROUP):
            rows = slice(r * SSM_HEAD_DIM, (r + 1) * SSM_HEAD_DIM)
            s1_ref[n, rows, :] = keep[r:r + 1, last_lane:last_lane + 1] * s0[rows] + update[rows]
    y += decay_in * inter + cst_ref[_ROW_D:_ROW_D + 1, _X_COLS] * xc
    y_ref[...] = _gated_norm(y, z_ref[...], cst_ref[_ROW_GAIN:_ROW_GAIN + 1, _X_COLS]).astype(y_ref.dtype)


def _ssd_sample(p, dt_c, dt_r, consts, conv_state, s0):
    nseq = s0.shape[0]
    nsb = nseq // SEQS_PER_STEP
    xcol, zcol = _COL["x"] // GROUP_WIDTH, _COL["z_s"] // GROUP_WIDTH
    bcol, ccol = _COL["B"] // D_STATE, _COL["C"] // D_STATE
    conv_block = lambda width, first: pl.BlockSpec(
        (CONV_W - 1, SEQS_PER_STEP, width), lambda s, g: (0, s, first + g))
    new_conv = lambda width: jax.ShapeDtypeStruct((CONV_W - 1, nseq, SSM_GROUPS * width), f32)
    state_spec = pl.BlockSpec((SEQS_PER_STEP, GROUP_WIDTH, D_STATE), lambda s, g: (s, g, 0))
    return pl.pallas_call(
        _ssd_sample_kernel,
        grid=(nsb, SSM_GROUPS),
        in_specs=[
            pl.BlockSpec((CHUNK, GROUP_WIDTH), lambda s, g: (s, xcol + g)),
            pl.BlockSpec((CHUNK, D_STATE), lambda s, g: (s, bcol + g)),
            pl.BlockSpec((CHUNK, D_STATE), lambda s, g: (s, ccol + g)),
            pl.BlockSpec((CHUNK, GROUP_WIDTH), lambda s, g: (s, zcol + g)),
            pl.BlockSpec((None, CHUNK, HEADS_PER_GROUP), lambda s, g: (g, s, 0)),
            pl.BlockSpec((None, HEADS_PER_GROUP, CHUNK), lambda s, g: (g, 0, s)),
            pl.BlockSpec((None, _CONST_ROWS, _CONST_WIDTH), lambda s, g: (g, 0, 0)),
            conv_block(GROUP_WIDTH, 0),
            conv_block(D_STATE, SSM_WIDTH // D_STATE),
            conv_block(D_STATE, (SSM_WIDTH + BC_WIDTH) // D_STATE),
            state_spec,
        ],
        out_specs=[pl.BlockSpec((CHUNK, GROUP_WIDTH), lambda s, g: (s, g)), state_spec,
                   conv_block(GROUP_WIDTH, 0), conv_block(D_STATE, 0), conv_block(D_STATE, 0)],
        out_shape=[
            jax.ShapeDtypeStruct((nseq * DEC_SEQ, SSM_WIDTH), bf16),
            jax.ShapeDtypeStruct(s0.shape, f32),
            new_conv(GROUP_WIDTH), new_conv(D_STATE), new_conv(D_STATE),
        ],
        compiler_params=_params(("parallel", "parallel")),
        name="ssd_sample",
    )(p, p, p, p, dt_c, dt_r, consts, conv_state, conv_state, conv_state, s0)


def _merge_kernel(a_ref, s_ref, ga_ref, gs_ref, wa_ref, ws_ref, o_ref):
    ya = jnp.dot(a_ref[...].astype(bf16), wa_ref[...], preferred_element_type=f32)
    ys = jnp.dot(s_ref[...], ws_ref[...], preferred_element_type=f32)
    o_ref[...] = (jax.nn.sigmoid(ga_ref[...]) * ya + jax.nn.sigmoid(gs_ref[...]) * ys).astype(o_ref.dtype)


def _merge(attn, ssm, p, wa, ws, tm=1024, tn=512):
    m = attn.shape[0]
    return pl.pallas_call(
        _merge_kernel,
        grid=(m // tm, D_MODEL // tn),
        in_specs=[
            pl.BlockSpec((tm, ATTN_WIDTH), lambda i, j: (i, 0)),
            pl.BlockSpec((tm, SSM_WIDTH), lambda i, j: (i, 0)),
            pl.BlockSpec((tm, tn), lambda i, j: (i, _COL["g_a"] // tn + j)),
            pl.BlockSpec((tm, tn), lambda i, j: (i, _COL["g_s"] // tn + j)),
            pl.BlockSpec((ATTN_WIDTH, tn), lambda i, j: (0, j)),
            pl.BlockSpec((SSM_WIDTH, tn), lambda i, j: (0, j)),
        ],
        out_specs=pl.BlockSpec((tm, tn), lambda i, j: (i, j)),
        out_shape=jax.ShapeDtypeStruct((m, D_MODEL), bf16),
        compiler_params=_params(("parallel", "arbitrary")),
        name="merge",
    )(attn, ssm, p, p, wa, ws)


def _out_kernel(m_ref, wo_ref, x_ref, g_ref, o_ref):
    y = jnp.dot(m_ref[...], wo_ref[...], preferred_element_type=f32)
    ms = jnp.mean(y * y, axis=-1, keepdims=True)
    o_ref[...] = x_ref[...] + y * lax.rsqrt(ms + NORM_EPS) * g_ref[...]


def _out(merged, wo, x, g, tm=512):
    m = x.shape[0]
    return pl.pallas_call(
        _out_kernel,
        grid=(m // tm,),
        in_specs=[
            pl.BlockSpec((tm, D_MODEL), lambda i: (i, 0)),
            pl.BlockSpec((D_MODEL, D_MODEL), lambda i: (0, 0)),
            pl.BlockSpec((tm, D_MODEL), lambda i: (i, 0)),
            pl.BlockSpec((1, D_MODEL), lambda i: (0, 0)),
        ],
        out_specs=pl.BlockSpec((tm, D_MODEL), lambda i: (i, 0)),
        out_shape=jax.ShapeDtypeStruct((m, D_MODEL), f32),
        compiler_params=_params(("parallel",)),
        name="outproj",
    )(merged, wo, x, g)


def _group_layouts(dt, rows):
    d = dt[:, :SSM_HEADS].reshape(rows, SSM_GROUPS, HEADS_PER_GROUP)
    return d.transpose(1, 0, 2), d.transpose(1, 2, 0)


def _layer(xp, xs, cache_k, cache_v, state_ssm, state_conv, norm_pre, w_in, conv_w, conv_b, dt_bias,
           a_log, d_skip, ssm_norm, attn_sinks, w_attn_br, w_ssm_br, w_out, norm_post):
    batch, nseq = xp.shape[0], xs.shape[0]
    mp, ms = batch * SEQ, nseq * DEC_SEQ
    w_t = w_in.T
    w_dt = jnp.pad(w_t[_SRC["dt"]:_SRC["dt"] + SSM_HEADS], ((0, LANES - SSM_HEADS), (0, 0))).astype(bf16)
    dtb = jnp.pad(dt_bias, (0, LANES - SSM_HEADS)).reshape(1, LANES)
    g_pre = norm_pre.reshape(1, D_MODEL)

    ssd_consts = _ssd_constants(conv_w, conv_b, a_log, d_skip, ssm_norm)
    wa, ws, wo = w_attn_br.astype(bf16), w_ssm_br.astype(bf16), w_out.astype(bf16)
    g_post = norm_post.reshape(1, D_MODEL)

    xp2, xs2 = xp.reshape(mp, D_MODEL), xs.reshape(ms, D_MODEL)
    ps, dts, w_main = _inproj_casting(xs2, g_pre, w_t, w_dt, dtb)
    pp, dtp = _inproj(xp2, g_pre, w_main, w_dt, dtb)

    attn_p = _attn_prompt(pp, attn_sinks, batch)
    ssm_p, st_p = _ssd_prompt(pp, *_group_layouts(dtp, mp), ssd_consts, batch)
    yp = _out(_merge(attn_p, ssm_p, pp, wa, ws), wo, xp2, g_post)
    pp3 = pp.reshape(batch, SEQ, P_WIDTH)
    k_p = pp3[:, SEQ - WINDOW:, _COL["k"]:_COL["k"] + KV_WIDTH]
    v_p = pp3[:, SEQ - WINDOW:, _COL["v"]:_COL["v"] + KV_WIDTH]
    conv_p = pp3[:, SEQ - (CONV_W - 1):, _COL["x"]:_COL["x"] + CONV_DIM]

    to_pairs = lambda t: t.transpose(0, 2, 3, 1).reshape(nseq, N_KV_HEADS // _HALF, LANES, WINDOW)
    from_pairs = lambda t: t.reshape(nseq, N_KV_HEADS, HEAD_DIM, WINDOW).transpose(0, 3, 1, 2)
    attn_s, k_s, v_s = _attn_sample(ps, attn_sinks, to_pairs(cache_k), to_pairs(cache_v))
    k_s, v_s = from_pairs(k_s), from_pairs(v_s)
    ssm_s, st_s, *new_conv = _ssd_sample(ps, *_group_layouts(dts, ms), ssd_consts, state_conv.transpose(1, 0, 2),
                                         state_ssm.reshape(nseq, SSM_WIDTH, D_STATE))
    ys = _out(_merge(attn_s, ssm_s, ps, wa, ws), wo, xs2, g_post)
    conv_s = jnp.concatenate(new_conv, axis=2).transpose(1, 0, 2)

    kv = lambda t, n: t.reshape(1, n, WINDOW, N_KV_HEADS, HEAD_DIM)
    st = lambda t, n: t.reshape(1, n, SSM_HEADS, SSM_HEAD_DIM, D_STATE)
    return (yp.reshape(xp.shape), ys.reshape(xs.shape), kv(k_p, batch), kv(v_p, batch), st(st_p, batch),
            conv_p[None], kv(k_s, nseq), kv(v_s, nseq), st(st_s, nseq), conv_s[None])


def kernel(x_prompt, x_sample, cache_k, cache_v, state_ssm, state_conv, norm_pre, w_in, conv_w, conv_b,
           dt_bias, a_log, d_skip, ssm_norm, attn_sinks, w_attn_br, w_ssm_br, w_out, norm_post):
    assert w_in.shape[0] == 1, "single-layer trunk"
    return _layer(x_prompt, x_sample, cache_k[0], cache_v[0], state_ssm[0], state_conv[0], norm_pre[0],
                  w_in[0], conv_w[0], conv_b[0], dt_bias[0], a_log[0], d_skip[0], ssm_norm[0],
                  attn_sinks[0], w_attn_br[0], w_ssm_br[0], w_out[0], norm_post[0])
```

```python
import jax
import jax.numpy as jnp
from jax import lax
from jax.experimental import pallas as pl
from jax.experimental.pallas import tpu as pltpu

f32 = jnp.float32
bf16 = jnp.bfloat16

D_MODEL = 2048
SEQ = 4096
DEC_SEQ = 8
N_HEADS = 32
N_KV_HEADS = 8
HEAD_DIM = 64
Q_PER_KV = N_HEADS // N_KV_HEADS
ATTN_WIDTH = N_HEADS * HEAD_DIM
KV_WIDTH = N_KV_HEADS * HEAD_DIM
WINDOW = 128
SSM_WIDTH = 2 * D_MODEL
SSM_HEAD_DIM = 64
SSM_HEADS = SSM_WIDTH // SSM_HEAD_DIM
SSM_GROUPS = 8
HEADS_PER_GROUP = SSM_HEADS // SSM_GROUPS
GROUP_WIDTH = HEADS_PER_GROUP * SSM_HEAD_DIM
D_STATE = 128
CONV_W = 4
BC_WIDTH = SSM_GROUPS * D_STATE
CONV_DIM = SSM_WIDTH + 2 * BC_WIDTH
CHUNK = 128
NORM_EPS = 1e-6

_SRC = dict(q=0, k=2048, v=2560, z_a=3072, xbc=5120, z_s=11264, dt=15360, g_a=15424, g_s=17472)
_COL = dict(q=0, k=2048, v=2560, z_a=3072, x=5120, B=9216, C=10240, z_s=11264, g_a=15360, g_s=17408)
P_WIDTH = 19456
HALF_ATTN = ATTN_WIDTH // 2
LANES = 128
SEQS_PER_STEP = 16

_VMEM_LIMIT = 56 * 1024 * 1024


def _params(sem):
    return pltpu.CompilerParams(dimension_semantics=sem, vmem_limit_bytes=_VMEM_LIMIT)


def _silu(v):
    half = 0.5 * v
    return half + half * jnp.tanh(half)


def _div(v, n):
    assert n & (n - 1) == 0
    return v >> (n.bit_length() - 1)


def _mod(v, n):
    assert n & (n - 1) == 0
    return v & (n - 1)


_NORM_ROWS = 128


_NT = (((1,), (1,)), ((), ()))


def _prenorm(x_ref, g_ref, wdt_ref, dtb_ref, h_ref, dt_ref):
    def norm_rows(i, carry):
        rows = pl.ds(pl.multiple_of(i * _NORM_ROWS, _NORM_ROWS), _NORM_ROWS)
        x = x_ref[rows, :]
        ms = jnp.mean(x * x, axis=-1, keepdims=True)
        h = (x * lax.rsqrt(ms + NORM_EPS) * g_ref[...]).astype(bf16)
        h_ref[rows, :] = h
        v = lax.dot_general(h, wdt_ref[...], _NT, preferred_element_type=f32) + dtb_ref[...]
        dt_ref[rows, :] = jnp.maximum(v, 0.0) + jnp.log1p(jnp.exp(-jnp.abs(v)))
        return carry

    lax.fori_loop(0, x_ref.shape[0] // _NORM_ROWS, norm_rows, 0)


def _inproj_kernel(x_ref, g_ref, w_ref, wdt_ref, dtb_ref, p_ref, dt_ref, h_ref):
    @pl.when(pl.program_id(1) == 0)
    def _():
        _prenorm(x_ref, g_ref, wdt_ref, dtb_ref, h_ref, dt_ref)

    p_ref[...] = lax.dot_general(h_ref[...], w_ref[...], _NT, preferred_element_type=f32)


def _inproj_casting_kernel(x_ref, g_ref, w32_ref, wdt_ref, dtb_ref, p_ref, dt_ref, w16_ref, h_ref):
    @pl.when(pl.program_id(1) == 0)
    def _():
        _prenorm(x_ref, g_ref, wdt_ref, dtb_ref, h_ref, dt_ref)

    w16 = w32_ref[...].astype(bf16)
    w16_ref[...] = w16
    p_ref[...] = lax.dot_general(h_ref[...], w16, _NT, preferred_element_type=f32)


def _inproj_casting(x, g, w_t, wdt, dtb, tm=1024, tn=512):
    m = x.shape[0]
    assert m == tm, "one row block, so every weight block is cast exactly once"
    main_blocks = _SRC["dt"] // tn
    skip = _SRC["g_a"] - _SRC["dt"]
    return pl.pallas_call(
        _inproj_casting_kernel,
        grid=(m // tm, P_WIDTH // tn),
        in_specs=[
            pl.BlockSpec((tm, D_MODEL), lambda i, j: (i, 0)),
            pl.BlockSpec((1, D_MODEL), lambda i, j: (0, 0)),
            pl.BlockSpec((pl.Element(tn), pl.Element(D_MODEL)),
                         lambda i, j: (pl.multiple_of(j * tn + jnp.where(j < main_blocks, 0, skip), 8), 0)),
            pl.BlockSpec((LANES, D_MODEL), lambda i, j: (0, 0)),
            pl.BlockSpec((1, LANES), lambda i, j: (0, 0)),
        ],
        out_specs=[
            pl.BlockSpec((tm, tn), lambda i, j: (i, j)),
            pl.BlockSpec((tm, LANES), lambda i, j: (i, 0)),
            pl.BlockSpec((tn, D_MODEL), lambda i, j: (j, 0)),
        ],
        out_shape=[jax.ShapeDtypeStruct((m, P_WIDTH), f32), jax.ShapeDtypeStruct((m, LANES), f32),
                   jax.ShapeDtypeStruct((P_WIDTH, D_MODEL), bf16)],
        scratch_shapes=[pltpu.VMEM((tm, D_MODEL), bf16)],
        compiler_params=_params(("arbitrary", "arbitrary")),
        name="inproj_casting",
    )(x, g, w_t, wdt, dtb)


def _inproj(x, g, w, wdt, dtb, tm=1024, tn=1024):
    m = x.shape[0]
    return pl.pallas_call(
        _inproj_kernel,
        grid=(m // tm, P_WIDTH // tn),
        in_specs=[
            pl.BlockSpec((tm, D_MODEL), lambda i, j: (i, 0)),
            pl.BlockSpec((1, D_MODEL), lambda i, j: (0, 0)),
            pl.BlockSpec((tn, D_MODEL), lambda i, j: (j, 0)),
            pl.BlockSpec((LANES, D_MODEL), lambda i, j: (0, 0)),
            pl.BlockSpec((1, LANES), lambda i, j: (0, 0)),
        ],
        out_specs=[
            pl.BlockSpec((tm, tn), lambda i, j: (i, j)),
            pl.BlockSpec((tm, LANES), lambda i, j: (i, 0)),
        ],
        out_shape=[jax.ShapeDtypeStruct((m, P_WIDTH), f32), jax.ShapeDtypeStruct((m, LANES), f32)],
        scratch_shapes=[pltpu.VMEM((tm, D_MODEL), bf16)],
        compiler_params=_params(("parallel", "arbitrary")),
        name="inproj",
    )(x, g, w, wdt, dtb)


_HALF = LANES // HEAD_DIM
assert _HALF == 2


def _alibi_slopes():
    return jnp.exp2(-8.0 * jnp.arange(1, N_HEADS + 1, dtype=f32) / N_HEADS)


def _prompt_penalty():
    s = jnp.arange(WINDOW)[:, None]
    q = jnp.arange(WINDOW)[None, :]
    dist = jnp.where(s <= q, q - s, WINDOW + q - s).astype(f32)
    pen = (_alibi_slopes()[:, None, None] * dist[None]).reshape(N_KV_HEADS, Q_PER_KV, WINDOW, WINDOW)
    return pen.transpose(0, 2, 1, 3).reshape(N_KV_HEADS, WINDOW, Q_PER_KV * WINDOW)


def _sample_penalty():
    i = jnp.arange(DEC_SEQ)[:, None]
    c = jnp.arange(2 * WINDOW)[None, :]
    dist = WINDOW + i - c
    valid = (dist >= 0) & (dist < WINDOW) & (c < WINDOW + DEC_SEQ)
    pen = _alibi_slopes()[:, None, None] * dist.astype(f32)[None]
    return jnp.where(valid[None], pen, jnp.inf).reshape(N_HEADS * DEC_SEQ, 2 * WINDOW)


_BLOCKS_PER_STEP = 4


def _attn_prompt_kernel(sink_ref, pen_ref, q_ref, kc_ref, kp_ref, vc_ref, vp_ref, za0_ref, za1_ref, a_ref):
    for sub in range(_BLOCKS_PER_STEP):
        rows = pl.ds(sub * WINDOW, WINDOW)
        before = pl.ds((sub - 1) * WINDOW, WINDOW)
        first_of_sequence = (pl.program_id(1) == 0) if sub == 0 else False
        _attn_prompt_block(
            sink_ref, pen_ref, q_ref.at[rows], kc_ref.at[rows], kp_ref if sub == 0 else kc_ref.at[before],
            vc_ref.at[rows], vp_ref if sub == 0 else vc_ref.at[before], za0_ref.at[rows], za1_ref.at[rows],
            a_ref.at[rows], jnp.where(first_of_sequence, -jnp.inf, 0.0))


def _attn_prompt_block(sink_ref, pen_ref, q_ref, kc_ref, kp_ref, vc_ref, vp_ref, za0_ref, za1_ref, a_ref, prev_off):
    cols4 = Q_PER_KV * WINDOW
    key = lax.broadcasted_iota(jnp.int32, (WINDOW, cols4), 0)
    qry = _mod(lax.broadcasted_iota(jnp.int32, (WINDOW, cols4), 1), WINDOW)
    from_cur = key <= qry
    low_half = lax.broadcasted_iota(jnp.int32, (WINDOW, LANES), 1) < HEAD_DIM
    nt = _NT
    kv_tiles = {}
    for j in range(N_KV_HEADS):
        if j % _HALF == 0:
            tile = slice((j // _HALF) * LANES, (j // _HALF + 1) * LANES)
            kv_tiles = dict(k_cur=kc_ref[:, tile].astype(bf16), k_prev=kp_ref[:, tile].astype(bf16),
                            v_cur=vc_ref[:, tile].T.astype(bf16), v_prev=vp_ref[:, tile].T.astype(bf16))
        mine = low_half == (j % _HALF == 0)
        pieces = []
        for h in range(j * Q_PER_KV, (j + 1) * Q_PER_KV):
            piece = q_ref[:, (h // _HALF) * LANES:(h // _HALF + 1) * LANES] * HEAD_DIM ** -0.5
            if h % _HALF != j % _HALF:
                piece = pltpu.roll(piece, HEAD_DIM, 1)
            pieces.append(jnp.where(mine, piece, 0.0))
        q = jnp.concatenate(pieces, axis=0).astype(bf16)
        s_cur = lax.dot_general(kv_tiles["k_cur"], q, nt, preferred_element_type=f32)
        s_prev = lax.dot_general(kv_tiles["k_prev"], q, nt, preferred_element_type=f32)
        t = jnp.where(from_cur, s_cur, s_prev + prev_off) - pen_ref[j]
        sinks = jnp.concatenate([jnp.full((1, WINDOW), sink_ref[j * Q_PER_KV + g], f32)
                                 for g in range(Q_PER_KV)], axis=1)
        m = jnp.maximum(jnp.max(t, axis=0, keepdims=True), sinks)
        p = jnp.exp(t - m)
        inv = 1.0 / (jnp.sum(p, axis=0, keepdims=True) + jnp.exp(sinks - m))
        o = jnp.dot(kv_tiles["v_cur"], jnp.where(from_cur, p, 0.0).astype(bf16), preferred_element_type=f32)
        o += jnp.dot(kv_tiles["v_prev"], jnp.where(from_cur, 0.0, p).astype(bf16), preferred_element_type=f32)
        o = o * inv
        dims = slice((j % _HALF) * HEAD_DIM, (j % _HALF + 1) * HEAD_DIM)
        for pair in range(Q_PER_KV // _HALF):
            g0 = pair * _HALF
            two_heads = jnp.concatenate([o[dims, g * WINDOW:(g + 1) * WINDOW] for g in (g0, g0 + 1)], axis=0)
            first = (j * Q_PER_KV // _HALF + pair) * LANES
            za_ref = (za0_ref, za1_ref)[first // HALF_ATTN]
            za = za_ref[:, first % HALF_ATTN:first % HALF_ATTN + LANES]
            a_ref[:, first:first + LANES] = (two_heads.T * _silu(za)).astype(a_ref.dtype)


def _attn_prompt(p, sinks, batch):
    step_rows = _BLOCKS_PER_STEP * WINDOW
    nb = SEQ // step_rows
    kcol, vcol = _COL["k"] // KV_WIDTH, _COL["v"] // KV_WIDTH
    zcol = _COL["z_a"] // HALF_ATTN
    cur = lambda b, i: b * nb + i
    prev = lambda b, i: (b * nb + i) * _BLOCKS_PER_STEP - jnp.where(i > 0, 1, 0)
    half_block = lambda col: pl.BlockSpec((step_rows, HALF_ATTN), lambda b, i: (cur(b, i), col))
    return pl.pallas_call(
        _attn_prompt_kernel,
        grid=(batch, nb),
        in_specs=[
            pl.BlockSpec(memory_space=pltpu.SMEM),
            pl.BlockSpec((N_KV_HEADS, WINDOW, Q_PER_KV * WINDOW), lambda b, i: (0, 0, 0)),
            pl.BlockSpec((step_rows, ATTN_WIDTH), lambda b, i: (cur(b, i), _COL["q"] // ATTN_WIDTH)),
            pl.BlockSpec((step_rows, KV_WIDTH), lambda b, i: (cur(b, i), kcol)),
            pl.BlockSpec((WINDOW, KV_WIDTH), lambda b, i: (prev(b, i), kcol)),
            pl.BlockSpec((step_rows, KV_WIDTH), lambda b, i: (cur(b, i), vcol)),
            pl.BlockSpec((WINDOW, KV_WIDTH), lambda b, i: (prev(b, i), vcol)),
            half_block(zcol), half_block(zcol + 1),
        ],
        out_specs=pl.BlockSpec((step_rows, ATTN_WIDTH), lambda b, i: (cur(b, i), 0)),
        out_shape=jax.ShapeDtypeStruct((batch * SEQ, ATTN_WIDTH), bf16),
        compiler_params=_params(("parallel", "parallel")),
        name="attn_prompt",
    )(sinks, _prompt_penalty(), p, p, p, p, p, p, p)


_ATTN_SEQS = 8
_SEQS_TOGETHER = 2


def _attn_sample_kernel(pen_ref, sink_ref, q_ref, kn_ref, vn_ref, za0_ref, za1_ref, ck_ref, cv_ref,
                        a_ref, ko_ref, vo_ref):
    pad = jnp.zeros((WINDOW - DEC_SEQ, KV_WIDTH), f32)
    low_half = lax.broadcasted_iota(jnp.int32, (DEC_SEQ, LANES), 1) < HEAD_DIM
    is_new = lax.broadcasted_iota(jnp.int32, (LANES, WINDOW), 1) < DEC_SEQ
    pairs = N_KV_HEADS // _HALF

    def to_half(piece, src, dst):
        return piece if src == dst else pltpu.roll(piece, HEAD_DIM, 1)

    rows_per_seq = N_HEADS * DEC_SEQ

    def scores_and_cache_update(n):
        new = pl.ds(pl.multiple_of(n * DEC_SEQ, DEC_SEQ), DEC_SEQ)
        k_new = jnp.concatenate([kn_ref[new, :], pad], axis=0)
        v_new = jnp.concatenate([vn_ref[new, :], pad], axis=0)
        kt16, vt16 = [], []
        for i in range(pairs):
            tile = slice(i * LANES, (i + 1) * LANES)
            for cache_ref, fresh, out_ref, as16 in ((ck_ref, k_new, ko_ref, kt16), (cv_ref, v_new, vo_ref, vt16)):
                old = cache_ref[n, i]
                as16.append(old.astype(bf16))
                out_ref[n, i] = pltpu.roll(jnp.where(is_new, fresh[:, tile].T, old), WINDOW - DEC_SEQ, 1)
        k_new16, v_new16 = k_new.astype(bf16), v_new.astype(bf16)
        scores = []
        for j in range(N_KV_HEADS):
            pieces = []
            for h in range(j * Q_PER_KV, (j + 1) * Q_PER_KV):
                piece = q_ref[new, (h // _HALF) * LANES:(h // _HALF + 1) * LANES] * HEAD_DIM ** -0.5
                piece = to_half(piece, h % _HALF, j % _HALF)
                pieces.append(jnp.where(low_half == (j % _HALF == 0), piece, 0.0))
            qj = jnp.concatenate(pieces, axis=0).astype(bf16)
            tile = slice((j // _HALF) * LANES, (j // _HALF + 1) * LANES)
            scores.append(jnp.concatenate(
                [jnp.dot(qj, kt16[j // _HALF], preferred_element_type=f32),
                 lax.dot_general(qj, k_new16[:, tile], _NT, preferred_element_type=f32)], axis=1))
        return jnp.concatenate(scores, axis=0) - pen_ref[...], vt16, v_new16

    def weighted_values(n, p16, inv, vt16, v_new16):
        new = pl.ds(pl.multiple_of(n * DEC_SEQ, DEC_SEQ), DEC_SEQ)
        rows_per_kv = Q_PER_KV * DEC_SEQ
        outs = []
        for j in range(N_KV_HEADS):
            rows = slice(j * rows_per_kv, (j + 1) * rows_per_kv)
            tile = slice((j // _HALF) * LANES, (j // _HALF + 1) * LANES)
            oj = lax.dot_general(p16[rows, :WINDOW], vt16[j // _HALF], _NT, preferred_element_type=f32)
            oj += jnp.dot(p16[rows, WINDOW:], v_new16[:, tile], preferred_element_type=f32)
            oj = oj * inv[rows]
            for g in range(Q_PER_KV):
                h = j * Q_PER_KV + g
                outs.append(to_half(oj[g * DEC_SEQ:(g + 1) * DEC_SEQ], j % _HALF, h % _HALF))
        o = jnp.concatenate([jnp.where(low_half, outs[h], outs[h + 1]) for h in range(0, N_HEADS, _HALF)],
                            axis=1)
        za = jnp.concatenate([za0_ref[new, :], za1_ref[new, :]], axis=1)
        a_ref[new, :] = o * _silu(za)

    def several_sequences(i, carry):
        seqs = [_SEQS_TOGETHER * i + k for k in range(_SEQS_TOGETHER)]
        staged = [scores_and_cache_update(n) for n in seqs]
        t = jnp.concatenate([s[0] for s in staged], axis=0)
        sinks = jnp.concatenate([sink_ref[...]] * len(seqs), axis=0)
        m = jnp.maximum(jnp.max(t, axis=-1, keepdims=True), sinks)
        p = jnp.exp(t - m)
        inv = 1.0 / (jnp.sum(p, axis=-1, keepdims=True) + jnp.exp(sinks - m))
        p16 = p.astype(bf16)
        for k, n in enumerate(seqs):
            rows = slice(k * rows_per_seq, (k + 1) * rows_per_seq)
            weighted_values(n, p16[rows], inv[rows], staged[k][1], staged[k][2])
        return carry

    lax.fori_loop(0, _ATTN_SEQS // _SEQS_TOGETHER, several_sequences, 0)


def _attn_sample(p, sinks, cache_k, cache_v):
    nseq = cache_k.shape[0]
    rows = _ATTN_SEQS * DEC_SEQ
    cache_spec = pl.BlockSpec((_ATTN_SEQS,) + cache_k.shape[1:], lambda s: (s, 0, 0, 0))
    sink_col = jnp.repeat(sinks.astype(f32), DEC_SEQ).reshape(N_HEADS * DEC_SEQ, 1)
    zcol = _COL["z_a"] // HALF_ATTN
    half_block = lambda col: pl.BlockSpec((rows, HALF_ATTN), lambda s: (s, col))
    return pl.pallas_call(
        _attn_sample_kernel,
        grid=(nseq // _ATTN_SEQS,),
        in_specs=[
            pl.BlockSpec((N_HEADS * DEC_SEQ, 2 * WINDOW), lambda s: (0, 0)),
            pl.BlockSpec((N_HEADS * DEC_SEQ, 1), lambda s: (0, 0)),
            pl.BlockSpec((rows, ATTN_WIDTH), lambda s: (s, _COL["q"] // ATTN_WIDTH)),
            pl.BlockSpec((rows, KV_WIDTH), lambda s: (s, _COL["k"] // KV_WIDTH)),
            pl.BlockSpec((rows, KV_WIDTH), lambda s: (s, _COL["v"] // KV_WIDTH)),
            half_block(zcol), half_block(zcol + 1),
            cache_spec, cache_spec,
        ],
        out_specs=[pl.BlockSpec((rows, ATTN_WIDTH), lambda s: (s, 0)), cache_spec, cache_spec],
        out_shape=[
            jax.ShapeDtypeStruct((nseq * DEC_SEQ, ATTN_WIDTH), f32),
            jax.ShapeDtypeStruct(cache_k.shape, f32),
            jax.ShapeDtypeStruct(cache_v.shape, f32),
        ],
        compiler_params=_params(("parallel",)),
        name="attn_sample",
    )(_sample_penalty(), sink_col, p, p, p, p, p, cache_k, cache_v)


def _expand_heads(v):
    low = lax.broadcasted_iota(jnp.int32, (v.shape[0], LANES), 1) < SSM_HEAD_DIM
    tiles = [jnp.where(low, v[:, r:r + 1], v[:, r + 1:r + 2]) for r in range(0, HEADS_PER_GROUP, 2)]
    return jnp.concatenate(tiles, axis=1)


def _split3(v):
    hi = v.astype(bf16)
    rest = v - hi.astype(f32)
    mid = rest.astype(bf16)
    return hi, mid, (rest - mid.astype(f32)).astype(bf16)


def _masked_sums(mask, v_c, v_r):
    m16 = jnp.where(mask, 1.0, 0.0).astype(bf16)
    out_c = sum(jnp.dot(m16, piece, preferred_element_type=f32) for piece in _split3(v_c))
    out_r = sum(lax.dot_general(piece, m16, (((1,), (1,)), ((), ())), preferred_element_type=f32)
                for piece in _split3(v_r))
    return out_c, out_r


def _select_rows(select, v):
    s16 = jnp.where(select, 1.0, 0.0).astype(bf16)
    return sum(jnp.dot(s16, piece, preferred_element_type=f32) for piece in _split3(v))


_ROW_BIAS, _ROW_D, _ROW_GAIN, _ROW_ALOG, _ROW_ALOG_COL = CONV_W, CONV_W + 1, CONV_W + 2, CONV_W + 3, CONV_W + 4
_CONST_ROWS = _ROW_ALOG_COL + HEADS_PER_GROUP
_CONST_WIDTH = GROUP_WIDTH + 2 * D_STATE
_X_COLS, _B_COLS, _C_COLS = (slice(0, GROUP_WIDTH), slice(GROUP_WIDTH, GROUP_WIDTH + D_STATE),
                             slice(GROUP_WIDTH + D_STATE, _CONST_WIDTH))


def _ssd_constants(conv_w, conv_b, a_log, d_skip, ssm_norm):
    grouped = lambda t, width: t.reshape(t.shape[0], SSM_GROUPS, width).transpose(1, 0, 2)
    padded = lambda t: jnp.pad(t, ((0, 0), (0, 0), (0, _CONST_WIDTH - t.shape[2])))
    taps = jnp.concatenate([conv_w, conv_b[None]], axis=0)
    top = jnp.concatenate([grouped(taps[:, :SSM_WIDTH], GROUP_WIDTH),
                           grouped(taps[:, SSM_WIDTH:SSM_WIDTH + BC_WIDTH], D_STATE),
                           grouped(taps[:, SSM_WIDTH + BC_WIDTH:], D_STATE)], axis=2)
    al = a_log.reshape(SSM_GROUPS, HEADS_PER_GROUP)
    return jnp.concatenate([top, padded(grouped(jnp.repeat(d_skip, SSM_HEAD_DIM)[None], GROUP_WIDTH)),
                            padded(grouped(ssm_norm[None], GROUP_WIDTH)),
                            padded(al[:, None, :]), padded(al[:, :, None])], axis=1)


def _intra_chunk(cb, causal, acum_c, acum_r, dt_r, xc):
    lane = lax.broadcasted_iota(jnp.int32, (CHUNK, LANES), 1)
    pieces = []
    for pair in range(HEADS_PER_GROUP // 2):
        x2 = xc[:, pair * LANES:(pair + 1) * LANES]
        acc = jnp.zeros((CHUNK, LANES), f32)
        for half in range(2):
            r = 2 * pair + half
            keep = (lane < SSM_HEAD_DIM) if half == 0 else (lane >= SSM_HEAD_DIM)
            decay = jnp.exp(jnp.where(causal, acum_c[:, r:r + 1] - acum_r[r:r + 1, :], -jnp.inf))
            acc += jnp.dot((cb * decay * dt_r[r:r + 1, :]).astype(bf16), jnp.where(keep, x2, 0.0).astype(bf16),
                           preferred_element_type=f32)
        pieces.append(acc)
    return jnp.concatenate(pieces, axis=1)


def _gated_norm(y, z, gain):
    u = y * _silu(z)
    ms = jnp.mean(u * u, axis=-1, keepdims=True)
    return u * lax.rsqrt(ms + NORM_EPS) * gain


def _conv_silu(cur, shifted, cst_ref, cols):
    y = cst_ref[_ROW_BIAS:_ROW_BIAS + 1, cols] + cst_ref[CONV_W - 1:CONV_W, cols] * cur
    for k in range(1, CONV_W):
        y = y + cst_ref[CONV_W - 1 - k:CONV_W - k, cols] * shifted[k - 1]
    return _silu(y)


def _decay_sums(mask, dt_c, dt_r, cst_ref):
    a_c = -jnp.exp(cst_ref[_ROW_ALOG:_ROW_ALOG + 1, 0:HEADS_PER_GROUP])
    a_r = -jnp.exp(cst_ref[_ROW_ALOG_COL:_ROW_ALOG_COL + HEADS_PER_GROUP, 0:1])
    return _masked_sums(mask, dt_c * a_c, dt_r * a_r)


_SSM_PARTS = SSM_WIDTH // BC_WIDTH


def _ssd_prompt_kernel(*refs):
    x_refs, refs = refs[:_SSM_PARTS], refs[_SSM_PARTS:]
    z_refs, refs = refs[:_SSM_PARTS], refs[_SSM_PARTS:]
    b_ref, c_ref, dtc_ref, dtr_ref, cst_ref, y_ref, st_ref, xpad, bpad, cpad, state = refs
    groups_per_part = SSM_GROUPS // _SSM_PARTS
    c = pl.program_id(1)
    tail = 8

    @pl.when(c == 0)
    def _():
        state[...] = jnp.zeros_like(state)
        for pad in (xpad, bpad, cpad):
            pad[:tail, :] = jnp.zeros((tail, pad.shape[1]), f32)

    @pl.when(c > 0)
    def _():
        for pad in (xpad, bpad, cpad):
            pad[:tail, :] = pad[CHUNK:CHUNK + tail, :]

    for i, part_ref in enumerate(x_refs):
        xpad[tail:, i * BC_WIDTH:(i + 1) * BC_WIDTH] = part_ref[...]
    bpad[tail:, :] = b_ref[...]
    cpad[tail:, :] = c_ref[...]
    row = lax.broadcasted_iota(jnp.int32, (CHUNK, CHUNK), 0)
    col = lax.broadcasted_iota(jnp.int32, (CHUNK, CHUNK), 1)
    causal = row >= col

    for g in range(SSM_GROUPS):
        cst = cst_ref.at[g]
        xs = slice(g * GROUP_WIDTH, (g + 1) * GROUP_WIDTH)
        ns = slice(g * D_STATE, (g + 1) * D_STATE)
        conv = []
        for pad, lanes, cols in ((xpad, xs, _X_COLS), (bpad, ns, _B_COLS), (cpad, ns, _C_COLS)):
            shifted = [pad[tail - k:tail - k + CHUNK, lanes] for k in range(1, CONV_W)]
            conv.append(_conv_silu(pad[tail:, lanes], shifted, cst, cols))
        xc, bm, cm = conv
        z = z_refs[g // groups_per_part][:, (g % groups_per_part) * GROUP_WIDTH:
                                         (g % groups_per_part + 1) * GROUP_WIDTH]

        dt_c, dt_r = dtc_ref[g], dtr_ref[g]
        acum_c, acum_r = _decay_sums(causal, dt_c, dt_r, cst)
        cb = lax.dot_general(cm.astype(bf16), bm.astype(bf16), (((1,), (1,)), ((), ())),
                             preferred_element_type=f32)
        y = _intra_chunk(cb, causal, acum_c, acum_r, dt_r, xc)

        s_in = state[g]
        decay_in = _expand_heads(jnp.exp(acum_c))
        y += decay_in * jnp.dot(cm.astype(bf16), s_in.astype(bf16), preferred_element_type=f32)
        weight_out = _expand_heads(jnp.exp(acum_c[CHUNK - 1:CHUNK, :] - acum_c) * dt_c)
        state[g] = decay_in[CHUNK - 1:CHUNK, :] * s_in + jnp.dot(
            bm.T.astype(bf16), (xc * weight_out).astype(bf16), preferred_element_type=f32)

        y += cst[_ROW_D:_ROW_D + 1, _X_COLS] * xc
        y_ref[:, xs] = _gated_norm(y, z, cst[_ROW_GAIN:_ROW_GAIN + 1, _X_COLS]).astype(y_ref.dtype)

    @pl.when(c == pl.num_programs(1) - 1)
    def _():
        for g in range(SSM_GROUPS):
            st_ref[g * GROUP_WIDTH:(g + 1) * GROUP_WIDTH, :] = state[g].T


def _ssd_prompt(p, dt_c, dt_r, consts, batch):
    nc = SEQ // CHUNK
    rb = lambda b, c: b * nc + c
    part = lambda name, i=0: pl.BlockSpec((CHUNK, BC_WIDTH), lambda b, c: (rb(b, c), _COL[name] // BC_WIDTH + i))
    return pl.pallas_call(
        _ssd_prompt_kernel,
        grid=(batch, nc),
        in_specs=[
            *[part("x", i) for i in range(_SSM_PARTS)],
            *[part("z_s", i) for i in range(_SSM_PARTS)],
            part("B"), part("C"),
            pl.BlockSpec((SSM_GROUPS, CHUNK, HEADS_PER_GROUP), lambda b, c: (0, rb(b, c), 0)),
            pl.BlockSpec((SSM_GROUPS, HEADS_PER_GROUP, CHUNK), lambda b, c: (0, 0, rb(b, c))),
            pl.BlockSpec((SSM_GROUPS, _CONST_ROWS, _CONST_WIDTH), lambda b, c: (0, 0, 0)),
        ],
        out_specs=[
            pl.BlockSpec((CHUNK, SSM_WIDTH), lambda b, c: (rb(b, c), 0)),
            pl.BlockSpec((None, SSM_WIDTH, D_STATE), lambda b, c: (b, 0, 0)),
        ],
        out_shape=[
            jax.ShapeDtypeStruct((batch * SEQ, SSM_WIDTH), bf16),
            jax.ShapeDtypeStruct((batch, SSM_WIDTH, D_STATE), f32),
        ],
        scratch_shapes=[
            pltpu.VMEM((CHUNK + 8, SSM_WIDTH), f32),
            pltpu.VMEM((CHUNK + 8, BC_WIDTH), f32),
            pltpu.VMEM((CHUNK + 8, BC_WIDTH), f32),
            pltpu.VMEM((SSM_GROUPS, D_STATE, GROUP_WIDTH), f32),
        ],
        compiler_params=_params(("parallel", "arbitrary")),
        name="ssd_prompt",
    )(*[p] * (2 * _SSM_PARTS + 2), dt_c, dt_r, consts)


def _ssd_sample_kernel(x_ref, b_ref, c_ref, z_ref, dtc_ref, dtr_ref, cst_ref,
                       cx_ref, cbuf_ref, cc_ref, s0_ref, y_ref, s1_ref, nx_ref, nb_ref, nc_ref):
    row1 = lax.broadcasted_iota(jnp.int32, (CHUNK, 1), 0)
    keep_rows = CONV_W - 1
    r = lax.broadcasted_iota(jnp.int32, (CHUNK, CHUNK), 0)
    c = lax.broadcasted_iota(jnp.int32, (CHUNK, CHUNK), 1)
    scatter = (c == _mod(r, DEC_SEQ) * SEQS_PER_STEP + _div(r, DEC_SEQ)) & (_mod(r, DEC_SEQ) < keep_rows)
    gather = ((c == _mod(r, SEQS_PER_STEP) * DEC_SEQ + DEC_SEQ - keep_rows + _div(r, SEQS_PER_STEP))
              & (r < keep_rows * SEQS_PER_STEP))
    conv = []
    for raw_ref, buf_ref, new_ref, cols in ((x_ref, cx_ref, nx_ref, _X_COLS), (b_ref, cbuf_ref, nb_ref, _B_COLS),
                                            (c_ref, cc_ref, nc_ref, _C_COLS)):
        cur = raw_ref[...]
        width = cur.shape[1]
        state_rows = jnp.concatenate([buf_ref[j] for j in range(keep_rows)]
                                     + [jnp.zeros((CHUNK - keep_rows * SEQS_PER_STEP, width), f32)], axis=0)
        cached = _select_rows(scatter, state_rows)
        moved = _select_rows(gather, cur)
        for j in range(keep_rows):
            new_ref[j] = moved[j * SEQS_PER_STEP:(j + 1) * SEQS_PER_STEP]
        shifted = []
        for k in range(1, CONV_W):
            from_cache = pltpu.roll(cached, (k - (CONV_W - 1)) % CHUNK, 0)
            shifted.append(jnp.where(_mod(row1, DEC_SEQ) >= k, pltpu.roll(cur, k, 0), from_cache))
        conv.append(_conv_silu(cur, shifted, cst_ref, cols))
    xc, bm, cm = conv

    dt_c, dt_r = dtc_ref[...], dtr_ref[...]
    row = lax.broadcasted_iota(jnp.int32, (CHUNK, CHUNK), 0)
    col = lax.broadcasted_iota(jnp.int32, (CHUNK, CHUNK), 1)
    same_seq = _div(row, DEC_SEQ) == _div(col, DEC_SEQ)
    causal = same_seq & (row >= col)
    acum_c, acum_r = _decay_sums(causal, dt_c, dt_r, cst_ref)
    cb = lax.dot_general(cm.astype(bf16), bm.astype(bf16), (((1,), (1,)), ((), ())),
                         preferred_element_type=f32)
    y = _intra_chunk(cb, causal, acum_c, acum_r, dt_r, xc)

    to_end, _ = _decay_sums(same_seq & (row < col), dt_c, dt_r, cst_ref)
    decay_in = _expand_heads(jnp.exp(acum_c))
    xw = (xc * _expand_heads(jnp.exp(to_end) * dt_c)).T.astype(bf16)
    keep = jnp.exp(acum_r)
    seq_of_row = _div(row1, DEC_SEQ)
    inter = jnp.zeros((CHUNK, GROUP_WIDTH), f32)
    for n in range(SEQS_PER_STEP):
        mine = seq_of_row == n
        s0 = s0_ref[n]
        inter += lax.dot_general(jnp.where(mine, cm, 0.0).astype(bf16), s0.astype(bf16),
                                 (((1,), (1,)), ((), ())), preferred_element_type=f32)
        update = jnp.dot(xw, jnp.where(mine, bm, 0.0).astype(bf16), preferred_element_type=f32)
        last_lane = (n + 1) * DEC_SEQ - 1
        for r in range(HEADS_PER_GROUP):
            rows = slice(r * SSM_HEAD_DIM, (r + 1) * SSM_HEAD_DIM)
            s1_ref[n, rows, :] = keep[r:r + 1, last_lane:last_lane + 1] * s0[rows] + update[rows]
    y += decay_in * inter + cst_ref[_ROW_D:_ROW_D + 1, _X_COLS] * xc
    y_ref[...] = _gated_norm(y, z_ref[...], cst_ref[_ROW_GAIN:_ROW_GAIN + 1, _X_COLS]).astype(y_ref.dtype)


def _ssd_sample(p, dt_c, dt_r, consts, conv_state, s0):
    nseq = s0.shape[0]
    nsb = nseq // SEQS_PER_STEP
    xcol, zcol = _COL["x"] // GROUP_WIDTH, _COL["z_s"] // GROUP_WIDTH
    bcol, ccol = _COL["B"] // D_STATE, _COL["C"] // D_STATE
    conv_block = lambda width, first: pl.BlockSpec(
        (CONV_W - 1, SEQS_PER_STEP, width), lambda s, g: (0, s, first + g))
    new_conv = lambda width: jax.ShapeDtypeStruct((CONV_W - 1, nseq, SSM_GROUPS * width), f32)
    state_spec = pl.BlockSpec((SEQS_PER_STEP, GROUP_WIDTH, D_STATE), lambda s, g: (s, g, 0))
    return pl.pallas_call(
        _ssd_sample_kernel,
        grid=(nsb, SSM_GROUPS),
        in_specs=[
            pl.BlockSpec((CHUNK, GROUP_WIDTH), lambda s, g: (s, xcol + g)),
            pl.BlockSpec((CHUNK, D_STATE), lambda s, g: (s, bcol + g)),
            pl.BlockSpec((CHUNK, D_STATE), lambda s, g: (s, ccol + g)),
            pl.BlockSpec((CHUNK, GROUP_WIDTH), lambda s, g: (s, zcol + g)),
            pl.BlockSpec((None, CHUNK, HEADS_PER_GROUP), lambda s, g: (g, s, 0)),
            pl.BlockSpec((None, HEADS_PER_GROUP, CHUNK), lambda s, g: (g, 0, s)),
            pl.BlockSpec((None, _CONST_ROWS, _CONST_WIDTH), lambda s, g: (g, 0, 0)),
            conv_block(GROUP_WIDTH, 0),
            conv_block(D_STATE, SSM_WIDTH // D_STATE),
            conv_block(D_STATE, (SSM_WIDTH + BC_WIDTH) // D_STATE),
            state_spec,
        ],
        out_specs=[pl.BlockSpec((CHUNK, GROUP_WIDTH), lambda s, g: (s, g)), state_spec,
                   conv_block(GROUP_WIDTH, 0), conv_block(D_STATE, 0), conv_block(D_STATE, 0)],
        out_shape=[
            jax.ShapeDtypeStruct((nseq * DEC_SEQ, SSM_WIDTH), bf16),
            jax.ShapeDtypeStruct(s0.shape, f32),
            new_conv(GROUP_WIDTH), new_conv(D_STATE), new_conv(D_STATE),
        ],
        compiler_params=_params(("parallel", "parallel")),
        name="ssd_sample",
    )(p, p, p, p, dt_c, dt_r, consts, conv_state, conv_state, conv_state, s0)


def _merge_kernel(a_ref, s_ref, ga_ref, gs_ref, wa_ref, ws_ref, o_ref):
    ya = jnp.dot(a_ref[...].astype(bf16), wa_ref[...], preferred_element_type=f32)
    ys = jnp.dot(s_ref[...], ws_ref[...], preferred_element_type=f32)
    o_ref[...] = (jax.nn.sigmoid(ga_ref[...]) * ya + jax.nn.sigmoid(gs_ref[...]) * ys).astype(o_ref.dtype)


def _merge(attn, ssm, p, wa, ws, tm=1024, tn=512):
    m = attn.shape[0]
    return pl.pallas_call(
        _merge_kernel,
        grid=(m // tm, D_MODEL // tn),
        in_specs=[
            pl.BlockSpec((tm, ATTN_WIDTH), lambda i, j: (i, 0)),
            pl.BlockSpec((tm, SSM_WIDTH), lambda i, j: (i, 0)),
            pl.BlockSpec((tm, tn), lambda i, j: (i, _COL["g_a"] // tn + j)),
            pl.BlockSpec((tm, tn), lambda i, j: (i, _COL["g_s"] // tn + j)),
            pl.BlockSpec((ATTN_WIDTH, tn), lambda i, j: (0, j)),
            pl.BlockSpec((SSM_WIDTH, tn), lambda i, j: (0, j)),
        ],
        out_specs=pl.BlockSpec((tm, tn), lambda i, j: (i, j)),
        out_shape=jax.ShapeDtypeStruct((m, D_MODEL), bf16),
        compiler_params=_params(("parallel", "arbitrary")),
        name="merge",
    )(attn, ssm, p, p, wa, ws)


def _out_kernel(m_ref, wo_ref, x_ref, g_ref, o_ref):
    y = jnp.dot(m_ref[...], wo_ref[...], preferred_element_type=f32)
    ms = jnp.mean(y * y, axis=-1, keepdims=True)
    o_ref[...] = x_ref[...] + y * lax.rsqrt(ms + NORM_EPS) * g_ref[...]


def _out(merged, wo, x, g, tm=512):
    m = x.shape[0]
    return pl.pallas_call(
        _out_kernel,
        grid=(m // tm,),
        in_specs=[
            pl.BlockSpec((tm, D_MODEL), lambda i: (i, 0)),
            pl.BlockSpec((D_MODEL, D_MODEL), lambda i: (0, 0)),
            pl.BlockSpec((tm, D_MODEL), lambda i: (i, 0)),
            pl.BlockSpec((1, D_MODEL), lambda i: (0, 0)),
        ],
        out_specs=pl.BlockSpec((tm, D_MODEL), lambda i: (i, 0)),
        out_shape=jax.ShapeDtypeStruct((m, D_MODEL), f32),
        compiler_params=_params(("parallel",)),
        name="outproj",
    )(merged, wo, x, g)


def _group_layouts(dt, rows):
    d = dt[:, :SSM_HEADS].reshape(rows, SSM_GROUPS, HEADS_PER_GROUP)
    return d.transpose(1, 0, 2), d.transpose(1, 2, 0)


def _layer(xp, xs, cache_k, cache_v, state_ssm, state_conv, norm_pre, w_in, conv_w, conv_b, dt_bias,
           a_log, d_skip, ssm_norm, attn_sinks, w_attn_br, w_ssm_br, w_out, norm_post):
    batch, nseq = xp.shape[0], xs.shape[0]
    mp, ms = batch * SEQ, nseq * DEC_SEQ
    w_t = w_in.T
    w_dt = jnp.pad(w_t[_SRC["dt"]:_SRC["dt"] + SSM_HEADS], ((0, LANES - SSM_HEADS), (0, 0))).astype(bf16)
    dtb = jnp.pad(dt_bias, (0, LANES - SSM_HEADS)).reshape(1, LANES)
    g_pre = norm_pre.reshape(1, D_MODEL)

    ssd_consts = _ssd_constants(conv_w, conv_b, a_log, d_skip, ssm_norm)
    wa, ws, wo = w_attn_br.astype(bf16), w_ssm_br.astype(bf16), w_out.astype(bf16)
    g_post = norm_post.reshape(1, D_MODEL)

    xp2, xs2 = xp.reshape(mp, D_MODEL), xs.reshape(ms, D_MODEL)
    ps, dts, w_main = _inproj_casting(xs2, g_pre, w_t, w_dt, dtb)
    pp, dtp = _inproj(xp2, g_pre, w_main, w_dt, dtb)

    attn_p = _attn_prompt(pp, attn_sinks, batch)
    ssm_p, st_p = _ssd_prompt(pp, *_group_layouts(dtp, mp), ssd_consts, batch)
    yp = _out(_merge(attn_p, ssm_p, pp, wa, ws), wo, xp2, g_post)
    pp3 = pp.reshape(batch, SEQ, P_WIDTH)
    k_p = pp3[:, SEQ - WINDOW:, _COL["k"]:_COL["k"] + KV_WIDTH]
    v_p = pp3[:, SEQ - WINDOW:, _COL["v"]:_COL["v"] + KV_WIDTH]
    conv_p = pp3[:, SEQ - (CONV_W - 1):, _COL["x"]:_COL["x"] + CONV_DIM]

    to_pairs = lambda t: t.transpose(0, 2, 3, 1).reshape(nseq, N_KV_HEADS // _HALF, LANES, WINDOW)
    from_pairs = lambda t: t.reshape(nseq, N_KV_HEADS, HEAD_DIM, WINDOW).transpose(0, 3, 1, 2)
    attn_s, k_s, v_s = _attn_sample(ps, attn_sinks, to_pairs(cache_k), to_pairs(cache_v))
    k_s, v_s = from_pairs(k_s), from_pairs(v_s)
    ssm_s, st_s, *new_conv = _ssd_sample(ps, *_group_layouts(dts, ms), ssd_consts, state_conv.transpose(1, 0, 2),
                                         state_ssm.reshape(nseq, SSM_WIDTH, D_STATE))
    ys = _out(_merge(attn_s, ssm_s, ps, wa, ws), wo, xs2, g_post)
    conv_s = jnp.concatenate(new_conv, axis=2).transpose(1, 0, 2)

    kv = lambda t, n: t.reshape(1, n, WINDOW, N_KV_HEADS, HEAD_DIM)
    st = lambda t, n: t.reshape(1, n, SSM_HEADS, SSM_HEAD_DIM, D_STATE)
    return (yp.reshape(xp.shape), ys.reshape(xs.shape), kv(k_p, batch), kv(v_p, batch), st(st_p, batch),
            conv_p[None], kv(k_s, nseq), kv(v_s, nseq), st(st_s, nseq), conv_s[None])


def kernel(x_prompt, x_sample, cache_k, cache_v, state_ssm, state_conv, norm_pre, w_in, conv_w, conv_b,
           dt_bias, a_log, d_skip, ssm_norm, attn_sinks, w_attn_br, w_ssm_br, w_out, norm_post):
    assert w_in.shape[0] == 1, "single-layer trunk"
    return _layer(x_prompt, x_sample, cache_k[0], cache_v[0], state_ssm[0], state_conv[0], norm_pre[0],
                  w_in[0], conv_w[0], conv_b[0], dt_bias[0], a_log[0], d_skip[0], ssm_norm[0],
                  attn_sinks[0], w_attn_br[0], w_ssm_br[0], w_out[0], norm_post[0])
```

```python
import jax
import jax.numpy as jnp
from jax import lax
from jax.experimental import pallas as pl
from jax.experimental.pallas import tpu as pltpu

f32 = jnp.float32
bf16 = jnp.bfloat16

D_MODEL = 2048
SEQ = 4096
DEC_SEQ = 8
N_HEADS = 32
N_KV_HEADS = 8
HEAD_DIM = 64
Q_PER_KV = N_HEADS // N_KV_HEADS
ATTN_WIDTH = N_HEADS * HEAD_DIM
KV_WIDTH = N_KV_HEADS * HEAD_DIM
WINDOW = 128
SSM_WIDTH = 2 * D_MODEL
SSM_HEAD_DIM = 64
SSM_HEADS = SSM_WIDTH // SSM_HEAD_DIM
SSM_GROUPS = 8
HEADS_PER_GROUP = SSM_HEADS // SSM_GROUPS
GROUP_WIDTH = HEADS_PER_GROUP * SSM_HEAD_DIM
D_STATE = 128
CONV_W = 4
BC_WIDTH = SSM_GROUPS * D_STATE
CONV_DIM = SSM_WIDTH + 2 * BC_WIDTH
CHUNK = 128
NORM_EPS = 1e-6

_SRC = dict(q=0, k=2048, v=2560, z_a=3072, xbc=5120, z_s=11264, dt=15360, g_a=15424, g_s=17472)
_COL = dict(q=0, k=2048, v=2560, z_a=3072, x=5120, B=9216, C=10240, z_s=11264, g_a=15360, g_s=17408)
P_WIDTH = 19456
HALF_ATTN = ATTN_WIDTH // 2
LANES = 128
SEQS_PER_STEP = 16

_VMEM_LIMIT = 56 * 1024 * 1024


def _params(sem):
    return pltpu.CompilerParams(dimension_semantics=sem, vmem_limit_bytes=_VMEM_LIMIT)


def _silu(v):
    half = 0.5 * v
    return half + half * jnp.tanh(half)


def _div(v, n):
    assert n & (n - 1) == 0
    return v >> (n.bit_length() - 1)


def _mod(v, n):
    assert n & (n - 1) == 0
    return v & (n - 1)


_NORM_ROWS = 128


_NT = (((1,), (1,)), ((), ()))


def _prenorm(x_ref, g_ref, wdt_ref, dtb_ref, h_ref, dt_ref):
    def norm_rows(i, carry):
        rows = pl.ds(pl.multiple_of(i * _NORM_ROWS, _NORM_ROWS), _NORM_ROWS)
        x = x_ref[rows, :]
        ms = jnp.mean(x * x, axis=-1, keepdims=True)
        h = (x * lax.rsqrt(ms + NORM_EPS) * g_ref[...]).astype(bf16)
        h_ref[rows, :] = h
        v = lax.dot_general(h, wdt_ref[...], _NT, preferred_element_type=f32) + dtb_ref[...]
        dt_ref[rows, :] = jnp.maximum(v, 0.0) + jnp.log1p(jnp.exp(-jnp.abs(v)))
        return carry

    lax.fori_loop(0, x_ref.shape[0] // _NORM_ROWS, norm_rows, 0)


def _inproj_kernel(x_ref, g_ref, w_ref, wdt_ref, dtb_ref, p_ref, dt_ref, h_ref):
    @pl.when(pl.program_id(1) == 0)
    def _():
        _prenorm(x_ref, g_ref, wdt_ref, dtb_ref, h_ref, dt_ref)

    p_ref[...] = lax.dot_general(h_ref[...], w_ref[...], _NT, preferred_element_type=f32)


def _inproj_casting_kernel(x_ref, g_ref, w32_ref, wdt_ref, dtb_ref, p_ref, dt_ref, w16_ref, h_ref):
    @pl.when(pl.program_id(1) == 0)
    def _():
        _prenorm(x_ref, g_ref, wdt_ref, dtb_ref, h_ref, dt_ref)

    w16 = w32_ref[...].astype(bf16)
    w16_ref[...] = w16
    p_ref[...] = lax.dot_general(h_ref[...], w16, _NT, preferred_element_type=f32)


def _inproj_casting(x, g, w_t, wdt, dtb, tm=1024, tn=512):
    m = x.shape[0]
    assert m == tm, "one row block, so every weight block is cast exactly once"
    main_blocks = _SRC["dt"] // tn
    skip = _SRC["g_a"] - _SRC["dt"]
    return pl.pallas_call(
        _inproj_casting_kernel,
        grid=(m // tm, P_WIDTH // tn),
        in_specs=[
            pl.BlockSpec((tm, D_MODEL), lambda i, j: (i, 0)),
            pl.BlockSpec((1, D_MODEL), lambda i, j: (0, 0)),
            pl.BlockSpec((pl.Element(tn), pl.Element(D_MODEL)),
                         lambda i, j: (pl.multiple_of(j * tn + jnp.where(j < main_blocks, 0, skip), 8), 0)),
            pl.BlockSpec((LANES, D_MODEL), lambda i, j: (0, 0)),
            pl.BlockSpec((1, LANES), lambda i, j: (0, 0)),
        ],
        out_specs=[
            pl.BlockSpec((tm, tn), lambda i, j: (i, j)),
            pl.BlockSpec((tm, LANES), lambda i, j: (i, 0)),
            pl.BlockSpec((tn, D_MODEL), lambda i, j: (j, 0)),
        ],
        out_shape=[jax.ShapeDtypeStruct((m, P_WIDTH), f32), jax.ShapeDtypeStruct((m, LANES), f32),
                   jax.ShapeDtypeStruct((P_WIDTH, D_MODEL), bf16)],
        scratch_shapes=[pltpu.VMEM((tm, D_MODEL), bf16)],
        compiler_params=_params(("arbitrary", "arbitrary")),
        name="inproj_casting",
    )(x, g, w_t, wdt, dtb)


def _inproj(x, g, w, wdt, dtb, tm=1024, tn=1024):
    m = x.shape[0]
    return pl.pallas_call(
        _inproj_kernel,
        grid=(m // tm, P_WIDTH // tn),
        in_specs=[
            pl.BlockSpec((tm, D_MODEL), lambda i, j: (i, 0)),
            pl.BlockSpec((1, D_MODEL), lambda i, j: (0, 0)),
            pl.BlockSpec((tn, D_MODEL), lambda i, j: (j, 0)),
            pl.BlockSpec((LANES, D_MODEL), lambda i, j: (0, 0)),
            pl.BlockSpec((1, LANES), lambda i, j: (0, 0)),
        ],
        out_specs=[
            pl.BlockSpec((tm, tn), lambda i, j: (i, j)),
            pl.BlockSpec((tm, LANES), lambda i, j: (i, 0)),
        ],
        out_shape=[jax.ShapeDtypeStruct((m, P_WIDTH), f32), jax.ShapeDtypeStruct((m, LANES), f32)],
        scratch_shapes=[pltpu.VMEM((tm, D_MODEL), bf16)],
        compiler_params=_params(("parallel", "arbitrary")),
        name="inproj",
    )(x, g, w, wdt, dtb)


_HALF = LANES // HEAD_DIM
assert _HALF == 2


def _alibi_slopes():
    return jnp.exp2(-8.0 * jnp.arange(1, N_HEADS + 1, dtype=f32) / N_HEADS)


def _prompt_penalty():
    s = jnp.arange(WINDOW)[:, None]
    q = jnp.arange(WINDOW)[None, :]
    dist = jnp.where(s <= q, q - s, WINDOW + q - s).astype(f32)
    pen = (_alibi_slopes()[:, None, None] * dist[None]).reshape(N_KV_HEADS, Q_PER_KV, WINDOW, WINDOW)
    return pen.transpose(0, 2, 1, 3).reshape(N_KV_HEADS, WINDOW, Q_PER_KV * WINDOW)


def _sample_penalty():
    i = jnp.arange(DEC_SEQ)[:, None]
    c = jnp.arange(2 * WINDOW)[None, :]
    dist = WINDOW + i - c
    valid = (dist >= 0) & (dist < WINDOW) & (c < WINDOW + DEC_SEQ)
    pen = _alibi_slopes()[:, None, None] * dist.astype(f32)[None]
    return jnp.where(valid[None], pen, jnp.inf).reshape(N_HEADS * DEC_SEQ, 2 * WINDOW)


_BLOCKS_PER_STEP = 2


def _attn_prompt_kernel(sink_ref, pen_ref, q_ref, kc_ref, kp_ref, vc_ref, vp_ref, za0_ref, za1_ref, a_ref):
    for sub in range(_BLOCKS_PER_STEP):
        rows = pl.ds(sub * WINDOW, WINDOW)
        before = pl.ds((sub - 1) * WINDOW, WINDOW)
        first_of_sequence = (pl.program_id(1) == 0) if sub == 0 else False
        _attn_prompt_block(
            sink_ref, pen_ref, q_ref.at[rows], kc_ref.at[rows], kp_ref if sub == 0 else kc_ref.at[before],
            vc_ref.at[rows], vp_ref if sub == 0 else vc_ref.at[before], za0_ref.at[rows], za1_ref.at[rows],
            a_ref.at[rows], jnp.where(first_of_sequence, -jnp.inf, 0.0))


def _attn_prompt_block(sink_ref, pen_ref, q_ref, kc_ref, kp_ref, vc_ref, vp_ref, za0_ref, za1_ref, a_ref, prev_off):
    cols4 = Q_PER_KV * WINDOW
    key = lax.broadcasted_iota(jnp.int32, (WINDOW, cols4), 0)
    qry = _mod(lax.broadcasted_iota(jnp.int32, (WINDOW, cols4), 1), WINDOW)
    from_cur = key <= qry
    low_half = lax.broadcasted_iota(jnp.int32, (WINDOW, LANES), 1) < HEAD_DIM
    nt = _NT
    kv_tiles = {}
    for j in range(N_KV_HEADS):
        if j % _HALF == 0:
            tile = slice((j // _HALF) * LANES, (j // _HALF + 1) * LANES)
            kv_tiles = dict(k_cur=kc_ref[:, tile].astype(bf16), k_prev=kp_ref[:, tile].astype(bf16),
                            v_cur=vc_ref[:, tile].T.astype(bf16), v_prev=vp_ref[:, tile].T.astype(bf16))
        mine = low_half == (j % _HALF == 0)
        pieces = []
        for h in range(j * Q_PER_KV, (j + 1) * Q_PER_KV):
            piece = q_ref[:, (h // _HALF) * LANES:(h // _HALF + 1) * LANES] * HEAD_DIM ** -0.5
            if h % _HALF != j % _HALF:
                piece = pltpu.roll(piece, HEAD_DIM, 1)
            pieces.append(jnp.where(mine, piece, 0.0))
        q = jnp.concatenate(pieces, axis=0).astype(bf16)
        s_cur = lax.dot_general(kv_tiles["k_cur"], q, nt, preferred_element_type=f32)
        s_prev = lax.dot_general(kv_tiles["k_prev"], q, nt, preferred_element_type=f32)
        t = jnp.where(from_cur, s_cur, s_prev + prev_off) - pen_ref[j]
        sinks = jnp.concatenate([jnp.full((1, WINDOW), sink_ref[j * Q_PER_KV + g], f32)
                                 for g in range(Q_PER_KV)], axis=1)
        m = jnp.maximum(jnp.max(t, axis=0, keepdims=True), sinks)
        p = jnp.exp(t - m)
        inv = 1.0 / (jnp.sum(p, axis=0, keepdims=True) + jnp.exp(sinks - m))
        o = jnp.dot(kv_tiles["v_cur"], jnp.where(from_cur, p, 0.0).astype(bf16), preferred_element_type=f32)
        o += jnp.dot(kv_tiles["v_prev"], jnp.where(from_cur, 0.0, p).astype(bf16), preferred_element_type=f32)
        o = o * inv
        dims = slice((j % _HALF) * HEAD_DIM, (j % _HALF + 1) * HEAD_DIM)
        for pair in range(Q_PER_KV // _HALF):
            g0 = pair * _HALF
            two_heads = jnp.concatenate([o[dims, g * WINDOW:(g + 1) * WINDOW] for g in (g0, g0 + 1)], axis=0)
            first = (j * Q_PER_KV // _HALF + pair) * LANES
            za_ref = (za0_ref, za1_ref)[first // HALF_ATTN]
            za = za_ref[:, first % HALF_ATTN:first % HALF_ATTN + LANES]
            a_ref[:, first:first + LANES] = (two_heads.T * _silu(za)).astype(a_ref.dtype)


def _attn_prompt(p, sinks, batch):
    step_rows = _BLOCKS_PER_STEP * WINDOW
    nb = SEQ // step_rows
    kcol, vcol = _COL["k"] // KV_WIDTH, _COL["v"] // KV_WIDTH
    zcol = _COL["z_a"] // HALF_ATTN
    cur = lambda b, i: b * nb + i
    prev = lambda b, i: (b * nb + i) * _BLOCKS_PER_STEP - jnp.where(i > 0, 1, 0)
    half_block = lambda col: pl.BlockSpec((step_rows, HALF_ATTN), lambda b, i: (cur(b, i), col))
    return pl.pallas_call(
        _attn_prompt_kernel,
        grid=(batch, nb),
        in_specs=[
            pl.BlockSpec(memory_space=pltpu.SMEM),
            pl.BlockSpec((N_KV_HEADS, WINDOW, Q_PER_KV * WINDOW), lambda b, i: (0, 0, 0)),
            pl.BlockSpec((step_rows, ATTN_WIDTH), lambda b, i: (cur(b, i), _COL["q"] // ATTN_WIDTH)),
            pl.BlockSpec((step_rows, KV_WIDTH), lambda b, i: (cur(b, i), kcol)),
            pl.BlockSpec((WINDOW, KV_WIDTH), lambda b, i: (prev(b, i), kcol)),
            pl.BlockSpec((step_rows, KV_WIDTH), lambda b, i: (cur(b, i), vcol)),
            pl.BlockSpec((WINDOW, KV_WIDTH), lambda b, i: (prev(b, i), vcol)),
            half_block(zcol), half_block(zcol + 1),
        ],
        out_specs=pl.BlockSpec((step_rows, ATTN_WIDTH), lambda b, i: (cur(b, i), 0)),
        out_shape=jax.ShapeDtypeStruct((batch * SEQ, ATTN_WIDTH), bf16),
        compiler_params=_params(("parallel", "parallel")),
        name="attn_prompt",
    )(sinks, _prompt_penalty(), p, p, p, p, p, p, p)


_ATTN_SEQS = 8
_SEQS_TOGETHER = 2


def _attn_sample_kernel(pen_ref, sink_ref, q_ref, kn_ref, vn_ref, za0_ref, za1_ref, ck_ref, cv_ref,
                        a_ref, ko_ref, vo_ref):
    pad = jnp.zeros((WINDOW - DEC_SEQ, KV_WIDTH), f32)
    low_half = lax.broadcasted_iota(jnp.int32, (DEC_SEQ, LANES), 1) < HEAD_DIM
    is_new = lax.broadcasted_iota(jnp.int32, (LANES, WINDOW), 1) < DEC_SEQ
    pairs = N_KV_HEADS // _HALF

    def to_half(piece, src, dst):
        return piece if src == dst else pltpu.roll(piece, HEAD_DIM, 1)

    rows_per_seq = N_HEADS * DEC_SEQ

    def scores_and_cache_update(n):
        new = pl.ds(pl.multiple_of(n * DEC_SEQ, DEC_SEQ), DEC_SEQ)
        k_new = jnp.concatenate([kn_ref[new, :], pad], axis=0)
        v_new = jnp.concatenate([vn_ref[new, :], pad], axis=0)
        kt16, vt16 = [], []
        for i in range(pairs):
            tile = slice(i * LANES, (i + 1) * LANES)
            for cache_ref, fresh, out_ref, as16 in ((ck_ref, k_new, ko_ref, kt16), (cv_ref, v_new, vo_ref, vt16)):
                old = cache_ref[n, i]
                as16.append(old.astype(bf16))
                out_ref[n, i] = pltpu.roll(jnp.where(is_new, fresh[:, tile].T, old), WINDOW - DEC_SEQ, 1)
        k_new16, v_new16 = k_new.astype(bf16), v_new.astype(bf16)
        scores = []
        for j in range(N_KV_HEADS):
            pieces = []
            for h in range(j * Q_PER_KV, (j + 1) * Q_PER_KV):
                piece = q_ref[new, (h // _HALF) * LANES:(h // _HALF + 1) * LANES] * HEAD_DIM ** -0.5
                piece = to_half(piece, h % _HALF, j % _HALF)
                pieces.append(jnp.where(low_half == (j % _HALF == 0), piece, 0.0))
            qj = jnp.concatenate(pieces, axis=0).astype(bf16)
            tile = slice((j // _HALF) * LANES, (j // _HALF + 1) * LANES)
            scores.append(jnp.concatenate(
                [jnp.dot(qj, kt16[j // _HALF], preferred_element_type=f32),
                 lax.dot_general(qj, k_new16[:, tile], _NT, preferred_element_type=f32)], axis=1))
        return jnp.concatenate(scores, axis=0) - pen_ref[...], vt16, v_new16

    def weighted_values(n, p16, inv, vt16, v_new16):
        new = pl.ds(pl.multiple_of(n * DEC_SEQ, DEC_SEQ), DEC_SEQ)
        rows_per_kv = Q_PER_KV * DEC_SEQ
        outs = []
        for j in range(N_KV_HEADS):
            rows = slice(j * rows_per_kv, (j + 1) * rows_per_kv)
            tile = slice((j // _HALF) * LANES, (j // _HALF + 1) * LANES)
            oj = lax.dot_general(p16[rows, :WINDOW], vt16[j // _HALF], _NT, preferred_element_type=f32)
            oj += jnp.dot(p16[rows, WINDOW:], v_new16[:, tile], preferred_element_type=f32)
            oj = oj * inv[rows]
            for g in range(Q_PER_KV):
                h = j * Q_PER_KV + g
                outs.append(to_half(oj[g * DEC_SEQ:(g + 1) * DEC_SEQ], j % _HALF, h % _HALF))
        o = jnp.concatenate([jnp.where(low_half, outs[h], outs[h + 1]) for h in range(0, N_HEADS, _HALF)],
                            axis=1)
        za = jnp.concatenate([za0_ref[new, :], za1_ref[new, :]], axis=1)
        a_ref[new, :] = o * _silu(za)

    def several_sequences(i, carry):
        seqs = [_SEQS_TOGETHER * i + k for k in range(_SEQS_TOGETHER)]
        staged = [scores_and_cache_update(n) for n in seqs]
        t = jnp.concatenate([s[0] for s in staged], axis=0)
        sinks = jnp.concatenate([sink_ref[...]] * len(seqs), axis=0)
        m = jnp.maximum(jnp.max(t, axis=-1, keepdims=True), sinks)
        p = jnp.exp(t - m)
        inv = 1.0 / (jnp.sum(p, axis=-1, keepdims=True) + jnp.exp(sinks - m))
        p16 = p.astype(bf16)
        for k, n in enumerate(seqs):
            rows = slice(k * rows_per_seq, (k + 1) * rows_per_seq)
            weighted_values(n, p16[rows], inv[rows], staged[k][1], staged[k][2])
        return carry

    lax.fori_loop(0, _ATTN_SEQS // _SEQS_TOGETHER, several_sequences, 0)


def _attn_sample(p, sinks, cache_k, cache_v):
    nseq = cache_k.shape[0]
    rows = _ATTN_SEQS * DEC_SEQ
    cache_spec = pl.BlockSpec((_ATTN_SEQS,) + cache_k.shape[1:], lambda s: (s, 0, 0, 0))
    sink_col = jnp.repeat(sinks.astype(f32), DEC_SEQ).reshape(N_HEADS * DEC_SEQ, 1)
    zcol = _COL["z_a"] // HALF_ATTN
    half_block = lambda col: pl.BlockSpec((rows, HALF_ATTN), lambda s: (s, col))
    return pl.pallas_call(
        _attn_sample_kernel,
        grid=(nseq // _ATTN_SEQS,),
        in_specs=[
            pl.BlockSpec((N_HEADS * DEC_SEQ, 2 * WINDOW), lambda s: (0, 0)),
            pl.BlockSpec((N_HEADS * DEC_SEQ, 1), lambda s: (0, 0)),
            pl.BlockSpec((rows, ATTN_WIDTH), lambda s: (s, _COL["q"] // ATTN_WIDTH)),
            pl.BlockSpec((rows, KV_WIDTH), lambda s: (s, _COL["k"] // KV_WIDTH)),
            pl.BlockSpec((rows, KV_WIDTH), lambda s: (s, _COL["v"] // KV_WIDTH)),
            half_block(zcol), half_block(zcol + 1),
            cache_spec, cache_spec,
        ],
        out_specs=[pl.BlockSpec((rows, ATTN_WIDTH), lambda s: (s, 0)), cache_spec, cache_spec],
        out_shape=[
            jax.ShapeDtypeStruct((nseq * DEC_SEQ, ATTN_WIDTH), f32),
            jax.ShapeDtypeStruct(cache_k.shape, f32),
            jax.ShapeDtypeStruct(cache_v.shape, f32),
        ],
        compiler_params=_params(("parallel",)),
        name="attn_sample",
    )(_sample_penalty(), sink_col, p, p, p, p, p, cache_k, cache_v)


def _expand_heads(v):
    head = lax.broadcasted_iota(jnp.int32, (HEADS_PER_GROUP, GROUP_WIDTH), 0)
    lane = lax.broadcasted_iota(jnp.int32, (HEADS_PER_GROUP, GROUP_WIDTH), 1)
    spread = jnp.where(_div(lane, SSM_HEAD_DIM) == head, 1.0, 0.0).astype(bf16)
    return sum(jnp.dot(piece, spread, preferred_element_type=f32) for piece in _split3(v))


def _split3(v):
    hi = v.astype(bf16)
    rest = v - hi.astype(f32)
    mid = rest.astype(bf16)
    return hi, mid, (rest - mid.astype(f32)).astype(bf16)


def _masked_sums(mask, v_c, v_r):
    m16 = jnp.where(mask, 1.0, 0.0).astype(bf16)
    out_c = sum(jnp.dot(m16, piece, preferred_element_type=f32) for piece in _split3(v_c))
    out_r = sum(lax.dot_general(piece, m16, (((1,), (1,)), ((), ())), preferred_element_type=f32)
                for piece in _split3(v_r))
    return out_c, out_r


def _select_rows(select, v):
    s16 = jnp.where(select, 1.0, 0.0).astype(bf16)
    return sum(jnp.dot(s16, piece, preferred_element_type=f32) for piece in _split3(v))


_ROW_BIAS, _ROW_D, _ROW_GAIN, _ROW_ALOG, _ROW_ALOG_COL = CONV_W, CONV_W + 1, CONV_W + 2, CONV_W + 3, CONV_W + 4
_CONST_ROWS = _ROW_ALOG_COL + HEADS_PER_GROUP
_CONST_WIDTH = GROUP_WIDTH + 2 * D_STATE
_X_COLS, _B_COLS, _C_COLS = (slice(0, GROUP_WIDTH), slice(GROUP_WIDTH, GROUP_WIDTH + D_STATE),
                             slice(GROUP_WIDTH + D_STATE, _CONST_WIDTH))


def _ssd_constants(conv_w, conv_b, a_log, d_skip, ssm_norm):
    grouped = lambda t, width: t.reshape(t.shape[0], SSM_GROUPS, width).transpose(1, 0, 2)
    padded = lambda t: jnp.pad(t, ((0, 0), (0, 0), (0, _CONST_WIDTH - t.shape[2])))
    taps = jnp.concatenate([conv_w, conv_b[None]], axis=0)
    top = jnp.concatenate([grouped(taps[:, :SSM_WIDTH], GROUP_WIDTH),
                           grouped(taps[:, SSM_WIDTH:SSM_WIDTH + BC_WIDTH], D_STATE),
                           grouped(taps[:, SSM_WIDTH + BC_WIDTH:], D_STATE)], axis=2)
    al = a_log.reshape(SSM_GROUPS, HEADS_PER_GROUP)
    return jnp.concatenate([top, padded(grouped(jnp.repeat(d_skip, SSM_HEAD_DIM)[None], GROUP_WIDTH)),
                            padded(grouped(ssm_norm[None], GROUP_WIDTH)),
                            padded(al[:, None, :]), padded(al[:, :, None])], axis=1)


def _intra_chunk(cb, causal, acum_c, acum_r, dt_r, xc):
    lane = lax.broadcasted_iota(jnp.int32, (CHUNK, LANES), 1)
    pieces = []
    for pair in range(HEADS_PER_GROUP // 2):
        x2 = xc[:, pair * LANES:(pair + 1) * LANES]
        acc = jnp.zeros((CHUNK, LANES), f32)
        for half in range(2):
            r = 2 * pair + half
            keep = (lane < SSM_HEAD_DIM) if half == 0 else (lane >= SSM_HEAD_DIM)
            decay = jnp.exp(jnp.where(causal, acum_c[:, r:r + 1] - acum_r[r:r + 1, :], -jnp.inf))
            acc += jnp.dot((cb * decay * dt_r[r:r + 1, :]).astype(bf16), jnp.where(keep, x2, 0.0).astype(bf16),
                           preferred_element_type=f32)
        pieces.append(acc)
    return jnp.concatenate(pieces, axis=1)


def _gated_norm(y, z, gain):
    u = y * _silu(z)
    ms = jnp.mean(u * u, axis=-1, keepdims=True)
    return u * lax.rsqrt(ms + NORM_EPS) * gain


def _conv_silu(cur, shifted, cst_ref, cols):
    y = cst_ref[_ROW_BIAS:_ROW_BIAS + 1, cols] + cst_ref[CONV_W - 1:CONV_W, cols] * cur
    for k in range(1, CONV_W):
        y = y + cst_ref[CONV_W - 1 - k:CONV_W - k, cols] * shifted[k - 1]
    return _silu(y)


def _decay_sums(mask, dt_c, dt_r, cst_ref):
    a_c = -jnp.exp(cst_ref[_ROW_ALOG:_ROW_ALOG + 1, 0:HEADS_PER_GROUP])
    a_r = -jnp.exp(cst_ref[_ROW_ALOG_COL:_ROW_ALOG_COL + HEADS_PER_GROUP, 0:1])
    return _masked_sums(mask, dt_c * a_c, dt_r * a_r)


_SSM_PARTS = SSM_WIDTH // BC_WIDTH


def _ssd_prompt_kernel(*refs):
    x_refs, refs = refs[:_SSM_PARTS], refs[_SSM_PARTS:]
    z_refs, refs = refs[:_SSM_PARTS], refs[_SSM_PARTS:]
    b_ref, c_ref, dtc_ref, dtr_ref, cst_ref, y_ref, st_ref, xpad, bpad, cpad, state = refs
    groups_per_part = SSM_GROUPS // _SSM_PARTS
    c = pl.program_id(1)
    tail = 8

    @pl.when(c == 0)
    def _():
        state[...] = jnp.zeros_like(state)
        for pad in (xpad, bpad, cpad):
            pad[:tail, :] = jnp.zeros((tail, pad.shape[1]), f32)

    @pl.when(c > 0)
    def _():
        for pad in (xpad, bpad, cpad):
            pad[:tail, :] = pad[CHUNK:CHUNK + tail, :]

    for i, part_ref in enumerate(x_refs):
        xpad[tail:, i * BC_WIDTH:(i + 1) * BC_WIDTH] = part_ref[...]
    bpad[tail:, :] = b_ref[...]
    cpad[tail:, :] = c_ref[...]
    row = lax.broadcasted_iota(jnp.int32, (CHUNK, CHUNK), 0)
    col = lax.broadcasted_iota(jnp.int32, (CHUNK, CHUNK), 1)
    causal = row >= col

    for g in range(SSM_GROUPS):
        cst = cst_ref.at[g]
        xs = slice(g * GROUP_WIDTH, (g + 1) * GROUP_WIDTH)
        ns = slice(g * D_STATE, (g + 1) * D_STATE)
        conv = []
        for pad, lanes, cols in ((xpad, xs, _X_COLS), (bpad, ns, _B_COLS), (cpad, ns, _C_COLS)):
            shifted = [pad[tail - k:tail - k + CHUNK, lanes] for k in range(1, CONV_W)]
            conv.append(_conv_silu(pad[tail:, lanes], shifted, cst, cols))
        xc, bm, cm = conv
        z = z_refs[g // groups_per_part][:, (g % groups_per_part) * GROUP_WIDTH:
                                         (g % groups_per_part + 1) * GROUP_WIDTH]

        dt_c, dt_r = dtc_ref[g], dtr_ref[g]
        acum_c, acum_r = _decay_sums(causal, dt_c, dt_r, cst)
        cb = lax.dot_general(cm.astype(bf16), bm.astype(bf16), (((1,), (1,)), ((), ())),
                             preferred_element_type=f32)
        y = _intra_chunk(cb, causal, acum_c, acum_r, dt_r, xc)

        s_in = state[g]
        decay_in = _expand_heads(jnp.exp(acum_c))
        y += decay_in * jnp.dot(cm.astype(bf16), s_in.astype(bf16), preferred_element_type=f32)
        weight_out = _expand_heads(jnp.exp(acum_c[CHUNK - 1:CHUNK, :] - acum_c) * dt_c)
        state[g] = decay_in[CHUNK - 1:CHUNK, :] * s_in + jnp.dot(
            bm.T.astype(bf16), (xc * weight_out).astype(bf16), preferred_element_type=f32)

        y += cst[_ROW_D:_ROW_D + 1, _X_COLS] * xc
        y_ref[:, xs] = _gated_norm(y, z, cst[_ROW_GAIN:_ROW_GAIN + 1, _X_COLS]).astype(y_ref.dtype)

    @pl.when(c == pl.num_programs(1) - 1)
    def _():
        for g in range(SSM_GROUPS):
            st_ref[g * GROUP_WIDTH:(g + 1) * GROUP_WIDTH, :] = state[g].T


def _ssd_prompt(p, dt_c, dt_r, consts, batch):
    nc = SEQ // CHUNK
    rb = lambda b, c: b * nc + c
    part = lambda name, i=0: pl.BlockSpec((CHUNK, BC_WIDTH), lambda b, c: (rb(b, c), _COL[name] // BC_WIDTH + i))
    return pl.pallas_call(
        _ssd_prompt_kernel,
        grid=(batch, nc),
        in_specs=[
            *[part("x", i) for i in range(_SSM_PARTS)],
            *[part("z_s", i) for i in range(_SSM_PARTS)],
            part("B"), part("C"),
            pl.BlockSpec((SSM_GROUPS, CHUNK, HEADS_PER_GROUP), lambda b, c: (0, rb(b, c), 0)),
            pl.BlockSpec((SSM_GROUPS, HEADS_PER_GROUP, CHUNK), lambda b, c: (0, 0, rb(b, c))),
            pl.BlockSpec((SSM_GROUPS, _CONST_ROWS, _CONST_WIDTH), lambda b, c: (0, 0, 0)),
        ],
        out_specs=[
            pl.BlockSpec((CHUNK, SSM_WIDTH), lambda b, c: (rb(b, c), 0)),
            pl.BlockSpec((None, SSM_WIDTH, D_STATE), lambda b, c: (b, 0, 0)),
        ],
        out_shape=[
            jax.ShapeDtypeStruct((batch * SEQ, SSM_WIDTH), bf16),
            jax.ShapeDtypeStruct((batch, SSM_WIDTH, D_STATE), f32),
        ],
        scratch_shapes=[
            pltpu.VMEM((CHUNK + 8, SSM_WIDTH), f32),
            pltpu.VMEM((CHUNK + 8, BC_WIDTH), f32),
            pltpu.VMEM((CHUNK + 8, BC_WIDTH), f32),
            pltpu.VMEM((SSM_GROUPS, D_STATE, GROUP_WIDTH), f32),
        ],
        compiler_params=_params(("parallel", "arbitrary")),
        name="ssd_prompt",
    )(*[p] * (2 * _SSM_PARTS + 2), dt_c, dt_r, consts)


def _ssd_sample_kernel(x_ref, b_ref, c_ref, z_ref, dtc_ref, dtr_ref, cst_ref,
                       cx_ref, cbuf_ref, cc_ref, s0_ref, y_ref, s1_ref, nx_ref, nb_ref, nc_ref):
    row1 = lax.broadcasted_iota(jnp.int32, (CHUNK, 1), 0)
    keep_rows = CONV_W - 1
    r = lax.broadcasted_iota(jnp.int32, (CHUNK, CHUNK), 0)
    c = lax.broadcasted_iota(jnp.int32, (CHUNK, CHUNK), 1)
    scatter = (c == _mod(r, DEC_SEQ) * SEQS_PER_STEP + _div(r, DEC_SEQ)) & (_mod(r, DEC_SEQ) < keep_rows)
    gather = ((c == _mod(r, SEQS_PER_STEP) * DEC_SEQ + DEC_SEQ - keep_rows + _div(r, SEQS_PER_STEP))
              & (r < keep_rows * SEQS_PER_STEP))
    conv = []
    for raw_ref, buf_ref, new_ref, cols in ((x_ref, cx_ref, nx_ref, _X_COLS), (b_ref, cbuf_ref, nb_ref, _B_COLS),
                                            (c_ref, cc_ref, nc_ref, _C_COLS)):
        cur = raw_ref[...]
        width = cur.shape[1]
        state_rows = jnp.concatenate([buf_ref[j] for j in range(keep_rows)]
                                     + [jnp.zeros((CHUNK - keep_rows * SEQS_PER_STEP, width), f32)], axis=0)
        cached = _select_rows(scatter, state_rows)
        moved = _select_rows(gather, cur)
        for j in range(keep_rows):
            new_ref[j] = moved[j * SEQS_PER_STEP:(j + 1) * SEQS_PER_STEP]
        shifted = []
        for k in range(1, CONV_W):
            from_cache = pltpu.roll(cached, (k - (CONV_W - 1)) % CHUNK, 0)
            shifted.append(jnp.where(_mod(row1, DEC_SEQ) >= k, pltpu.roll(cur, k, 0), from_cache))
        conv.append(_conv_silu(cur, shifted, cst_ref, cols))
    xc, bm, cm = conv

    dt_c, dt_r = dtc_ref[...], dtr_ref[...]
    row = lax.broadcasted_iota(jnp.int32, (CHUNK, CHUNK), 0)
    col = lax.broadcasted_iota(jnp.int32, (CHUNK, CHUNK), 1)
    same_seq = _div(row, DEC_SEQ) == _div(col, DEC_SEQ)
    causal = same_seq & (row >= col)
    acum_c, acum_r = _decay_sums(causal, dt_c, dt_r, cst_ref)
    cb = lax.dot_general(cm.astype(bf16), bm.astype(bf16), (((1,), (1,)), ((), ())),
                         preferred_element_type=f32)
    y = _intra_chunk(cb, causal, acum_c, acum_r, dt_r, xc)

    to_end, _ = _decay_sums(same_seq & (row < col), dt_c, dt_r, cst_ref)
    decay_in = _expand_heads(jnp.exp(acum_c))
    xw = (xc * _expand_heads(jnp.exp(to_end) * dt_c)).T.astype(bf16)
    keep = jnp.exp(acum_r)
    seq_of_row = _div(row1, DEC_SEQ)
    inter = jnp.zeros((CHUNK, GROUP_WIDTH), f32)
    for n in range(SEQS_PER_STEP):
        mine = seq_of_row == n
        s0 = s0_ref[n]
        inter += lax.dot_general(jnp.where(mine, cm, 0.0).astype(bf16), s0.astype(bf16),
                                 (((1,), (1,)), ((), ())), preferred_element_type=f32)
        update = jnp.dot(xw, jnp.where(mine, bm, 0.0).astype(bf16), preferred_element_type=f32)
        last_lane = (n + 1) * DEC_SEQ - 1
        for r in range(HEADS_PER_GROUP):
            rows = slice(r * SSM_HEAD_DIM, (r + 1) * SSM_HEAD_DIM)
            s1_ref[n, rows, :] = keep[r:r + 1, last_lane:last_lane + 1] * s0[rows] + update[rows]
    y += decay_in * inter + cst_ref[_ROW_D:_ROW_D + 1, _X_COLS] * xc
    y_ref[...] = _gated_norm(y, z_ref[...], cst_ref[_ROW_GAIN:_ROW_GAIN + 1, _X_COLS]).astype(y_ref.dtype)


def _ssd_sample(p, dt_c, dt_r, consts, conv_state, s0):
    nseq = s0.shape[0]
    nsb = nseq // SEQS_PER_STEP
    xcol, zcol = _COL["x"] // GROUP_WIDTH, _COL["z_s"] // GROUP_WIDTH
    bcol, ccol = _COL["B"] // D_STATE, _COL["C"] // D_STATE
    conv_block = lambda width, first: pl.BlockSpec(
        (CONV_W - 1, SEQS_PER_STEP, width), lambda s, g: (0, s, first + g))
    new_conv = lambda width: jax.ShapeDtypeStruct((CONV_W - 1, nseq, SSM_GROUPS * width), f32)
    state_spec = pl.BlockSpec((SEQS_PER_STEP, GROUP_WIDTH, D_STATE), lambda s, g: (s, g, 0))
    return pl.pallas_call(
        _ssd_sample_kernel,
        grid=(nsb, SSM_GROUPS),
        in_specs=[
            pl.BlockSpec((CHUNK, GROUP_WIDTH), lambda s, g: (s, xcol + g)),
            pl.BlockSpec((CHUNK, D_STATE), lambda s, g: (s, bcol + g)),
            pl.BlockSpec((CHUNK, D_STATE), lambda s, g: (s, ccol + g)),
            pl.BlockSpec((CHUNK, GROUP_WIDTH), lambda s, g: (s, zcol + g)),
            pl.BlockSpec((None, CHUNK, HEADS_PER_GROUP), lambda s, g: (g, s, 0)),
            pl.BlockSpec((None, HEADS_PER_GROUP, CHUNK), lambda s, g: (g, 0, s)),
            pl.BlockSpec((None, _CONST_ROWS, _CONST_WIDTH), lambda s, g: (g, 0, 0)),
            conv_block(GROUP_WIDTH, 0),
            conv_block(D_STATE, SSM_WIDTH // D_STATE),
            conv_block(D_STATE, (SSM_WIDTH + BC_WIDTH) // D_STATE),
            state_spec,
        ],
        out_specs=[pl.BlockSpec((CHUNK, GROUP_WIDTH), lambda s, g: (s, g)), state_spec,
                   conv_block(GROUP_WIDTH, 0), conv_block(D_STATE, 0), conv_block(D_STATE, 0)],
        out_shape=[
            jax.ShapeDtypeStruct((nseq * DEC_SEQ, SSM_WIDTH), bf16),
            jax.ShapeDtypeStruct(s0.shape, f32),
            new_conv(GROUP_WIDTH), new_conv(D_STATE), new_conv(D_STATE),
        ],
        compiler_params=_params(("parallel", "parallel")),
        name="ssd_sample",
    )(p, p, p, p, dt_c, dt_r, consts, conv_state, conv_state, conv_state, s0)


def _merge_kernel(a_ref, s_ref, ga_ref, gs_ref, wa_ref, ws_ref, o_ref):
    ya = jnp.dot(a_ref[...].astype(bf16), wa_ref[...], preferred_element_type=f32)
    ys = jnp.dot(s_ref[...], ws_ref[...], preferred_element_type=f32)
    o_ref[...] = (jax.nn.sigmoid(ga_ref[...]) * ya + jax.nn.sigmoid(gs_ref[...]) * ys).astype(o_ref.dtype)


def _merge(attn, ssm, p, wa, ws, tm=1024, tn=512):
    m = attn.shape[0]
    return pl.pallas_call(
        _merge_kernel,
        grid=(m // tm, D_MODEL // tn),
        in_specs=[
            pl.BlockSpec((tm, ATTN_WIDTH), lambda i, j: (i, 0)),
            pl.BlockSpec((tm, SSM_WIDTH), lambda i, j: (i, 0)),
            pl.BlockSpec((tm, tn), lambda i, j: (i, _COL["g_a"] // tn + j)),
            pl.BlockSpec((tm, tn), lambda i, j: (i, _COL["g_s"] // tn + j)),
            pl.BlockSpec((ATTN_WIDTH, tn), lambda i, j: (0, j)),
            pl.BlockSpec((SSM_WIDTH, tn), lambda i, j: (0, j)),
        ],
        out_specs=pl.BlockSpec((tm, tn), lambda i, j: (i, j)),
        out_shape=jax.ShapeDtypeStruct((m, D_MODEL), bf16),
        compiler_params=_params(("parallel", "arbitrary")),
        name="merge",
    )(attn, ssm, p, p, wa, ws)


def _out_kernel(m_ref, wo_ref, x_ref, g_ref, o_ref):
    y = jnp.dot(m_ref[...], wo_ref[...], preferred_element_type=f32)
    ms = jnp.mean(y * y, axis=-1, keepdims=True)
    o_ref[...] = x_ref[...] + y * lax.rsqrt(ms + NORM_EPS) * g_ref[...]


def _out(merged, wo, x, g, tm=512):
    m = x.shape[0]
    return pl.pallas_call(
        _out_kernel,
        grid=(m // tm,),
        in_specs=[
            pl.BlockSpec((tm, D_MODEL), lambda i: (i, 0)),
            pl.BlockSpec((D_MODEL, D_MODEL), lambda i: (0, 0)),
            pl.BlockSpec((tm, D_MODEL), lambda i: (i, 0)),
            pl.BlockSpec((1, D_MODEL), lambda i: (0, 0)),
        ],
        out_specs=pl.BlockSpec((tm, D_MODEL), lambda i: (i, 0)),
        out_shape=jax.ShapeDtypeStruct((m, D_MODEL), f32),
        compiler_params=_params(("parallel",)),
        name="outproj",
    )(merged, wo, x, g)


def _group_layouts(dt, rows):
    d = dt[:, :SSM_HEADS].reshape(rows, SSM_GROUPS, HEADS_PER_GROUP)
    return d.transpose(1, 0, 2), d.transpose(1, 2, 0)


def _layer(xp, xs, cache_k, cache_v, state_ssm, state_conv, norm_pre, w_in, conv_w, conv_b, dt_bias,
           a_log, d_skip, ssm_norm, attn_sinks, w_attn_br, w_ssm_br, w_out, norm_post):
    batch, nseq = xp.shape[0], xs.shape[0]
    mp, ms = batch * SEQ, nseq * DEC_SEQ
    w_t = w_in.T
    w_dt = jnp.pad(w_t[_SRC["dt"]:_SRC["dt"] + SSM_HEADS], ((0, LANES - SSM_HEADS), (0, 0))).astype(bf16)
    dtb = jnp.pad(dt_bias, (0, LANES - SSM_HEADS)).reshape(1, LANES)
    g_pre = norm_pre.reshape(1, D_MODEL)

    ssd_consts = _ssd_constants(conv_w, conv_b, a_log, d_skip, ssm_norm)
    wa, ws, wo = w_attn_br.astype(bf16), w_ssm_br.astype(bf16), w_out.astype(bf16)
    g_post = norm_post.reshape(1, D_MODEL)

    xp2, xs2 = xp.reshape(mp, D_MODEL), xs.reshape(ms, D_MODEL)
    ps, dts, w_main = _inproj_casting(xs2, g_pre, w_t, w_dt, dtb)
    pp, dtp = _inproj(xp2, g_pre, w_main, w_dt, dtb)

    attn_p = _attn_prompt(pp, attn_sinks, batch)
    ssm_p, st_p = _ssd_prompt(pp, *_group_layouts(dtp, mp), ssd_consts, batch)
    yp = _out(_merge(attn_p, ssm_p, pp, wa, ws), wo, xp2, g_post)
    pp3 = pp.reshape(batch, SEQ, P_WIDTH)
    k_p = pp3[:, SEQ - WINDOW:, _COL["k"]:_COL["k"] + KV_WIDTH]
    v_p = pp3[:, SEQ - WINDOW:, _COL["v"]:_COL["v"] + KV_WIDTH]
    conv_p = pp3[:, SEQ - (CONV_W - 1):, _COL["x"]:_COL["x"] + CONV_DIM]

    to_pairs = lambda t: t.transpose(0, 2, 3, 1).reshape(nseq, N_KV_HEADS // _HALF, LANES, WINDOW)
    from_pairs = lambda t: t.reshape(nseq, N_KV_HEADS, HEAD_DIM, WINDOW).transpose(0, 3, 1, 2)
    attn_s, k_s, v_s = _attn_sample(ps, attn_sinks, to_pairs(cache_k), to_pairs(cache_v))
    k_s, v_s = from_pairs(k_s), from_pairs(v_s)
    ssm_s, st_s, *new_conv = _ssd_sample(ps, *_group_layouts(dts, ms), ssd_consts, state_conv.transpose(1, 0, 2),
                                         state_ssm.reshape(nseq, SSM_WIDTH, D_STATE))
    ys = _out(_merge(attn_s, ssm_s, ps, wa, ws), wo, xs2, g_post)
    conv_s = jnp.concatenate(new_conv, axis=2).transpose(1, 0, 2)

    kv = lambda t, n: t.reshape(1, n, WINDOW, N_KV_HEADS, HEAD_DIM)
    st = lambda t, n: t.reshape(1, n, SSM_HEADS, SSM_HEAD_DIM, D_STATE)
    return (yp.reshape(xp.shape), ys.reshape(xs.shape), kv(k_p, batch), kv(v_p, batch), st(st_p, batch),
            conv_p[None], kv(k_s, nseq), kv(v_s, nseq), st(st_s, nseq), conv_s[None])


def kernel(x_prompt, x_sample, cache_k, cache_v, state_ssm, state_conv, norm_pre, w_in, conv_w, conv_b,
           dt_bias, a_log, d_skip, ssm_norm, attn_sinks, w_attn_br, w_ssm_br, w_out, norm_post):
    assert w_in.shape[0] == 1, "single-layer trunk"
    return _layer(x_prompt, x_sample, cache_k[0], cache_v[0], state_ssm[0], state_conv[0], norm_pre[0],
                  w_in[0], conv_w[0], conv_b[0], dt_bias[0], a_log[0], d_skip[0], ssm_norm[0],
                  attn_sinks[0], w_attn_br[0], w_ssm_br[0], w_out[0], norm_post[0])
```

```python
import jax
import jax.numpy as jnp
from jax import lax
from jax.experimental import pallas as pl
from jax.experimental.pallas import tpu as pltpu

f32 = jnp.float32
bf16 = jnp.bfloat16

D_MODEL = 2048
SEQ = 4096
DEC_SEQ = 8
N_HEADS = 32
N_KV_HEADS = 8
HEAD_DIM = 64
Q_PER_KV = N_HEADS // N_KV_HEADS
ATTN_WIDTH = N_HEADS * HEAD_DIM
KV_WIDTH = N_KV_HEADS * HEAD_DIM
WINDOW = 128
SSM_WIDTH = 2 * D_MODEL
SSM_HEAD_DIM = 64
SSM_HEADS = SSM_WIDTH // SSM_HEAD_DIM
SSM_GROUPS = 8
HEADS_PER_GROUP = SSM_HEADS // SSM_GROUPS
GROUP_WIDTH = HEADS_PER_GROUP * SSM_HEAD_DIM
D_STATE = 128
CONV_W = 4
BC_WIDTH = SSM_GROUPS * D_STATE
CONV_DIM = SSM_WIDTH + 2 * BC_WIDTH
CHUNK = 128
NORM_EPS = 1e-6

_SRC = dict(q=0, k=2048, v=2560, z_a=3072, xbc=5120, z_s=11264, dt=15360, g_a=15424, g_s=17472)
_COL = dict(q=0, k=2048, v=2560, z_a=3072, x=5120, B=9216, C=10240, z_s=11264, g_a=15360, g_s=17408)
P_WIDTH = 19456
HALF_ATTN = ATTN_WIDTH // 2
LANES = 128
SEQS_PER_STEP = 16

_VMEM_LIMIT = 56 * 1024 * 1024


def _params(sem):
    return pltpu.CompilerParams(dimension_semantics=sem, vmem_limit_bytes=_VMEM_LIMIT)


def _silu(v):
    half = 0.5 * v
    return half + half * jnp.tanh(half)


def _div(v, n):
    assert n & (n - 1) == 0
    return v >> (n.bit_length() - 1)


def _mod(v, n):
    assert n & (n - 1) == 0
    return v & (n - 1)


_NORM_ROWS = 128


_NT = (((1,), (1,)), ((), ()))


def _prenorm(x_ref, g_ref, wdt_ref, dtb_ref, h_ref, dt_ref):
    def norm_rows(i, carry):
        rows = pl.ds(pl.multiple_of(i * _NORM_ROWS, _NORM_ROWS), _NORM_ROWS)
        x = x_ref[rows, :]
        ms = jnp.mean(x * x, axis=-1, keepdims=True)
        h = (x * lax.rsqrt(ms + NORM_EPS) * g_ref[...]).astype(bf16)
        h_ref[rows, :] = h
        v = lax.dot_general(h, wdt_ref[...], _NT, preferred_element_type=f32) + dtb_ref[...]
        dt_ref[rows, :] = jnp.maximum(v, 0.0) + jnp.log1p(jnp.exp(-jnp.abs(v)))
        return carry

    lax.fori_loop(0, x_ref.shape[0] // _NORM_ROWS, norm_rows, 0)


def _inproj_kernel(x_ref, g_ref, w_ref, wdt_ref, dtb_ref, p_ref, dt_ref, h_ref):
    @pl.when(pl.program_id(1) == 0)
    def _():
        _prenorm(x_ref, g_ref, wdt_ref, dtb_ref, h_ref, dt_ref)

    p_ref[...] = lax.dot_general(h_ref[...], w_ref[...], _NT, preferred_element_type=f32)


def _inproj_casting_kernel(x_ref, g_ref, w32_ref, wdt_ref, dtb_ref, p_ref, dt_ref, w16_ref, h_ref):
    @pl.when(pl.program_id(1) == 0)
    def _():
        _prenorm(x_ref, g_ref, wdt_ref, dtb_ref, h_ref, dt_ref)

    w16 = w32_ref[...].astype(bf16)
    w16_ref[...] = w16
    p_ref[...] = lax.dot_general(h_ref[...], w16, _NT, preferred_element_type=f32)


def _inproj_casting(x, g, w_t, wdt, dtb, tm=1024, tn=512):
    m = x.shape[0]
    assert m == tm, "one row block, so every weight block is cast exactly once"
    main_blocks = _SRC["dt"] // tn
    skip = _SRC["g_a"] - _SRC["dt"]
    return pl.pallas_call(
        _inproj_casting_kernel,
        grid=(m // tm, P_WIDTH // tn),
        in_specs=[
            pl.BlockSpec((tm, D_MODEL), lambda i, j: (i, 0)),
            pl.BlockSpec((1, D_MODEL), lambda i, j: (0, 0)),
            pl.BlockSpec((pl.Element(tn), pl.Element(D_MODEL)),
                         lambda i, j: (pl.multiple_of(j * tn + jnp.where(j < main_blocks, 0, skip), 8), 0)),
            pl.BlockSpec((LANES, D_MODEL), lambda i, j: (0, 0)),
            pl.BlockSpec((1, LANES), lambda i, j: (0, 0)),
        ],
        out_specs=[
            pl.BlockSpec((tm, tn), lambda i, j: (i, j)),
            pl.BlockSpec((tm, LANES), lambda i, j: (i, 0)),
            pl.BlockSpec((tn, D_MODEL), lambda i, j: (j, 0)),
        ],
        out_shape=[jax.ShapeDtypeStruct((m, P_WIDTH), f32), jax.ShapeDtypeStruct((m, LANES), f32),
                   jax.ShapeDtypeStruct((P_WIDTH, D_MODEL), bf16)],
        scratch_shapes=[pltpu.VMEM((tm, D_MODEL), bf16)],
        compiler_params=_params(("arbitrary", "arbitrary")),
        name="inproj_casting",
    )(x, g, w_t, wdt, dtb)


def _inproj(x, g, w, wdt, dtb, tm=1024, tn=1024):
    m = x.shape[0]
    return pl.pallas_call(
        _inproj_kernel,
        grid=(m // tm, P_WIDTH // tn),
        in_specs=[
            pl.BlockSpec((tm, D_MODEL), lambda i, j: (i, 0)),
            pl.BlockSpec((1, D_MODEL), lambda i, j: (0, 0)),
            pl.BlockSpec((tn, D_MODEL), lambda i, j: (j, 0)),
            pl.BlockSpec((LANES, D_MODEL), lambda i, j: (0, 0)),
            pl.BlockSpec((1, LANES), lambda i, j: (0, 0)),
        ],
        out_specs=[
            pl.BlockSpec((tm, tn), lambda i, j: (i, j)),
            pl.BlockSpec((tm, LANES), lambda i, j: (i, 0)),
        ],
        out_shape=[jax.ShapeDtypeStruct((m, P_WIDTH), f32), jax.ShapeDtypeStruct((m, LANES), f32)],
        scratch_shapes=[pltpu.VMEM((tm, D_MODEL), bf16)],
        compiler_params=_params(("parallel", "arbitrary")),
        name="inproj",
    )(x, g, w, wdt, dtb)


_HALF = LANES // HEAD_DIM
assert _HALF == 2


def _alibi_slopes():
    return jnp.exp2(-8.0 * jnp.arange(1, N_HEADS + 1, dtype=f32) / N_HEADS)


def _prompt_penalty():
    s = jnp.arange(WINDOW)[:, None]
    q = jnp.arange(WINDOW)[None, :]
    dist = jnp.where(s <= q, q - s, WINDOW + q - s).astype(f32)
    pen = (_alibi_slopes()[:, None, None] * dist[None]).reshape(N_KV_HEADS, Q_PER_KV, WINDOW, WINDOW)
    return pen.transpose(0, 2, 1, 3).reshape(N_KV_HEADS, WINDOW, Q_PER_KV * WINDOW)


def _sample_penalty():
    i = jnp.arange(DEC_SEQ)[:, None]
    c = jnp.arange(2 * WINDOW)[None, :]
    dist = WINDOW + i - c
    valid = (dist >= 0) & (dist < WINDOW) & (c < WINDOW + DEC_SEQ)
    pen = _alibi_slopes()[:, None, None] * dist.astype(f32)[None]
    return jnp.where(valid[None], pen, jnp.inf).reshape(N_HEADS * DEC_SEQ, 2 * WINDOW)


_BLOCKS_PER_STEP = 2


def _attn_prompt_kernel(sink_ref, pen_ref, q_ref, kc_ref, kp_ref, vc_ref, vp_ref, za0_ref, za1_ref, a_ref):
    for sub in range(_BLOCKS_PER_STEP):
        rows = pl.ds(sub * WINDOW, WINDOW)
        before = pl.ds((sub - 1) * WINDOW, WINDOW)
        first_of_sequence = (pl.program_id(1) == 0) if sub == 0 else False
        _attn_prompt_block(
            sink_ref, pen_ref, q_ref.at[rows], kc_ref.at[rows], kp_ref if sub == 0 else kc_ref.at[before],
            vc_ref.at[rows], vp_ref if sub == 0 else vc_ref.at[before], za0_ref.at[rows], za1_ref.at[rows],
            a_ref.at[rows], jnp.where(first_of_sequence, -jnp.inf, 0.0))


def _attn_prompt_block(sink_ref, pen_ref, q_ref, kc_ref, kp_ref, vc_ref, vp_ref, za0_ref, za1_ref, a_ref, prev_off):
    cols4 = Q_PER_KV * WINDOW
    key = lax.broadcasted_iota(jnp.int32, (WINDOW, cols4), 0)
    qry = _mod(lax.broadcasted_iota(jnp.int32, (WINDOW, cols4), 1), WINDOW)
    from_cur = key <= qry
    low_half = lax.broadcasted_iota(jnp.int32, (WINDOW, LANES), 1) < HEAD_DIM
    nt = _NT
    kv_tiles = {}
    for j in range(N_KV_HEADS):
        if j % _HALF == 0:
            tile = slice((j // _HALF) * LANES, (j // _HALF + 1) * LANES)
            kv_tiles = dict(k_cur=kc_ref[:, tile].astype(bf16), k_prev=kp_ref[:, tile].astype(bf16),
                            v_cur=vc_ref[:, tile].T.astype(bf16), v_prev=vp_ref[:, tile].T.astype(bf16))
        mine = low_half == (j % _HALF == 0)
        pieces = []
        for h in range(j * Q_PER_KV, (j + 1) * Q_PER_KV):
            piece = q_ref[:, (h // _HALF) * LANES:(h // _HALF + 1) * LANES] * HEAD_DIM ** -0.5
            if h % _HALF != j % _HALF:
                piece = pltpu.roll(piece, HEAD_DIM, 1)
            pieces.append(jnp.where(mine, piece, 0.0))
        q = jnp.concatenate(pieces, axis=0).astype(bf16)
        s_cur = lax.dot_general(kv_tiles["k_cur"], q, nt, preferred_element_type=f32)
        s_prev = lax.dot_general(kv_tiles["k_prev"], q, nt, preferred_element_type=f32)
        t = jnp.where(from_cur, s_cur, s_prev + prev_off) - pen_ref[j]
        sinks = jnp.concatenate([jnp.full((1, WINDOW), sink_ref[j * Q_PER_KV + g], f32)
                                 for g in range(Q_PER_KV)], axis=1)
        m = jnp.maximum(jnp.max(t, axis=0, keepdims=True), sinks)
        p = jnp.exp(t - m)
        inv = 1.0 / (jnp.sum(p, axis=0, keepdims=True) + jnp.exp(sinks - m))
        o = jnp.dot(kv_tiles["v_cur"], jnp.where(from_cur, p, 0.0).astype(bf16), preferred_element_type=f32)
        o += jnp.dot(kv_tiles["v_prev"], jnp.where(from_cur, 0.0, p).astype(bf16), preferred_element_type=f32)
        o = o * inv
        dims = slice((j % _HALF) * HEAD_DIM, (j % _HALF + 1) * HEAD_DIM)
        for pair in range(Q_PER_KV // _HALF):
            g0 = pair * _HALF
            two_heads = jnp.concatenate([o[dims, g * WINDOW:(g + 1) * WINDOW] for g in (g0, g0 + 1)], axis=0)
            first = (j * Q_PER_KV // _HALF + pair) * LANES
            za_ref = (za0_ref, za1_ref)[first // HALF_ATTN]
            za = za_ref[:, first % HALF_ATTN:first % HALF_ATTN + LANES]
            a_ref[:, first:first + LANES] = (two_heads.T * _silu(za)).astype(a_ref.dtype)


def _attn_prompt(p, sinks, batch):
    step_rows = _BLOCKS_PER_STEP * WINDOW
    nb = SEQ // step_rows
    kcol, vcol = _COL["k"] // KV_WIDTH, _COL["v"] // KV_WIDTH
    zcol = _COL["z_a"] // HALF_ATTN
    cur = lambda b, i: b * nb + i
    prev = lambda b, i: (b * nb + i) * _BLOCKS_PER_STEP - jnp.where(i > 0, 1, 0)
    half_block = lambda col: pl.BlockSpec((step_rows, HALF_ATTN), lambda b, i: (cur(b, i), col))
    return pl.pallas_call(
        _attn_prompt_kernel,
        grid=(batch, nb),
        in_specs=[
            pl.BlockSpec(memory_space=pltpu.SMEM),
            pl.BlockSpec((N_KV_HEADS, WINDOW, Q_PER_KV * WINDOW), lambda b, i: (0, 0, 0)),
            pl.BlockSpec((step_rows, ATTN_WIDTH), lambda b, i: (cur(b, i), _COL["q"] // ATTN_WIDTH)),
            pl.BlockSpec((step_rows, KV_WIDTH), lambda b, i: (cur(b, i), kcol)),
            pl.BlockSpec((WINDOW, KV_WIDTH), lambda b, i: (prev(b, i), kcol)),
            pl.BlockSpec((step_rows, KV_WIDTH), lambda b, i: (cur(b, i), vcol)),
            pl.BlockSpec((WINDOW, KV_WIDTH), lambda b, i: (prev(b, i), vcol)),
            half_block(zcol), half_block(zcol + 1),
        ],
        out_specs=pl.BlockSpec((step_rows, ATTN_WIDTH), lambda b, i: (cur(b, i), 0)),
        out_shape=jax.ShapeDtypeStruct((batch * SEQ, ATTN_WIDTH), bf16),
        compiler_params=_params(("parallel", "parallel")),
        name="attn_prompt",
    )(sinks, _prompt_penalty(), p, p, p, p, p, p, p)


_ATTN_SEQS = 8
_SEQS_TOGETHER = 2


def _attn_sample_kernel(pen_ref, sink_ref, q_ref, kn_ref, vn_ref, za0_ref, za1_ref, ck_ref, cv_ref,
                        a_ref, ko_ref, vo_ref):
    pad = jnp.zeros((WINDOW - DEC_SEQ, KV_WIDTH), f32)
    low_half = lax.broadcasted_iota(jnp.int32, (DEC_SEQ, LANES), 1) < HEAD_DIM
    is_new = lax.broadcasted_iota(jnp.int32, (LANES, WINDOW), 1) < DEC_SEQ
    pairs = N_KV_HEADS // _HALF

    def to_half(piece, src, dst):
        return piece if src == dst else pltpu.roll(piece, HEAD_DIM, 1)

    rows_per_seq = N_HEADS * DEC_SEQ

    def scores_and_cache_update(n):
        new = pl.ds(pl.multiple_of(n * DEC_SEQ, DEC_SEQ), DEC_SEQ)
        k_new = jnp.concatenate([kn_ref[new, :], pad], axis=0)
        v_new = jnp.concatenate([vn_ref[new, :], pad], axis=0)
        kt16, vt16 = [], []
        for i in range(pairs):
            tile = slice(i * LANES, (i + 1) * LANES)
            for cache_ref, fresh, out_ref, as16 in ((ck_ref, k_new, ko_ref, kt16), (cv_ref, v_new, vo_ref, vt16)):
                old = cache_ref[n, i]
                as16.append(old.astype(bf16))
                out_ref[n, i] = pltpu.roll(jnp.where(is_new, fresh[:, tile].T, old), WINDOW - DEC_SEQ, 1)
        k_new16, v_new16 = k_new.astype(bf16), v_new.astype(bf16)
        scores = []
        for j in range(N_KV_HEADS):
            pieces = []
            for h in range(j * Q_PER_KV, (j + 1) * Q_PER_KV):
                piece = q_ref[new, (h // _HALF) * LANES:(h // _HALF + 1) * LANES] * HEAD_DIM ** -0.5
                piece = to_half(piece, h % _HALF, j % _HALF)
                pieces.append(jnp.where(low_half == (j % _HALF == 0), piece, 0.0))
            qj = jnp.concatenate(pieces, axis=0).astype(bf16)
            tile = slice((j // _HALF) * LANES, (j // _HALF + 1) * LANES)
            scores.append(jnp.concatenate(
                [jnp.dot(qj, kt16[j // _HALF], preferred_element_type=f32),
                 lax.dot_general(qj, k_new16[:, tile], _NT, preferred_element_type=f32)], axis=1))
        return jnp.concatenate(scores, axis=0) - pen_ref[...], vt16, v_new16

    def weighted_values(n, p16, inv, vt16, v_new16):
        new = pl.ds(pl.multiple_of(n * DEC_SEQ, DEC_SEQ), DEC_SEQ)
        rows_per_kv = Q_PER_KV * DEC_SEQ
        outs = []
        for j in range(N_KV_HEADS):
            rows = slice(j * rows_per_kv, (j + 1) * rows_per_kv)
            tile = slice((j // _HALF) * LANES, (j // _HALF + 1) * LANES)
            oj = lax.dot_general(p16[rows, :WINDOW], vt16[j // _HALF], _NT, preferred_element_type=f32)
            oj += jnp.dot(p16[rows, WINDOW:], v_new16[:, tile], preferred_element_type=f32)
            oj = oj * inv[rows]
            for g in range(Q_PER_KV):
                h = j * Q_PER_KV + g
                outs.append(to_half(oj[g * DEC_SEQ:(g + 1) * DEC_SEQ], j % _HALF, h % _HALF))
        o = jnp.concatenate([jnp.where(low_half, outs[h], outs[h + 1]) for h in range(0, N_HEADS, _HALF)],
                            axis=1)
        za = jnp.concatenate([za0_ref[new, :], za1_ref[new, :]], axis=1)
        a_ref[new, :] = o * _silu(za)

    def several_sequences(i, carry):
        seqs = [_SEQS_TOGETHER * i + k for k in range(_SEQS_TOGETHER)]
        staged = [scores_and_cache_update(n) for n in seqs]
        t = jnp.concatenate([s[0] for s in staged], axis=0)
        sinks = jnp.concatenate([sink_ref[...]] * len(seqs), axis=0)
        m = jnp.maximum(jnp.max(t, axis=-1, keepdims=True), sinks)
        p = jnp.exp(t - m)
        inv = 1.0 / (jnp.sum(p, axis=-1, keepdims=True) + jnp.exp(sinks - m))
        p16 = p.astype(bf16)
        for k, n in enumerate(seqs):
            rows = slice(k * rows_per_seq, (k + 1) * rows_per_seq)
            weighted_values(n, p16[rows], inv[rows], staged[k][1], staged[k][2])
        return carry

    lax.fori_loop(0, _ATTN_SEQS // _SEQS_TOGETHER, several_sequences, 0)


def _attn_sample(p, sinks, cache_k, cache_v):
    nseq = cache_k.shape[0]
    rows = _ATTN_SEQS * DEC_SEQ
    cache_spec = pl.BlockSpec((_ATTN_SEQS,) + cache_k.shape[1:], lambda s: (s, 0, 0, 0))
    sink_col = jnp.repeat(sinks.astype(f32), DEC_SEQ).reshape(N_HEADS * DEC_SEQ, 1)
    zcol = _COL["z_a"] // HALF_ATTN
    half_block = lambda col: pl.BlockSpec((rows, HALF_ATTN), lambda s: (s, col))
    return pl.pallas_call(
        _attn_sample_kernel,
        grid=(nseq // _ATTN_SEQS,),
        in_specs=[
            pl.BlockSpec((N_HEADS * DEC_SEQ, 2 * WINDOW), lambda s: (0, 0)),
            pl.BlockSpec((N_HEADS * DEC_SEQ, 1), lambda s: (0, 0)),
            pl.BlockSpec((rows, ATTN_WIDTH), lambda s: (s, _COL["q"] // ATTN_WIDTH)),
            pl.BlockSpec((rows, KV_WIDTH), lambda s: (s, _COL["k"] // KV_WIDTH)),
            pl.BlockSpec((rows, KV_WIDTH), lambda s: (s, _COL["v"] // KV_WIDTH)),
            half_block(zcol), half_block(zcol + 1),
            cache_spec, cache_spec,
        ],
        out_specs=[pl.BlockSpec((rows, ATTN_WIDTH), lambda s: (s, 0)), cache_spec, cache_spec],
        out_shape=[
            jax.ShapeDtypeStruct((nseq * DEC_SEQ, ATTN_WIDTH), f32),
            jax.ShapeDtypeStruct(cache_k.shape, f32),
            jax.ShapeDtypeStruct(cache_v.shape, f32),
        ],
        compiler_params=_params(("parallel",)),
        name="attn_sample",
    )(_sample_penalty(), sink_col, p, p, p, p, p, cache_k, cache_v)


def _expand_heads(v):
    head = lax.broadcasted_iota(jnp.int32, (HEADS_PER_GROUP, GROUP_WIDTH), 0)
    lane = lax.broadcasted_iota(jnp.int32, (HEADS_PER_GROUP, GROUP_WIDTH), 1)
    spread = jnp.where(_div(lane, SSM_HEAD_DIM) == head, 1.0, 0.0).astype(bf16)
    return sum(jnp.dot(piece, spread, preferred_element_type=f32) for piece in _split3(v))


def _split3(v):
    hi = v.astype(bf16)
    rest = v - hi.astype(f32)
    mid = rest.astype(bf16)
    return hi, mid, (rest - mid.astype(f32)).astype(bf16)


def _masked_sums(mask, v_c, v_r):
    m16 = jnp.where(mask, 1.0, 0.0).astype(bf16)
    out_c = sum(jnp.dot(m16, piece, preferred_element_type=f32) for piece in _split3(v_c))
    out_r = sum(lax.dot_general(piece, m16, (((1,), (1,)), ((), ())), preferred_element_type=f32)
                for piece in _split3(v_r))
    return out_c, out_r


def _select_rows(select, v):
    s16 = jnp.where(select, 1.0, 0.0).astype(bf16)
    return sum(jnp.dot(s16, piece, preferred_element_type=f32) for piece in _split3(v))


_ROW_BIAS, _ROW_D, _ROW_GAIN, _ROW_ALOG, _ROW_ALOG_COL = CONV_W, CONV_W + 1, CONV_W + 2, CONV_W + 3, CONV_W + 4
_CONST_ROWS = _ROW_ALOG_COL + HEADS_PER_GROUP
_CONST_WIDTH = GROUP_WIDTH + 2 * D_STATE
_X_COLS, _B_COLS, _C_COLS = (slice(0, GROUP_WIDTH), slice(GROUP_WIDTH, GROUP_WIDTH + D_STATE),
                             slice(GROUP_WIDTH + D_STATE, _CONST_WIDTH))


def _ssd_constants(conv_w, conv_b, a_log, d_skip, ssm_norm):
    grouped = lambda t, width: t.reshape(t.shape[0], SSM_GROUPS, width).transpose(1, 0, 2)
    padded = lambda t: jnp.pad(t, ((0, 0), (0, 0), (0, _CONST_WIDTH - t.shape[2])))
    taps = jnp.concatenate([conv_w, conv_b[None]], axis=0)
    top = jnp.concatenate([grouped(taps[:, :SSM_WIDTH], GROUP_WIDTH),
                           grouped(taps[:, SSM_WIDTH:SSM_WIDTH + BC_WIDTH], D_STATE),
                           grouped(taps[:, SSM_WIDTH + BC_WIDTH:], D_STATE)], axis=2)
    al = a_log.reshape(SSM_GROUPS, HEADS_PER_GROUP)
    return jnp.concatenate([top, padded(grouped(jnp.repeat(d_skip, SSM_HEAD_DIM)[None], GROUP_WIDTH)),
                            padded(grouped(ssm_norm[None], GROUP_WIDTH)),
                            padded(al[:, None, :]), padded(al[:, :, None])], axis=1)


def _intra_chunk(cb, causal, acum_c, acum_r, dt_r, xc):
    lane = lax.broadcasted_iota(jnp.int32, (CHUNK, LANES), 1)
    pieces = []
    for pair in range(HEADS_PER_GROUP // 2):
        x2 = xc[:, pair * LANES:(pair + 1) * LANES]
        acc = jnp.zeros((CHUNK, LANES), f32)
        for half in range(2):
            r = 2 * pair + half
            keep = (lane < SSM_HEAD_DIM) if half == 0 else (lane >= SSM_HEAD_DIM)
            decay = jnp.exp(jnp.where(causal, acum_c[:, r:r + 1] - acum_r[r:r + 1, :], -jnp.inf))
            acc += jnp.dot((cb * decay * dt_r[r:r + 1, :]).astype(bf16), jnp.where(keep, x2, 0.0).astype(bf16),
                           preferred_element_type=f32)
        pieces.append(acc)
    return jnp.concatenate(pieces, axis=1)


def _gated_norm(y, z, gain):
    u = y * _silu(z)
    ms = jnp.mean(u * u, axis=-1, keepdims=True)
    return u * lax.rsqrt(ms + NORM_EPS) * gain


def _conv_silu(cur, shifted, cst_ref, cols):
    y = cst_ref[_ROW_BIAS:_ROW_BIAS + 1, cols] + cst_ref[CONV_W - 1:CONV_W, cols] * cur
    for k in range(1, CONV_W):
        y = y + cst_ref[CONV_W - 1 - k:CONV_W - k, cols] * shifted[k - 1]
    return _silu(y)


def _decay_sums(mask, dt_c, dt_r, cst_ref):
    a_c = -jnp.exp(cst_ref[_ROW_ALOG:_ROW_ALOG + 1, 0:HEADS_PER_GROUP])
    a_r = -jnp.exp(cst_ref[_ROW_ALOG_COL:_ROW_ALOG_COL + HEADS_PER_GROUP, 0:1])
    return _masked_sums(mask, dt_c * a_c, dt_r * a_r)


_SSM_PARTS = SSM_WIDTH // BC_WIDTH


def _ssd_prompt_kernel(*refs):
    x_refs, refs = refs[:_SSM_PARTS], refs[_SSM_PARTS:]
    z_refs, refs = refs[:_SSM_PARTS], refs[_SSM_PARTS:]
    b_ref, c_ref, dtc_ref, dtr_ref, cst_ref, y_ref, st_ref, xpad, bpad, cpad, state = refs
    groups_per_part = SSM_GROUPS // _SSM_PARTS
    c = pl.program_id(1)
    tail = 8

    @pl.when(c == 0)
    def _():
        state[...] = jnp.zeros_like(state)
        for pad in (xpad, bpad, cpad):
            pad[:tail, :] = jnp.zeros((tail, pad.shape[1]), f32)

    @pl.when(c > 0)
    def _():
        for pad in (xpad, bpad, cpad):
            pad[:tail, :] = pad[CHUNK:CHUNK + tail, :]

    for i, part_ref in enumerate(x_refs):
        xpad[tail:, i * BC_WIDTH:(i + 1) * BC_WIDTH] = part_ref[...]
    bpad[tail:, :] = b_ref[...]
    cpad[tail:, :] = c_ref[...]
    row = lax.broadcasted_iota(jnp.int32, (CHUNK, CHUNK), 0)
    col = lax.broadcasted_iota(jnp.int32, (CHUNK, CHUNK), 1)
    causal = row >= col

    for g in range(SSM_GROUPS):
        cst = cst_ref.at[g]
        xs = slice(g * GROUP_WIDTH, (g + 1) * GROUP_WIDTH)
        ns = slice(g * D_STATE, (g + 1) * D_STATE)
        conv = []
        for pad, lanes, cols in ((xpad, xs, _X_COLS), (bpad, ns, _B_COLS), (cpad, ns, _C_COLS)):
            shifted = [pad[tail - k:tail - k + CHUNK, lanes] for k in range(1, CONV_W)]
            conv.append(_conv_silu(pad[tail:, lanes], shifted, cst, cols))
        xc, bm, cm = conv
        z = z_refs[g // groups_per_part][:, (g % groups_per_part) * GROUP_WIDTH:
                                         (g % groups_per_part + 1) * GROUP_WIDTH]

        dt_c, dt_r = dtc_ref[g], dtr_ref[g]
        acum_c, acum_r = _decay_sums(causal, dt_c, dt_r, cst)
        cb = lax.dot_general(cm.astype(bf16), bm.astype(bf16), (((1,), (1,)), ((), ())),
                             preferred_element_type=f32)
        y = _intra_chunk(cb, causal, acum_c, acum_r, dt_r, xc)

        s_in = state[g]
        decay_in = _expand_heads(jnp.exp(acum_c))
        y += decay_in * jnp.dot(cm.astype(bf16), s_in.astype(bf16), preferred_element_type=f32)
        weight_out = _expand_heads(jnp.exp(acum_c[CHUNK - 1:CHUNK, :] - acum_c) * dt_c)
        state[g] = decay_in[CHUNK - 1:CHUNK, :] * s_in + jnp.dot(
            bm.T.astype(bf16), (xc * weight_out).astype(bf16), preferred_element_type=f32)

        y += cst[_ROW_D:_ROW_D + 1, _X_COLS] * xc
        y_ref[:, xs] = _gated_norm(y, z, cst[_ROW_GAIN:_ROW_GAIN + 1, _X_COLS]).astype(y_ref.dtype)

    @pl.when(c == pl.num_programs(1) - 1)
    def _():
        for g in range(SSM_GROUPS):
            st_ref[g * GROUP_WIDTH:(g + 1) * GROUP_WIDTH, :] = state[g].T


def _ssd_prompt(p, dt_c, dt_r, consts, batch):
    nc = SEQ // CHUNK
    rb = lambda b, c: b * nc + c
    part = lambda name, i=0: pl.BlockSpec((CHUNK, BC_WIDTH), lambda b, c: (rb(b, c), _COL[name] // BC_WIDTH + i))
    return pl.pallas_call(
        _ssd_prompt_kernel,
        grid=(batch, nc),
        in_specs=[
            *[part("x", i) for i in range(_SSM_PARTS)],
            *[part("z_s", i) for i in range(_SSM_PARTS)],
            part("B"), part("C"),
            pl.BlockSpec((SSM_GROUPS, CHUNK, HEADS_PER_GROUP), lambda b, c: (0, rb(b, c), 0)),
            pl.BlockSpec((SSM_GROUPS, HEADS_PER_GROUP, CHUNK), lambda b, c: (0, 0, rb(b, c))),
            pl.BlockSpec((SSM_GROUPS, _CONST_ROWS, _CONST_WIDTH), lambda b, c: (0, 0, 0)),
        ],
        out_specs=[
            pl.BlockSpec((CHUNK, SSM_WIDTH), lambda b, c: (rb(b, c), 0)),
            pl.BlockSpec((None, SSM_WIDTH, D_STATE), lambda b, c: (b, 0, 0)),
        ],
        out_shape=[
            jax.ShapeDtypeStruct((batch * SEQ, SSM_WIDTH), bf16),
            jax.ShapeDtypeStruct((batch, SSM_WIDTH, D_STATE), f32),
        ],
        scratch_shapes=[
            pltpu.VMEM((CHUNK + 8, SSM_WIDTH), f32),
            pltpu.VMEM((CHUNK + 8, BC_WIDTH), f32),
            pltpu.VMEM((CHUNK + 8, BC_WIDTH), f32),
            pltpu.VMEM((SSM_GROUPS, D_STATE, GROUP_WIDTH), f32),
        ],
        compiler_params=_params(("parallel", "arbitrary")),
        name="ssd_prompt",
    )(*[p] * (2 * _SSM_PARTS + 2), dt_c, dt_r, consts)


_STATE_BUFFERS = 3
_STATE_LOOKAHEAD = _STATE_BUFFERS - 1


def _ssd_sample_kernel(x_ref, b_ref, c_ref, z_ref, dtc_ref, dtr_ref, cst_ref,
                       cx_ref, cbuf_ref, cc_ref, s0_hbm, y_ref, s1_ref, nx_ref, nb_ref, nc_ref, s0_ring, s0_sem):
    steps = pl.num_programs(0) * SSM_GROUPS
    step = pl.program_id(0) * SSM_GROUPS + pl.program_id(1)

    def state_copy(t, slot):
        block, group = t // SSM_GROUPS, t % SSM_GROUPS
        return pltpu.make_async_copy(
            s0_hbm.at[pl.ds(block * SEQS_PER_STEP, SEQS_PER_STEP), pl.ds(group * GROUP_WIDTH, GROUP_WIDTH), :],
            s0_ring.at[slot], s0_sem.at[slot])

    @pl.when(step == 0)
    def _():
        for t in range(_STATE_LOOKAHEAD):
            state_copy(t, t).start()

    @pl.when(step + _STATE_LOOKAHEAD < steps)
    def _():
        state_copy(step + _STATE_LOOKAHEAD, (step + _STATE_LOOKAHEAD) % _STATE_BUFFERS).start()

    slot = step % _STATE_BUFFERS
    state_copy(step, slot).wait()
    s0_ref = s0_ring.at[slot]
    row1 = lax.broadcasted_iota(jnp.int32, (CHUNK, 1), 0)
    keep_rows = CONV_W - 1
    r = lax.broadcasted_iota(jnp.int32, (CHUNK, CHUNK), 0)
    c = lax.broadcasted_iota(jnp.int32, (CHUNK, CHUNK), 1)
    scatter = (c == _mod(r, DEC_SEQ) * SEQS_PER_STEP + _div(r, DEC_SEQ)) & (_mod(r, DEC_SEQ) < keep_rows)
    gather = ((c == _mod(r, SEQS_PER_STEP) * DEC_SEQ + DEC_SEQ - keep_rows + _div(r, SEQS_PER_STEP))
              & (r < keep_rows * SEQS_PER_STEP))
    conv = []
    for raw_ref, buf_ref, new_ref, cols in ((x_ref, cx_ref, nx_ref, _X_COLS), (b_ref, cbuf_ref, nb_ref, _B_COLS),
                                            (c_ref, cc_ref, nc_ref, _C_COLS)):
        cur = raw_ref[...]
        width = cur.shape[1]
        state_rows = jnp.concatenate([buf_ref[j] for j in range(keep_rows)]
                                     + [jnp.zeros((CHUNK - keep_rows * SEQS_PER_STEP, width), f32)], axis=0)
        cached = _select_rows(scatter, state_rows)
        moved = _select_rows(gather, cur)
        for j in range(keep_rows):
            new_ref[j] = moved[j * SEQS_PER_STEP:(j + 1) * SEQS_PER_STEP]
        shifted = []
        for k in range(1, CONV_W):
            from_cache = pltpu.roll(cached, (k - (CONV_W - 1)) % CHUNK, 0)
            shifted.append(jnp.where(_mod(row1, DEC_SEQ) >= k, pltpu.roll(cur, k, 0), from_cache))
        conv.append(_conv_silu(cur, shifted, cst_ref, cols))
    xc, bm, cm = conv

    dt_c, dt_r = dtc_ref[...], dtr_ref[...]
    row = lax.broadcasted_iota(jnp.int32, (CHUNK, CHUNK), 0)
    col = lax.broadcasted_iota(jnp.int32, (CHUNK, CHUNK), 1)
    same_seq = _div(row, DEC_SEQ) == _div(col, DEC_SEQ)
    causal = same_seq & (row >= col)
    acum_c, acum_r = _decay_sums(causal, dt_c, dt_r, cst_ref)
    cb = lax.dot_general(cm.astype(bf16), bm.astype(bf16), (((1,), (1,)), ((), ())),
                         preferred_element_type=f32)
    y = _intra_chunk(cb, causal, acum_c, acum_r, dt_r, xc)

    to_end, _ = _decay_sums(same_seq & (row < col), dt_c, dt_r, cst_ref)
    decay_in = _expand_heads(jnp.exp(acum_c))
    xw = (xc * _expand_heads(jnp.exp(to_end) * dt_c)).T.astype(bf16)
    keep = jnp.exp(acum_r)
    seq_of_row = _div(row1, DEC_SEQ)
    inter = jnp.zeros((CHUNK, GROUP_WIDTH), f32)
    for n in range(SEQS_PER_STEP):
        mine = seq_of_row == n
        s0 = s0_ref[n]
        inter += lax.dot_general(jnp.where(mine, cm, 0.0).astype(bf16), s0.astype(bf16),
                                 (((1,), (1,)), ((), ())), preferred_element_type=f32)
        update = jnp.dot(xw, jnp.where(mine, bm, 0.0).astype(bf16), preferred_element_type=f32)
        last_lane = (n + 1) * DEC_SEQ - 1
        for r in range(HEADS_PER_GROUP):
            rows = slice(r * SSM_HEAD_DIM, (r + 1) * SSM_HEAD_DIM)
            s1_ref[n, rows, :] = keep[r:r + 1, last_lane:last_lane + 1] * s0[rows] + update[rows]
    y += decay_in * inter + cst_ref[_ROW_D:_ROW_D + 1, _X_COLS] * xc
    y_ref[...] = _gated_norm(y, z_ref[...], cst_ref[_ROW_GAIN:_ROW_GAIN + 1, _X_COLS]).astype(y_ref.dtype)


def _ssd_sample(p, dt_c, dt_r, consts, conv_state, s0):
    nseq = s0.shape[0]
    nsb = nseq // SEQS_PER_STEP
    xcol, zcol = _COL["x"] // GROUP_WIDTH, _COL["z_s"] // GROUP_WIDTH
    bcol, ccol = _COL["B"] // D_STATE, _COL["C"] // D_STATE
    conv_block = lambda width, first: pl.BlockSpec(
        (CONV_W - 1, SEQS_PER_STEP, width), lambda s, g: (0, s, first + g))
    new_conv = lambda width: jax.ShapeDtypeStruct((CONV_W - 1, nseq, SSM_GROUPS * width), f32)
    state_spec = pl.BlockSpec((SEQS_PER_STEP, GROUP_WIDTH, D_STATE), lambda s, g: (s, g, 0))
    return pl.pallas_call(
        _ssd_sample_kernel,
        grid=(nsb, SSM_GROUPS),
        in_specs=[
            pl.BlockSpec((CHUNK, GROUP_WIDTH), lambda s, g: (s, xcol + g)),
            pl.BlockSpec((CHUNK, D_STATE), lambda s, g: (s, bcol + g)),
            pl.BlockSpec((CHUNK, D_STATE), lambda s, g: (s, ccol + g)),
            pl.BlockSpec((CHUNK, GROUP_WIDTH), lambda s, g: (s, zcol + g)),
            pl.BlockSpec((None, CHUNK, HEADS_PER_GROUP), lambda s, g: (g, s, 0)),
            pl.BlockSpec((None, HEADS_PER_GROUP, CHUNK), lambda s, g: (g, 0, s)),
            pl.BlockSpec((None, _CONST_ROWS, _CONST_WIDTH), lambda s, g: (g, 0, 0)),
            conv_block(GROUP_WIDTH, 0),
            conv_block(D_STATE, SSM_WIDTH // D_STATE),
            conv_block(D_STATE, (SSM_WIDTH + BC_WIDTH) // D_STATE),
            pl.BlockSpec(memory_space=pl.ANY),
        ],
        out_specs=[pl.BlockSpec((CHUNK, GROUP_WIDTH), lambda s, g: (s, g)), state_spec,
                   conv_block(GROUP_WIDTH, 0), conv_block(D_STATE, 0), conv_block(D_STATE, 0)],
        out_shape=[
            jax.ShapeDtypeStruct((nseq * DEC_SEQ, SSM_WIDTH), bf16),
            jax.ShapeDtypeStruct(s0.shape, f32),
            new_conv(GROUP_WIDTH), new_conv(D_STATE), new_conv(D_STATE),
        ],
        scratch_shapes=[pltpu.VMEM((_STATE_BUFFERS, SEQS_PER_STEP, GROUP_WIDTH, D_STATE), f32),
                        pltpu.SemaphoreType.DMA((_STATE_BUFFERS,))],
        compiler_params=_params(("arbitrary", "arbitrary")),
        name="ssd_sample",
    )(p, p, p, p, dt_c, dt_r, consts, conv_state, conv_state, conv_state, s0)


def _merge_kernel(a_ref, s_ref, ga_ref, gs_ref, wa_ref, ws_ref, o_ref):
    ya = jnp.dot(a_ref[...].astype(bf16), wa_ref[...], preferred_element_type=f32)
    ys = jnp.dot(s_ref[...], ws_ref[...], preferred_element_type=f32)
    o_ref[...] = (jax.nn.sigmoid(ga_ref[...]) * ya + jax.nn.sigmoid(gs_ref[...]) * ys).astype(o_ref.dtype)


def _merge(attn, ssm, p, wa, ws, tm=1024, tn=512):
    m = attn.shape[0]
    return pl.pallas_call(
        _merge_kernel,
        grid=(m // tm, D_MODEL // tn),
        in_specs=[
            pl.BlockSpec((tm, ATTN_WIDTH), lambda i, j: (i, 0)),
            pl.BlockSpec((tm, SSM_WIDTH), lambda i, j: (i, 0)),
            pl.BlockSpec((tm, tn), lambda i, j: (i, _COL["g_a"] // tn + j)),
            pl.BlockSpec((tm, tn), lambda i, j: (i, _COL["g_s"] // tn + j)),
            pl.BlockSpec((ATTN_WIDTH, tn), lambda i, j: (0, j)),
            pl.BlockSpec((SSM_WIDTH, tn), lambda i, j: (0, j)),
        ],
        out_specs=pl.BlockSpec((tm, tn), lambda i, j: (i, j)),
        out_shape=jax.ShapeDtypeStruct((m, D_MODEL), bf16),
        compiler_params=_params(("parallel", "arbitrary")),
        name="merge",
    )(attn, ssm, p, p, wa, ws)


def _out_kernel(m_ref, wo_ref, x_ref, g_ref, o_ref):
    y = jnp.dot(m_ref[...], wo_ref[...], preferred_element_type=f32)
    ms = jnp.mean(y * y, axis=-1, keepdims=True)
    o_ref[...] = x_ref[...] + y * lax.rsqrt(ms + NORM_EPS) * g_ref[...]


def _out(merged, wo, x, g, tm=512):
    m = x.shape[0]
    return pl.pallas_call(
        _out_kernel,
        grid=(m // tm,),
        in_specs=[
            pl.BlockSpec((tm, D_MODEL), lambda i: (i, 0)),
            pl.BlockSpec((D_MODEL, D_MODEL), lambda i: (0, 0)),
            pl.BlockSpec((tm, D_MODEL), lambda i: (i, 0)),
            pl.BlockSpec((1, D_MODEL), lambda i: (0, 0)),
        ],
        out_specs=pl.BlockSpec((tm, D_MODEL), lambda i: (i, 0)),
        out_shape=jax.ShapeDtypeStruct((m, D_MODEL), f32),
        compiler_params=_params(("parallel",)),
        name="outproj",
    )(merged, wo, x, g)


def _group_layouts(dt, rows):
    d = dt[:, :SSM_HEADS].reshape(rows, SSM_GROUPS, HEADS_PER_GROUP)
    return d.transpose(1, 0, 2), d.transpose(1, 2, 0)


def _layer(xp, xs, cache_k, cache_v, state_ssm, state_conv, norm_pre, w_in, conv_w, conv_b, dt_bias,
           a_log, d_skip, ssm_norm, attn_sinks, w_attn_br, w_ssm_br, w_out, norm_post):
    batch, nseq = xp.shape[0], xs.shape[0]
    mp, ms = batch * SEQ, nseq * DEC_SEQ
    w_t = w_in.T
    w_dt = jnp.pad(w_t[_SRC["dt"]:_SRC["dt"] + SSM_HEADS], ((0, LANES - SSM_HEADS), (0, 0))).astype(bf16)
    dtb = jnp.pad(dt_bias, (0, LANES - SSM_HEADS)).reshape(1, LANES)
    g_pre = norm_pre.reshape(1, D_MODEL)

    ssd_consts = _ssd_constants(conv_w, conv_b, a_log, d_skip, ssm_norm)
    wa, ws, wo = w_attn_br.astype(bf16), w_ssm_br.astype(bf16), w_out.astype(bf16)
    g_post = norm_post.reshape(1, D_MODEL)

    xp2, xs2 = xp.reshape(mp, D_MODEL), xs.reshape(ms, D_MODEL)
    ps, dts, w_main = _inproj_casting(xs2, g_pre, w_t, w_dt, dtb)
    pp, dtp = _inproj(xp2, g_pre, w_main, w_dt, dtb)

    attn_p = _attn_prompt(pp, attn_sinks, batch)
    ssm_p, st_p = _ssd_prompt(pp, *_group_layouts(dtp, mp), ssd_consts, batch)
    yp = _out(_merge(attn_p, ssm_p, pp, wa, ws), wo, xp2, g_post)
    pp3 = pp.reshape(batch, SEQ, P_WIDTH)
    k_p = pp3[:, SEQ - WINDOW:, _COL["k"]:_COL["k"] + KV_WIDTH]
    v_p = pp3[:, SEQ - WINDOW:, _COL["v"]:_COL["v"] + KV_WIDTH]
    conv_p = pp3[:, SEQ - (CONV_W - 1):, _COL["x"]:_COL["x"] + CONV_DIM]

    to_pairs = lambda t: t.transpose(0, 2, 3, 1).reshape(nseq, N_KV_HEADS // _HALF, LANES, WINDOW)
    from_pairs = lambda t: t.reshape(nseq, N_KV_HEADS, HEAD_DIM, WINDOW).transpose(0, 3, 1, 2)
    attn_s, k_s, v_s = _attn_sample(ps, attn_sinks, to_pairs(cache_k), to_pairs(cache_v))
    k_s, v_s = from_pairs(k_s), from_pairs(v_s)
    ssm_s, st_s, *new_conv = _ssd_sample(ps, *_group_layouts(dts, ms), ssd_consts, state_conv.transpose(1, 0, 2),
                                         state_ssm.reshape(nseq, SSM_WIDTH, D_STATE))
    ys = _out(_merge(attn_s, ssm_s, ps, wa, ws), wo, xs2, g_post)
    conv_s = jnp.concatenate(new_conv, axis=2).transpose(1, 0, 2)

    kv = lambda t, n: t.reshape(1, n, WINDOW, N_KV_HEADS, HEAD_DIM)
    st = lambda t, n: t.reshape(1, n, SSM_HEADS, SSM_HEAD_DIM, D_STATE)
    return (yp.reshape(xp.shape), ys.reshape(xs.shape), kv(k_p, batch), kv(v_p, batch), st(st_p, batch),
            conv_p[None], kv(k_s, nseq), kv(v_s, nseq), st(st_s, nseq), conv_s[None])


def kernel(x_prompt, x_sample, cache_k, cache_v, state_ssm, state_conv, norm_pre, w_in, conv_w, conv_b,
           dt_bias, a_log, d_skip, ssm_norm, attn_sinks, w_attn_br, w_ssm_br, w_out, norm_post):
    assert w_in.shape[0] == 1, "single-layer trunk"
    return _layer(x_prompt, x_sample, cache_k[0], cache_v[0], state_ssm[0], state_conv[0], norm_pre[0],
                  w_in[0], conv_w[0], conv_b[0], dt_bias[0], a_log[0], d_skip[0], ssm_norm[0],
                  attn_sinks[0], w_attn_br[0], w_ssm_br[0], w_out[0], norm_post[0])
```
